```python
import math
import jax, jax.numpy as jnp
from jax import lax
import numpy as np

D_MODEL = 1024
BATCH = 8
SEQ = 8192
DEPTH = 4

CHUNK = 64
N_MIXERS = 3
N_SB_LAYERS = (DEPTH + 2) // 3
N_S5_LAYERS = (DEPTH + 1) // 3
N_CV_LAYERS = DEPTH // 3
SB_HEADS = 16
SB_HEAD_DIM = D_MODEL // SB_HEADS
Q_BLOCK = 128
S5_GROUP = 16
S5_GROUPS = D_MODEL // S5_GROUP
S5_STATE = 64
S5_DT_MIN = 1e-3
S5_DT_MAX = 1e-1
CONV_WIDTH = 31
D_FF = ((8 * D_MODEL + 2) // 3 + 255) // 256 * 256
EPS = 1e-6

kernel_name = "hybrid_stickbreak_s5_conformer_trunk"


def rms_norm(x, g):
    xf = x.astype(jnp.float32)
    y = xf * lax.rsqrt(jnp.mean(xf * xf, axis=-1, keepdims=True) + EPS)
    return (y * g.astype(jnp.float32)).astype(x.dtype)


def modulate(h, shift, scale):
    return h * (1 + scale[:, None, :]) + shift[:, None, :]


def stick_breaking_attention(u, w_qkv, w_o):
    bsz, seq, _ = u.shape
    q, k, v = jnp.split(u @ w_qkv, 3, axis=-1)
    to_heads = lambda t: t.reshape(bsz, seq, SB_HEADS, SB_HEAD_DIM).transpose(0, 2, 1, 3)
    q, k, v = to_heads(q), to_heads(k), to_heads(v)
    scale = SB_HEAD_DIM ** -0.5
    outs = []
    for blk in range(seq // Q_BLOCK):
        q0, q1 = blk * Q_BLOCK, (blk + 1) * Q_BLOCK
        qb, kb, vb = q[:, :, q0:q1], k[:, :, :q1], v[:, :, :q1]
        z = jnp.einsum('bhqd,bhkd->bhqk', qb, kb).astype(jnp.float32) * scale
        t_idx = q0 + jnp.arange(Q_BLOCK)[:, None]
        s_idx = jnp.arange(q1)[None, :]
        mask = s_idx < t_idx
        log_beta = jax.nn.log_sigmoid(z)
        log_keep = jnp.where(mask, jax.nn.log_sigmoid(-z), 0.0)
        log_later = lax.cumsum(log_keep, axis=3, reverse=True) - log_keep
        w = jnp.where(mask, jnp.exp(log_beta + log_later), 0.0)
        outs.append(jnp.einsum('bhqk,bhkd->bhqd', w.astype(vb.dtype), vb))
    o = jnp.concatenate(outs, axis=2).transpose(0, 2, 1, 3).reshape(bsz, seq, D_MODEL)
    return o @ w_o


def s5_layer(u, lam_re, lam_im, log_dt, b_re, b_im, c_re, c_im, d_skip, w_glu, b_glu):
    bsz, seq, _ = u.shape
    uf = u.astype(jnp.float32)
    ug = uf.reshape(bsz, seq, S5_GROUPS, S5_GROUP)
    dt = jnp.exp(log_dt.astype(jnp.float32))[:, None]
    lr, li = lam_re.astype(jnp.float32), lam_im.astype(jnp.float32)
    mag = jnp.exp(lr * dt)
    ar, ai = mag * jnp.cos(li * dt), mag * jnp.sin(li * dt)
    den = lr * lr + li * li
    er = ((ar - 1) * lr + ai * li) / den
    ei = (ai * lr - (ar - 1) * li) / den
    br, bi = b_re.astype(jnp.float32), b_im.astype(jnp.float32)
    bbr = er[..., None] * br - ei[..., None] * bi
    bbi = er[..., None] * bi + ei[..., None] * br
    bu_r = jnp.einsum('bsgc,gpc->bsgp', ug, bbr)
    bu_i = jnp.einsum('bsgc,gpc->bsgp', ug, bbi)
    a_r = jnp.broadcast_to(ar, (1, seq) + ar.shape)
    a_i = jnp.broadcast_to(ai, (1, seq) + ai.shape)

    def combine(e1, e2):
        a1r, a1i, b1r, b1i = e1
        a2r, a2i, b2r, b2i = e2
        return (a1r * a2r - a1i * a2i,
                a1r * a2i + a1i * a2r,
                a2r * b1r - a2i * b1i + b2r,
                a2r * b1i + a2i * b1r + b2i)

    _, _, xr, xi = lax.associative_scan(combine, (a_r, a_i, bu_r, bu_i), axis=1)
    y = (jnp.einsum('bsgp,gcp->bsgc', xr, c_re.astype(jnp.float32))
         - jnp.einsum('bsgp,gcp->bsgc', xi, c_im.astype(jnp.float32)))
    y = y.reshape(bsz, seq, D_MODEL) + d_skip.astype(jnp.float32) * uf
    y = jax.nn.gelu(y).astype(u.dtype)
    ga, gb = jnp.split(y @ w_glu + b_glu, 2, axis=-1)
    return ga * jax.nn.sigmoid(gb)


def conformer_conv(u, w_pw1, b_pw1, w_dw, b_dw, ln_g, ln_b, w_pw2, b_pw2):
    ga, gb = jnp.split(u @ w_pw1 + b_pw1, 2, axis=-1)
    h = ga * jax.nn.sigmoid(gb)
    hp = jnp.pad(h, ((0, 0), (CONV_WIDTH - 1, 0), (0, 0)))
    h = lax.conv_general_dilated(hp, w_dw[:, None, :], window_strides=(1,), padding='VALID',
                                 dimension_numbers=('NWC', 'WIO', 'NWC'),
                                 feature_group_count=D_MODEL) + b_dw
    hf = h.astype(jnp.float32)
    mu = jnp.mean(hf, axis=-1, keepdims=True)
    var = jnp.mean(jnp.square(hf - mu), axis=-1, keepdims=True)
    h = ((hf - mu) * lax.rsqrt(var + EPS) * ln_g.astype(jnp.float32)
         + ln_b.astype(jnp.float32)).astype(u.dtype)
    h = jax.nn.silu(h)
    return h @ w_pw2 + b_pw2


def swiglu(u, w_gate, w_up, w_down):
    return (jax.nn.silu(u @ w_gate) * (u @ w_up)) @ w_down


def _fwd_setup_inputs(seed: int = 0) -> dict:
    key = jax.random.key(seed)
    keys = list(jax.random.split(key, 40))
    nk = lambda: keys.pop()
    nrm = lambda shape, std: jax.random.normal(nk(), shape, jnp.float32) * std
    D, F, G, P, GC = D_MODEL, D_FF, S5_GROUPS, S5_STATE, S5_GROUP
    NA, NB, NC = N_SB_LAYERS, N_S5_LAYERS, N_CV_LAYERS
    n = jnp.arange(P, dtype=jnp.float32)
    return {
        "x": nrm((BATCH, SEQ, D), 1.0),
        "c": nrm((BATCH, D), 1.0),
        "norm_g": 1.0 + nrm((DEPTH, 4, D), 0.05),
        "w_mod": nrm((DEPTH, D, 6 * D), 0.5 * D ** -0.5),
        "b_mod": nrm((DEPTH, 6 * D), 0.01),
        "sb_w_qkv": nrm((NA, D, 3 * D), D ** -0.5),
        "sb_w_o": nrm((NA, D, D), D ** -0.5),
        "s5_lam_re": -0.5 + nrm((NB, G, P), 0.01),
        "s5_lam_im": jnp.pi * n + nrm((NB, G, P), 0.01),
        "s5_log_dt": jax.random.uniform(nk(), (NB, G), jnp.float32,
                                        minval=math.log(S5_DT_MIN), maxval=math.log(S5_DT_MAX)),
        "s5_b_re": nrm((NB, G, P, GC), (2 * GC) ** -0.5),
        "s5_b_im": nrm((NB, G, P, GC), (2 * GC) ** -0.5),
        "s5_c_re": nrm((NB, G, GC, P), (2 * P) ** -0.5),
        "s5_c_im": nrm((NB, G, GC, P), (2 * P) ** -0.5),
        "s5_d": nrm((NB, D), 1.0),
        "s5_w_glu": nrm((NB, D, 2 * D), D ** -0.5),
        "s5_b_glu": nrm((NB, 2 * D), 0.01),
        "cv_w_pw1": nrm((NC, D, 2 * D), D ** -0.5),
        "cv_b_pw1": nrm((NC, 2 * D), 0.01),
        "cv_w_dw": nrm((NC, CONV_WIDTH, D), CONV_WIDTH ** -0.5),
        "cv_b_dw": nrm((NC, D), 0.01),
        "cv_ln_g": 1.0 + nrm((NC, D), 0.05),
        "cv_ln_b": nrm((NC, D), 0.01),
        "cv_w_pw2": nrm((NC, D, D), D ** -0.5),
        "cv_b_pw2": nrm((NC, D), 0.01),
        "ffn_w_gate": nrm((DEPTH, D, F), D ** -0.5),
        "ffn_w_up": nrm((DEPTH, D, F), D ** -0.5),
        "ffn_w_down": nrm((DEPTH, F, D), F ** -0.5),
    }


def _fwd_reference(x, c, norm_g, w_mod, b_mod, sb_w_qkv, sb_w_o,
              s5_lam_re, s5_lam_im, s5_log_dt, s5_b_re, s5_b_im, s5_c_re, s5_c_im,
              s5_d, s5_w_glu, s5_b_glu,
              cv_w_pw1, cv_b_pw1, cv_w_dw, cv_b_dw, cv_ln_g, cv_ln_b, cv_w_pw2, cv_b_pw2,
              ffn_w_gate, ffn_w_up, ffn_w_down):
    mod_all = jnp.einsum('bd,lde->lbe', jax.nn.silu(c), w_mod) + b_mod[:, None, :]
    h = x
    for layer in range(DEPTH):
        sh_m, sc_m, g_m, sh_f, sc_f, g_f = jnp.split(mod_all[layer], 6, axis=-1)
        kind, j = layer % N_MIXERS, layer // N_MIXERS
        u = modulate(rms_norm(h, norm_g[layer, 0]), sh_m, sc_m)
        if kind == 0:
            m = stick_breaking_attention(u, sb_w_qkv[j], sb_w_o[j])
        elif kind == 1:
            m = s5_layer(u, s5_lam_re[j], s5_lam_im[j], s5_log_dt[j], s5_b_re[j], s5_b_im[j],
                         s5_c_re[j], s5_c_im[j], s5_d[j], s5_w_glu[j], s5_b_glu[j])
        else:
            m = conformer_conv(u, cv_w_pw1[j], cv_b_pw1[j], cv_w_dw[j], cv_b_dw[j],
                               cv_ln_g[j], cv_ln_b[j], cv_w_pw2[j], cv_b_pw2[j])
        h = h + g_m[:, None, :] * rms_norm(m, norm_g[layer, 1])
        u = modulate(rms_norm(h, norm_g[layer, 2]), sh_f, sc_f)
        f = swiglu(u, ffn_w_gate[layer], ffn_w_up[layer], ffn_w_down[layer])
        h = h + g_f[:, None, :] * rms_norm(f, norm_g[layer, 3])
    return h


import jax as _jax
import jax.numpy as _jnp

TWIN_FORMAT = 'train_step'
FWD_PARAMS = ['x', 'c', 'norm_g', 'w_mod', 'b_mod', 'sb_w_qkv', 'sb_w_o', 's5_lam_re', 's5_lam_im', 's5_log_dt', 's5_b_re', 's5_b_im', 's5_c_re', 's5_c_im', 's5_d', 's5_w_glu', 's5_b_glu', 'cv_w_pw1', 'cv_b_pw1', 'cv_w_dw', 'cv_b_dw', 'cv_ln_g', 'cv_ln_b', 'cv_w_pw2', 'cv_b_pw2', 'ffn_w_gate', 'ffn_w_up', 'ffn_w_down']
TWIN_WEIGHTS = ['norm_g', 'w_mod', 'b_mod', 'sb_w_qkv', 'sb_w_o', 's5_lam_re', 's5_lam_im', 's5_log_dt', 's5_b_re', 's5_b_im', 's5_c_re', 's5_c_im', 's5_d', 's5_w_glu', 's5_b_glu', 'cv_w_pw1', 'cv_b_pw1', 'cv_w_dw', 'cv_b_dw', 'cv_ln_g', 'cv_ln_b', 'cv_w_pw2', 'cv_b_pw2', 'ffn_w_gate', 'ffn_w_up', 'ffn_w_down']
TWIN_DIFF_INPUT = 'x'
TWIN_INPUTS = ['x', 'c', 'norm_g', 'w_mod', 'b_mod', 'sb_w_qkv', 'sb_w_o', 's5_lam_re', 's5_lam_im', 's5_log_dt', 's5_b_re', 's5_b_im', 's5_c_re', 's5_c_im', 's5_d', 's5_w_glu', 's5_b_glu', 'cv_w_pw1', 'cv_b_pw1', 'cv_w_dw', 'cv_b_dw', 'cv_ln_g', 'cv_ln_b', 'cv_w_pw2', 'cv_b_pw2', 'ffn_w_gate', 'ffn_w_up', 'ffn_w_down', 'loss_target', 'm_norm_g', 'm_w_mod', 'm_b_mod', 'm_sb_w_qkv', 'm_sb_w_o', 'm_s5_lam_re', 'm_s5_lam_im', 'm_s5_log_dt', 'm_s5_b_re', 'm_s5_b_im', 'm_s5_c_re', 'm_s5_c_im', 'm_s5_d', 'm_s5_w_glu', 'm_s5_b_glu', 'm_cv_w_pw1', 'm_cv_b_pw1', 'm_cv_w_dw', 'm_cv_b_dw', 'm_cv_ln_g', 'm_cv_ln_b', 'm_cv_w_pw2', 'm_cv_b_pw2', 'm_ffn_w_gate', 'm_ffn_w_up', 'm_ffn_w_down', 'v_norm_g', 'v_w_mod', 'v_b_mod', 'v_sb_w_qkv', 'v_sb_w_o', 'v_s5_lam_re', 'v_s5_lam_im', 'v_s5_log_dt', 'v_s5_b_re', 'v_s5_b_im', 'v_s5_c_re', 'v_s5_c_im', 'v_s5_d', 'v_s5_w_glu', 'v_s5_b_glu', 'v_cv_w_pw1', 'v_cv_b_pw1', 'v_cv_w_dw', 'v_cv_b_dw', 'v_cv_ln_g', 'v_cv_ln_b', 'v_cv_w_pw2', 'v_cv_b_pw2', 'v_ffn_w_gate', 'v_ffn_w_up', 'v_ffn_w_down']
TWIN_OUTPUTS = ['loss', 'grad_x', 'grad_norm_g', 'grad_w_mod', 'grad_b_mod', 'grad_sb_w_qkv', 'grad_sb_w_o', 'grad_s5_lam_re', 'grad_s5_lam_im', 'grad_s5_log_dt', 'grad_s5_b_re', 'grad_s5_b_im', 'grad_s5_c_re', 'grad_s5_c_im', 'grad_s5_d', 'grad_s5_w_glu', 'grad_s5_b_glu', 'grad_cv_w_pw1', 'grad_cv_b_pw1', 'grad_cv_w_dw', 'grad_cv_b_dw', 'grad_cv_ln_g', 'grad_cv_ln_b', 'grad_cv_w_pw2', 'grad_cv_b_pw2', 'grad_ffn_w_gate', 'grad_ffn_w_up', 'grad_ffn_w_down', 'delta_norm_g', 'delta_w_mod', 'delta_b_mod', 'delta_sb_w_qkv', 'delta_sb_w_o', 'delta_s5_lam_re', 'delta_s5_lam_im', 'delta_s5_log_dt', 'delta_s5_b_re', 'delta_s5_b_im', 'delta_s5_c_re', 'delta_s5_c_im', 'delta_s5_d', 'delta_s5_w_glu', 'delta_s5_b_glu', 'delta_cv_w_pw1', 'delta_cv_b_pw1', 'delta_cv_w_dw', 'delta_cv_b_dw', 'delta_cv_ln_g', 'delta_cv_ln_b', 'delta_cv_w_pw2', 'delta_cv_b_pw2', 'delta_ffn_w_gate', 'delta_ffn_w_up', 'delta_ffn_w_down', 'new_m_norm_g', 'new_m_w_mod', 'new_m_b_mod', 'new_m_sb_w_qkv', 'new_m_sb_w_o', 'new_m_s5_lam_re', 'new_m_s5_lam_im', 'new_m_s5_log_dt', 'new_m_s5_b_re', 'new_m_s5_b_im', 'new_m_s5_c_re', 'new_m_s5_c_im', 'new_m_s5_d', 'new_m_s5_w_glu', 'new_m_s5_b_glu', 'new_m_cv_w_pw1', 'new_m_cv_b_pw1', 'new_m_cv_w_dw', 'new_m_cv_b_dw', 'new_m_cv_ln_g', 'new_m_cv_ln_b', 'new_m_cv_w_pw2', 'new_m_cv_b_pw2', 'new_m_ffn_w_gate', 'new_m_ffn_w_up', 'new_m_ffn_w_down', 'new_v_norm_g', 'new_v_w_mod', 'new_v_b_mod', 'new_v_sb_w_qkv', 'new_v_sb_w_o', 'new_v_s5_lam_re', 'new_v_s5_lam_im', 'new_v_s5_log_dt', 'new_v_s5_b_re', 'new_v_s5_b_im', 'new_v_s5_c_re', 'new_v_s5_c_im', 'new_v_s5_d', 'new_v_s5_w_glu', 'new_v_s5_b_glu', 'new_v_cv_w_pw1', 'new_v_cv_b_pw1', 'new_v_cv_w_dw', 'new_v_cv_b_dw', 'new_v_cv_ln_g', 'new_v_cv_ln_b', 'new_v_cv_w_pw2', 'new_v_cv_b_pw2', 'new_v_ffn_w_gate', 'new_v_ffn_w_up', 'new_v_ffn_w_down']
TWIN_LEAF_KINDS = {'loss': 'loss', 'grad_x': 'grad_x', 'grad_norm_g': 'grad_w', 'grad_w_mod': 'grad_w', 'grad_b_mod': 'grad_w', 'grad_sb_w_qkv': 'grad_w', 'grad_sb_w_o': 'grad_w', 'grad_s5_lam_re': 'grad_w', 'grad_s5_lam_im': 'grad_w', 'grad_s5_log_dt': 'grad_w', 'grad_s5_b_re': 'grad_w', 'grad_s5_b_im': 'grad_w', 'grad_s5_c_re': 'grad_w', 'grad_s5_c_im': 'grad_w', 'grad_s5_d': 'grad_w', 'grad_s5_w_glu': 'grad_w', 'grad_s5_b_glu': 'grad_w', 'grad_cv_w_pw1': 'grad_w', 'grad_cv_b_pw1': 'grad_w', 'grad_cv_w_dw': 'grad_w', 'grad_cv_b_dw': 'grad_w', 'grad_cv_ln_g': 'grad_w', 'grad_cv_ln_b': 'grad_w', 'grad_cv_w_pw2': 'grad_w', 'grad_cv_b_pw2': 'grad_w', 'grad_ffn_w_gate': 'grad_w', 'grad_ffn_w_up': 'grad_w', 'grad_ffn_w_down': 'grad_w', 'delta_norm_g': 'delta_w', 'delta_w_mod': 'delta_w', 'delta_b_mod': 'delta_w', 'delta_sb_w_qkv': 'delta_w', 'delta_sb_w_o': 'delta_w', 'delta_s5_lam_re': 'delta_w', 'delta_s5_lam_im': 'delta_w', 'delta_s5_log_dt': 'delta_w', 'delta_s5_b_re': 'delta_w', 'delta_s5_b_im': 'delta_w', 'delta_s5_c_re': 'delta_w', 'delta_s5_c_im': 'delta_w', 'delta_s5_d': 'delta_w', 'delta_s5_w_glu': 'delta_w', 'delta_s5_b_glu': 'delta_w', 'delta_cv_w_pw1': 'delta_w', 'delta_cv_b_pw1': 'delta_w', 'delta_cv_w_dw': 'delta_w', 'delta_cv_b_dw': 'delta_w', 'delta_cv_ln_g': 'delta_w', 'delta_cv_ln_b': 'delta_w', 'delta_cv_w_pw2': 'delta_w', 'delta_cv_b_pw2': 'delta_w', 'delta_ffn_w_gate': 'delta_w', 'delta_ffn_w_up': 'delta_w', 'delta_ffn_w_down': 'delta_w', 'new_m_norm_g': 'new_m', 'new_m_w_mod': 'new_m', 'new_m_b_mod': 'new_m', 'new_m_sb_w_qkv': 'new_m', 'new_m_sb_w_o': 'new_m', 'new_m_s5_lam_re': 'new_m', 'new_m_s5_lam_im': 'new_m', 'new_m_s5_log_dt': 'new_m', 'new_m_s5_b_re': 'new_m', 'new_m_s5_b_im': 'new_m', 'new_m_s5_c_re': 'new_m', 'new_m_s5_c_im': 'new_m', 'new_m_s5_d': 'new_m', 'new_m_s5_w_glu': 'new_m', 'new_m_s5_b_glu': 'new_m', 'new_m_cv_w_pw1': 'new_m', 'new_m_cv_b_pw1': 'new_m', 'new_m_cv_w_dw': 'new_m', 'new_m_cv_b_dw': 'new_m', 'new_m_cv_ln_g': 'new_m', 'new_m_cv_ln_b': 'new_m', 'new_m_cv_w_pw2': 'new_m', 'new_m_cv_b_pw2': 'new_m', 'new_m_ffn_w_gate': 'new_m', 'new_m_ffn_w_up': 'new_m', 'new_m_ffn_w_down': 'new_m', 'new_v_norm_g': 'new_v', 'new_v_w_mod': 'new_v', 'new_v_b_mod': 'new_v', 'new_v_sb_w_qkv': 'new_v', 'new_v_sb_w_o': 'new_v', 'new_v_s5_lam_re': 'new_v', 'new_v_s5_lam_im': 'new_v', 'new_v_s5_log_dt': 'new_v', 'new_v_s5_b_re': 'new_v', 'new_v_s5_b_im': 'new_v', 'new_v_s5_c_re': 'new_v', 'new_v_s5_c_im': 'new_v', 'new_v_s5_d': 'new_v', 'new_v_s5_w_glu': 'new_v', 'new_v_s5_b_glu': 'new_v', 'new_v_cv_w_pw1': 'new_v', 'new_v_cv_b_pw1': 'new_v', 'new_v_cv_w_dw': 'new_v', 'new_v_cv_b_dw': 'new_v', 'new_v_cv_ln_g': 'new_v', 'new_v_cv_ln_b': 'new_v', 'new_v_cv_w_pw2': 'new_v', 'new_v_cv_b_pw2': 'new_v', 'new_v_ffn_w_gate': 'new_v', 'new_v_ffn_w_up': 'new_v', 'new_v_ffn_w_down': 'new_v'}


def _forward(args):
    return _fwd_reference(*[args[k] for k in FWD_PARAMS])


def _output_shape():
    def fwd():
        inp = _fwd_setup_inputs(0)
        return _fwd_reference(*[inp[k] for k in FWD_PARAMS])
    out = _jax.eval_shape(fwd)
    return out.shape, out.dtype

N_MICROBATCH = 1
ADAM_LR = 0.001
ADAM_B1 = 0.9
ADAM_B2 = 0.999
ADAM_EPS = 1e-08
ADAM_WD = 0.01
ADAM_STEP = 10
PER_EXAMPLE_BATCH_AXIS = {'x': 0, 'c': 0, 'loss_target': 0}
SHARED_INPUTS = []
_WEIGHT_DTYPES = {'norm_g': _jnp.float32, 'w_mod': _jnp.float32, 'b_mod': _jnp.float32, 'sb_w_qkv': _jnp.float32, 'sb_w_o': _jnp.float32, 's5_lam_re': _jnp.float32, 's5_lam_im': _jnp.float32, 's5_log_dt': _jnp.float32, 's5_b_re': _jnp.float32, 's5_b_im': _jnp.float32, 's5_c_re': _jnp.float32, 's5_c_im': _jnp.float32, 's5_d': _jnp.float32, 's5_w_glu': _jnp.float32, 's5_b_glu': _jnp.float32, 'cv_w_pw1': _jnp.float32, 'cv_b_pw1': _jnp.float32, 'cv_w_dw': _jnp.float32, 'cv_b_dw': _jnp.float32, 'cv_ln_g': _jnp.float32, 'cv_ln_b': _jnp.float32, 'cv_w_pw2': _jnp.float32, 'cv_b_pw2': _jnp.float32, 'ffn_w_gate': _jnp.float32, 'ffn_w_up': _jnp.float32, 'ffn_w_down': _jnp.float32}
MOMENT_SCALE = {'norm_g': 4.386265e+00, 'w_mod': 2.478963e+00, 'b_mod': 5.385411e+00, 'sb_w_qkv': 4.317993e-01, 'sb_w_o': 7.695449e-01, 's5_lam_re': 3.904714e-02, 's5_lam_im': 5.176348e-02, 's5_log_dt': 5.365899e+00, 's5_b_re': 3.865850e-02, 's5_b_im': 4.041648e-02, 's5_c_re': 7.308590e-02, 's5_c_im': 7.844321e-02, 's5_d': 1.280057e+00, 's5_w_glu': 8.927293e-01, 's5_b_glu': 2.357088e+00, 'cv_w_pw1': 3.158723e-01, 'cv_b_pw1': 1.041095e+00, 'cv_w_dw': 4.774110e-01, 'cv_b_dw': 2.539801e+00, 'cv_ln_g': 1.162309e+00, 'cv_ln_b': 1.634151e+00, 'cv_w_pw2': 7.915688e-01, 'cv_b_pw2': 3.288348e+00, 'ffn_w_gate': 1.151845e-01, 'ffn_w_up': 1.507414e-01, 'ffn_w_down': 2.531786e-01}


def _to_microbatches(a, axis):
    t = _jnp.moveaxis(a, axis, 0)
    t = t.reshape((N_MICROBATCH, t.shape[0] // N_MICROBATCH) + t.shape[1:])
    return _jnp.moveaxis(t, 1, axis + 1)


def setup_inputs(seed: int = 0) -> dict:
    inp = _fwd_setup_inputs(seed)
    key = _jax.random.fold_in(_jax.random.key(seed), 7919)
    shape, _ = _output_shape()
    out = dict(inp)
    out["loss_target"] = _jax.random.normal(_jax.random.fold_in(key, 0), shape, _jnp.float32)
    for i, name in enumerate(TWIN_WEIGHTS):
        w = inp[name].astype(_jnp.float32)
        if MOMENT_SCALE is None:
            s = _jnp.sqrt(_jnp.mean(_jnp.square(w)) + 1e-30)
        else:
            s = MOMENT_SCALE[name]
        km, kv = _jax.random.split(_jax.random.fold_in(key, i + 1))
        out[name] = w
        out["m_" + name] = s * _jax.random.normal(km, w.shape, _jnp.float32)
        out["v_" + name] = (s * s) * _jax.random.uniform(kv, w.shape, _jnp.float32, 0.5, 1.5)
    if N_MICROBATCH > 1:
        for name, axis in PER_EXAMPLE_BATCH_AXIS.items():
            out[name] = _to_microbatches(out[name], axis)
    return {'x': out['x'], 'c': out['c'], 'norm_g': out['norm_g'], 'w_mod': out['w_mod'], 'b_mod': out['b_mod'], 'sb_w_qkv': out['sb_w_qkv'], 'sb_w_o': out['sb_w_o'], 's5_lam_re': out['s5_lam_re'], 's5_lam_im': out['s5_lam_im'], 's5_log_dt': out['s5_log_dt'], 's5_b_re': out['s5_b_re'], 's5_b_im': out['s5_b_im'], 's5_c_re': out['s5_c_re'], 's5_c_im': out['s5_c_im'], 's5_d': out['s5_d'], 's5_w_glu': out['s5_w_glu'], 's5_b_glu': out['s5_b_glu'], 'cv_w_pw1': out['cv_w_pw1'], 'cv_b_pw1': out['cv_b_pw1'], 'cv_w_dw': out['cv_w_dw'], 'cv_b_dw': out['cv_b_dw'], 'cv_ln_g': out['cv_ln_g'], 'cv_ln_b': out['cv_ln_b'], 'cv_w_pw2': out['cv_w_pw2'], 'cv_b_pw2': out['cv_b_pw2'], 'ffn_w_gate': out['ffn_w_gate'], 'ffn_w_up': out['ffn_w_up'], 'ffn_w_down': out['ffn_w_down'], 'loss_target': out['loss_target'], 'm_norm_g': out['m_norm_g'], 'm_w_mod': out['m_w_mod'], 'm_b_mod': out['m_b_mod'], 'm_sb_w_qkv': out['m_sb_w_qkv'], 'm_sb_w_o': out['m_sb_w_o'], 'm_s5_lam_re': out['m_s5_lam_re'], 'm_s5_lam_im': out['m_s5_lam_im'], 'm_s5_log_dt': out['m_s5_log_dt'], 'm_s5_b_re': out['m_s5_b_re'], 'm_s5_b_im': out['m_s5_b_im'], 'm_s5_c_re': out['m_s5_c_re'], 'm_s5_c_im': out['m_s5_c_im'], 'm_s5_d': out['m_s5_d'], 'm_s5_w_glu': out['m_s5_w_glu'], 'm_s5_b_glu': out['m_s5_b_glu'], 'm_cv_w_pw1': out['m_cv_w_pw1'], 'm_cv_b_pw1': out['m_cv_b_pw1'], 'm_cv_w_dw': out['m_cv_w_dw'], 'm_cv_b_dw': out['m_cv_b_dw'], 'm_cv_ln_g': out['m_cv_ln_g'], 'm_cv_ln_b': out['m_cv_ln_b'], 'm_cv_w_pw2': out['m_cv_w_pw2'], 'm_cv_b_pw2': out['m_cv_b_pw2'], 'm_ffn_w_gate': out['m_ffn_w_gate'], 'm_ffn_w_up': out['m_ffn_w_up'], 'm_ffn_w_down': out['m_ffn_w_down'], 'v_norm_g': out['v_norm_g'], 'v_w_mod': out['v_w_mod'], 'v_b_mod': out['v_b_mod'], 'v_sb_w_qkv': out['v_sb_w_qkv'], 'v_sb_w_o': out['v_sb_w_o'], 'v_s5_lam_re': out['v_s5_lam_re'], 'v_s5_lam_im': out['v_s5_lam_im'], 'v_s5_log_dt': out['v_s5_log_dt'], 'v_s5_b_re': out['v_s5_b_re'], 'v_s5_b_im': out['v_s5_b_im'], 'v_s5_c_re': out['v_s5_c_re'], 'v_s5_c_im': out['v_s5_c_im'], 'v_s5_d': out['v_s5_d'], 'v_s5_w_glu': out['v_s5_w_glu'], 'v_s5_b_glu': out['v_s5_b_glu'], 'v_cv_w_pw1': out['v_cv_w_pw1'], 'v_cv_b_pw1': out['v_cv_b_pw1'], 'v_cv_w_dw': out['v_cv_w_dw'], 'v_cv_b_dw': out['v_cv_b_dw'], 'v_cv_ln_g': out['v_cv_ln_g'], 'v_cv_ln_b': out['v_cv_ln_b'], 'v_cv_w_pw2': out['v_cv_w_pw2'], 'v_cv_b_pw2': out['v_cv_b_pw2'], 'v_ffn_w_gate': out['v_ffn_w_gate'], 'v_ffn_w_up': out['v_ffn_w_up'], 'v_ffn_w_down': out['v_ffn_w_down']}


def _loss(weights, diff, rest, loss_target):
    with _jax.named_scope("forward"):
        args = {**rest, TWIN_DIFF_INPUT: diff, **{k: w.astype(_WEIGHT_DTYPES[k]) for k, w in weights.items()}}
        y = _forward(args)
    with _jax.named_scope("loss_head"):
        err = _jnp.square(y.astype(_jnp.float32) - loss_target)
        return 0.5 * _jnp.sum(_jnp.mean(err, axis=-1)) if err.ndim else 0.5 * err


def _adamw(w, g, m, v):
    m = ADAM_B1 * m + (1.0 - ADAM_B1) * g
    v = ADAM_B2 * v + (1.0 - ADAM_B2) * _jnp.square(g)
    m_hat = m / (1.0 - ADAM_B1 ** ADAM_STEP)
    v_hat = v / (1.0 - ADAM_B2 ** ADAM_STEP)
    delta = -ADAM_LR * (m_hat / (_jnp.sqrt(v_hat) + ADAM_EPS) + ADAM_WD * w)
    return delta, m, v


def reference(x, c, norm_g, w_mod, b_mod, sb_w_qkv, sb_w_o, s5_lam_re, s5_lam_im, s5_log_dt, s5_b_re, s5_b_im, s5_c_re, s5_c_im, s5_d, s5_w_glu, s5_b_glu, cv_w_pw1, cv_b_pw1, cv_w_dw, cv_b_dw, cv_ln_g, cv_ln_b, cv_w_pw2, cv_b_pw2, ffn_w_gate, ffn_w_up, ffn_w_down, loss_target, m_norm_g, m_w_mod, m_b_mod, m_sb_w_qkv, m_sb_w_o, m_s5_lam_re, m_s5_lam_im, m_s5_log_dt, m_s5_b_re, m_s5_b_im, m_s5_c_re, m_s5_c_im, m_s5_d, m_s5_w_glu, m_s5_b_glu, m_cv_w_pw1, m_cv_b_pw1, m_cv_w_dw, m_cv_b_dw, m_cv_ln_g, m_cv_ln_b, m_cv_w_pw2, m_cv_b_pw2, m_ffn_w_gate, m_ffn_w_up, m_ffn_w_down, v_norm_g, v_w_mod, v_b_mod, v_sb_w_qkv, v_sb_w_o, v_s5_lam_re, v_s5_lam_im, v_s5_log_dt, v_s5_b_re, v_s5_b_im, v_s5_c_re, v_s5_c_im, v_s5_d, v_s5_w_glu, v_s5_b_glu, v_cv_w_pw1, v_cv_b_pw1, v_cv_w_dw, v_cv_b_dw, v_cv_ln_g, v_cv_ln_b, v_cv_w_pw2, v_cv_b_pw2, v_ffn_w_gate, v_ffn_w_up, v_ffn_w_down):
    given = dict(x=x, c=c, norm_g=norm_g, w_mod=w_mod, b_mod=b_mod, sb_w_qkv=sb_w_qkv, sb_w_o=sb_w_o, s5_lam_re=s5_lam_re, s5_lam_im=s5_lam_im, s5_log_dt=s5_log_dt, s5_b_re=s5_b_re, s5_b_im=s5_b_im, s5_c_re=s5_c_re, s5_c_im=s5_c_im, s5_d=s5_d, s5_w_glu=s5_w_glu, s5_b_glu=s5_b_glu, cv_w_pw1=cv_w_pw1, cv_b_pw1=cv_b_pw1, cv_w_dw=cv_w_dw, cv_b_dw=cv_b_dw, cv_ln_g=cv_ln_g, cv_ln_b=cv_ln_b, cv_w_pw2=cv_w_pw2, cv_b_pw2=cv_b_pw2, ffn_w_gate=ffn_w_gate, ffn_w_up=ffn_w_up, ffn_w_down=ffn_w_down, loss_target=loss_target, m_norm_g=m_norm_g, m_w_mod=m_w_mod, m_b_mod=m_b_mod, m_sb_w_qkv=m_sb_w_qkv, m_sb_w_o=m_sb_w_o, m_s5_lam_re=m_s5_lam_re, m_s5_lam_im=m_s5_lam_im, m_s5_log_dt=m_s5_log_dt, m_s5_b_re=m_s5_b_re, m_s5_b_im=m_s5_b_im, m_s5_c_re=m_s5_c_re, m_s5_c_im=m_s5_c_im, m_s5_d=m_s5_d, m_s5_w_glu=m_s5_w_glu, m_s5_b_glu=m_s5_b_glu, m_cv_w_pw1=m_cv_w_pw1, m_cv_b_pw1=m_cv_b_pw1, m_cv_w_dw=m_cv_w_dw, m_cv_b_dw=m_cv_b_dw, m_cv_ln_g=m_cv_ln_g, m_cv_ln_b=m_cv_ln_b, m_cv_w_pw2=m_cv_w_pw2, m_cv_b_pw2=m_cv_b_pw2, m_ffn_w_gate=m_ffn_w_gate, m_ffn_w_up=m_ffn_w_up, m_ffn_w_down=m_ffn_w_down, v_norm_g=v_norm_g, v_w_mod=v_w_mod, v_b_mod=v_b_mod, v_sb_w_qkv=v_sb_w_qkv, v_sb_w_o=v_sb_w_o, v_s5_lam_re=v_s5_lam_re, v_s5_lam_im=v_s5_lam_im, v_s5_log_dt=v_s5_log_dt, v_s5_b_re=v_s5_b_re, v_s5_b_im=v_s5_b_im, v_s5_c_re=v_s5_c_re, v_s5_c_im=v_s5_c_im, v_s5_d=v_s5_d, v_s5_w_glu=v_s5_w_glu, v_s5_b_glu=v_s5_b_glu, v_cv_w_pw1=v_cv_w_pw1, v_cv_b_pw1=v_cv_b_pw1, v_cv_w_dw=v_cv_w_dw, v_cv_b_dw=v_cv_b_dw, v_cv_ln_g=v_cv_ln_g, v_cv_ln_b=v_cv_ln_b, v_cv_w_pw2=v_cv_w_pw2, v_cv_b_pw2=v_cv_b_pw2, v_ffn_w_gate=v_ffn_w_gate, v_ffn_w_up=v_ffn_w_up, v_ffn_w_down=v_ffn_w_down)
    weights = {n: given[n] for n in TWIN_WEIGHTS}
    shared = {n: given[n] for n in SHARED_INPUTS}
    per_example = {n: given[n] for n in ['x', 'c']}
    grad_fn = _jax.value_and_grad(_loss, argnums=(0, 1))

    def one_microbatch(ex, loss_target):
        ex = dict(ex)
        diff = ex.pop(TWIN_DIFF_INPUT)
        return grad_fn(weights, diff, {**shared, **ex}, loss_target)

    if N_MICROBATCH == 1:
        loss, (grad_w, grad_x) = one_microbatch(per_example, given["loss_target"])
    else:
        def body(carry, xs):
            loss_sum, grad_sum = carry
            l_k, (gw_k, gx_k) = one_microbatch(xs[0], xs[1])
            with _jax.named_scope("update"):
                return (loss_sum + l_k, _jax.tree.map(_jnp.add, grad_sum, gw_k)), gx_k

        init = (_jnp.zeros((), _jnp.float32), _jax.tree.map(_jnp.zeros_like, weights))
        (loss, grad_w), grad_x = _jax.lax.scan(body, init, (per_example, given["loss_target"]))
    with _jax.named_scope("update"):
        delta_w, new_m, new_v = {}, {}, {}
        for n in TWIN_WEIGHTS:
            delta_w[n], new_m[n], new_v[n] = _adamw(weights[n], grad_w[n], given["m_" + n], given["v_" + n])
    return (loss, grad_x, *[grad_w[n] for n in TWIN_WEIGHTS], *[delta_w[n] for n in TWIN_WEIGHTS],
            *[new_m[n] for n in TWIN_WEIGHTS], *[new_v[n] for n in TWIN_WEIGHTS])
```

```python
import functools
import math

import jax
import jax.numpy as jnp
from jax import lax
from jax.experimental import pallas as pl
from jax.experimental.pallas import tpu as pltpu

F32 = jnp.float32
BF16 = jnp.bfloat16
I32 = jnp.int32
SDS = jax.ShapeDtypeStruct
MESH = pl.DeviceIdType.MESH

HEAD_DIM = 64
S5_GROUP = 16
S5_STATE = 64
S5_BLOCK_GROUPS = 8
S5_BLOCK_CH = S5_GROUP * S5_BLOCK_GROUPS
S5_BLOCK_ST = S5_STATE * S5_BLOCK_GROUPS
CONV_WIDTH = 31
CONV_HALO = 32
EPS = 1e-6
ADAM_LR = 0.001
ADAM_B1 = 0.9
ADAM_B2 = 0.999
ADAM_EPS = 1e-08
ADAM_WD = 0.01
ADAM_STEP = 10

LANES = 128
SUBLANES = 8
VMEM_LIMIT = 56 * 2 ** 20
PACK_LANES = 512
ATT_BLOCK = 128
EXP_FLOOR = -104.0

_COUNTER = [0]


def _name(base):
    _COUNTER[0] += 1
    return f"{base}_{_COUNTER[0]}"


def _tile(dim, target, mult=LANES):
    t = min(dim, target)
    t -= t % mult
    while t > mult and dim % t:
        t -= mult
    return t if t > 0 and dim % t == 0 else dim


def _params(sem=None):
    return pltpu.CompilerParams(dimension_semantics=sem, vmem_limit_bytes=VMEM_LIMIT)


_DIMS = {"nn": (((1,), (0,)), ((), ())), "nt": (((1,), (1,)), ((), ())), "tn": (((0,), (0,)), ((), ()))}


def _matmul(a, b, form, out_dtype, base, tm=512, tn=512, tk=512):
    if form == "tn":
        k_dim, m_dim = a.shape
    else:
        m_dim, k_dim = a.shape
    n_dim = b.shape[0] if form == "nt" else b.shape[1]
    tm, tn, tk = _tile(m_dim, tm), _tile(n_dim, tn), _tile(k_dim, tk)
    nk = k_dim // tk
    dims = _DIMS[form]

    def body(a_ref, b_ref, o_ref, acc_ref):
        k = pl.program_id(2)

        @pl.when(k == 0)
        def _():
            acc_ref[...] = jnp.zeros_like(acc_ref)

        acc_ref[...] += lax.dot_general(a_ref[...].astype(BF16), b_ref[...].astype(BF16), dims,
                                        preferred_element_type=F32)

        @pl.when(k == nk - 1)
        def _():
            o_ref[...] = acc_ref[...].astype(o_ref.dtype)

    a_spec = (pl.BlockSpec((tk, tm), lambda i, j, k: (k, i)) if form == "tn"
              else pl.BlockSpec((tm, tk), lambda i, j, k: (i, k)))
    b_spec = (pl.BlockSpec((tn, tk), lambda i, j, k: (j, k)) if form == "nt"
              else pl.BlockSpec((tk, tn), lambda i, j, k: (k, j)))
    return pl.pallas_call(
        body, name=_name(base), grid=(m_dim // tm, n_dim // tn, nk),
        in_specs=[a_spec, b_spec], out_specs=pl.BlockSpec((tm, tn), lambda i, j, k: (i, j)),
        out_shape=SDS((m_dim, n_dim), out_dtype), scratch_shapes=[pltpu.VMEM((tm, tn), F32)],
        compiler_params=_params(("parallel", "parallel", "arbitrary")),
    )(a, b)


def _rowwise(fn, rows, vecs, out_rows, out_vecs, base, tr=256):
    s_dim = rows[0].shape[0]
    tr = _tile(s_dim, tr, SUBLANES)
    nr, nv, nor = len(rows), len(vecs), len(out_rows)

    def body(*refs):
        r_in, v_in = refs[:nr], refs[nr:nr + nv]
        r_out, v_out = refs[nr + nv:nr + nv + nor], refs[nr + nv + nor:]
        ro, vo = fn(*[r[...] for r in r_in], *[v[...] for v in v_in])
        for ref, val in zip(r_out, ro, strict=True):
            ref[...] = val.astype(ref.dtype)
        if v_out:
            i = pl.program_id(0)

            @pl.when(i == 0)
            def _():
                for ref, val in zip(v_out, vo, strict=True):
                    ref[...] = val.astype(ref.dtype)

            @pl.when(i > 0)
            def _():
                for ref, val in zip(v_out, vo, strict=True):
                    ref[...] += val.astype(ref.dtype)

    in_specs = [pl.BlockSpec((tr, r.shape[1]), lambda i: (i, 0)) for r in rows]
    in_specs += [pl.BlockSpec(v.shape, lambda i: (0, 0)) for v in vecs]
    out_specs = [pl.BlockSpec((tr, o.shape[1]), lambda i: (i, 0)) for o in out_rows]
    out_specs += [pl.BlockSpec(o.shape, lambda i: (0, 0)) for o in out_vecs]
    outs = pl.pallas_call(
        body, name=_name(base), grid=(s_dim // tr,), in_specs=in_specs, out_specs=out_specs,
        out_shape=list(out_rows) + list(out_vecs),
        compiler_params=_params(("arbitrary",)),
    )(*rows, *vecs)
    return outs


def _f(x):
    return x.astype(F32)


def _rms(x, g):
    return x * lax.rsqrt(jnp.mean(x * x, axis=-1, keepdims=True) + EPS) * g


def _pre(h, g, sh, sc):
    return _rms(h, g) * (1 + sc) + sh


def _post(h, m, gate, g):
    return h + gate * _rms(m, g)


def _halves(t):
    d = t.shape[1] // 2
    return t[:, :d], t[:, d:]


def _glu2(ta, tb, ba, bb):
    return (ta + ba) * jax.nn.sigmoid(tb + bb)


def _glu(t, b):
    return _glu2(*_halves(t), *_halves(b))


def _glu_bwd(t, dout, b):
    _, vjp = jax.vjp(_glu2, *_halves(t), *_halves(b))
    dta, dtb, dba, dbb = vjp(dout)
    return jnp.concatenate([dta, dtb], axis=1), jnp.concatenate([dba, dbb], axis=1)


def _swish2(g, u):
    return jax.nn.silu(g) * u


def _swish_gate(gu):
    return _swish2(*_halves(gu))


def _ln_silu(h, g, b):
    mu = jnp.mean(h, axis=-1, keepdims=True)
    var = jnp.mean(jnp.square(h - mu), axis=-1, keepdims=True)
    return jax.nn.silu((h - mu) * lax.rsqrt(var + EPS) * g + b)


def _dot(a, b, form="nn"):
    return lax.dot_general(a, b, _DIMS[form], preferred_element_type=F32)


def _split_dot(x, tri):
    hi = x.astype(BF16)
    lo = (x - hi.astype(F32)).astype(BF16)
    return _dot(hi, tri) + _dot(lo, tri)


def _sb_scores(qb, kb, scale, mask):
    z = _dot(qb, kb, "nt") * scale
    sp = jnp.maximum(z, 0.0) + jnp.log(1.0 + jnp.exp(-jnp.abs(z)))
    lk = -sp
    if mask is not None:
        lk = jnp.where(mask, lk, 0.0)
    return lk, z - sp


def _sb_fwd(q, k, v, base):
    h_dim, s_dim, hd = q.shape
    bq = ATT_BLOCK
    scale = hd ** -0.5

    def body(q_ref, k_ref, v_ref, o_ref):
        i = pl.program_id(1)
        qb = q_ref[0]
        row = lax.broadcasted_iota(I32, (bq, bq), 0)
        col = lax.broadcasted_iota(I32, (bq, bq), 1)
        later = (row > col).astype(BF16)
        causal = col < row

        def tile(j, c, acc, mask):
            ks = pl.ds(pl.multiple_of(j * bq, bq), bq)
            kb, vb = k_ref[0, ks, :], v_ref[0, ks, :]
            lk, lb = _sb_scores(qb, kb, scale, mask)
            w = jnp.exp(lb + _split_dot(lk, later) + c)
            if mask is not None:
                w = jnp.where(mask, w, 0.0)
            acc = acc + _dot(w.astype(BF16), vb)
            return c + jnp.sum(lk, axis=1, keepdims=True), acc

        c, acc = tile(i, jnp.zeros((bq, 1), F32), jnp.zeros((bq, hd), F32), causal)
        c, acc = lax.fori_loop(0, i, lambda n, ca: tile(i - 1 - n, ca[0], ca[1], None), (c, acc))
        o_ref[0] = acc

    blk = pl.BlockSpec((1, bq, hd), lambda h, i: (h, i, 0))
    full = pl.BlockSpec((1, s_dim, hd), lambda h, i: (h, 0, 0))
    return pl.pallas_call(
        body, name=_name(base), grid=(h_dim, s_dim // bq), in_specs=[blk, full, full], out_specs=blk,
        out_shape=SDS((h_dim, s_dim, hd), F32), compiler_params=_params(("parallel", "arbitrary")),
    )(q, k, v)


def _sb_bwd(q, k, v, do, base):
    h_dim, s_dim, hd = q.shape
    bq = ATT_BLOCK
    nq = s_dim // bq
    scale = hd ** -0.5

    def body(q_ref, k_ref, v_ref, do_ref, dq_ref, dk_ref, dv_ref):
        i = pl.program_id(1)
        qb, dob = q_ref[0], do_ref[0]
        row = lax.broadcasted_iota(I32, (bq, bq), 0)
        col = lax.broadcasted_iota(I32, (bq, bq), 1)
        later = (row > col).astype(BF16)
        from_here = (row >= col).astype(BF16)
        causal = col < row

        @pl.when(i == 0)
        def _():
            dk_ref[...] = jnp.zeros_like(dk_ref)
            dv_ref[...] = jnp.zeros_like(dv_ref)

        def weights(j, c, mask):
            ks = pl.ds(pl.multiple_of(j * bq, bq), bq)
            kb, vb = k_ref[0, ks, :], v_ref[0, ks, :]
            lk, lb = _sb_scores(qb, kb, scale, mask)
            w = jnp.exp(lb + _split_dot(lk, later) + c)
            if mask is not None:
                w = jnp.where(mask, w, 0.0)
            de = _dot(dob, vb, "nt") * w
            return ks, kb, lk, lb, w, de

        def total(j, c, tot, mask):
            _, _, lk, _, _, de = weights(j, c, mask)
            return c + jnp.sum(lk, axis=1, keepdims=True), tot + jnp.sum(de, axis=1, keepdims=True)

        zero = jnp.zeros((bq, 1), F32)
        c, tot = total(i, zero, zero, causal)
        c, tot = lax.fori_loop(0, i, lambda n, ct: total(i - 1 - n, ct[0], ct[1], None), (c, tot))

        def grads(j, c, r, dq, mask):
            ks, kb, lk, lb, w, de = weights(j, c, mask)
            before = tot - (r + _split_dot(de, from_here))
            dz = de * jnp.exp(lk) - jnp.exp(lb) * before
            if mask is not None:
                dz = jnp.where(mask, dz, 0.0)
            dzb = (dz * scale).astype(BF16)
            dk_ref[0, ks, :] += _dot(dzb, qb, "tn")
            dv_ref[0, ks, :] += _dot(w.astype(BF16), dob, "tn")
            dq = dq + _dot(dzb, kb)
            return c + jnp.sum(lk, axis=1, keepdims=True), r + jnp.sum(de, axis=1, keepdims=True), dq

        c, r, dq = grads(i, zero, zero, jnp.zeros((bq, hd), F32), causal)
        c, r, dq = lax.fori_loop(0, i, lambda n, s: grads(i - 1 - n, s[0], s[1], s[2], None), (c, r, dq))
        dq_ref[0] = dq

    blk = pl.BlockSpec((1, bq, hd), lambda h, i: (h, i, 0))
    full = pl.BlockSpec((1, s_dim, hd), lambda h, i: (h, 0, 0))
    out = SDS((h_dim, s_dim, hd), F32)
    return pl.pallas_call(
        body, name=_name(base), grid=(h_dim, nq), in_specs=[blk, full, full, blk], out_specs=[blk, full, full],
        out_shape=[out, out, out], compiler_params=_params(("parallel", "arbitrary")),
    )(q, k, v, do)


def _cmul(ar, ai, br, bi):
    return ar * br - ai * bi, ar * bi + ai * br


def _s5_prep_fn(lr, li, ldt, br, bi):
    dt = jnp.exp(ldt)
    mag = jnp.exp(lr * dt)
    ar, ai = mag * jnp.cos(li * dt), mag * jnp.sin(li * dt)
    den = lr * lr + li * li
    er = ((ar - 1) * lr + ai * li) / den
    ei = (ai * lr - (ar - 1) * li) / den
    return ar, ai, er * br - ei * bi, er * bi + ei * br


def _s5_prep(lr, li, ldt, br, bi, base):
    col, mat = SDS(lr.shape, F32), SDS(br.shape, F32)
    return _rowwise(lambda *a: (_s5_prep_fn(*a), ()), [lr, li, ldt, br, bi], [], [col, col, mat, mat], [], base, tr=512)


def _s5_prep_bwd(lr, li, ldt, br, bi, cots, base):
    col, mat = SDS(lr.shape, F32), SDS(br.shape, F32)

    def bwd(lr, li, ldt, br, bi, c0, c1, c2, c3):
        _, vjp = jax.vjp(_s5_prep_fn, lr, li, ldt, br, bi)
        return vjp((c0, c1, c2, c3)), ()

    return _rowwise(bwd, [lr, li, ldt, br, bi, *cots], [], [col, col, col, mat, mat], [], base, tr=512)


def _scan_tables(ar_ref, ai_ref, sl, conj):
    shape = (SUBLANES, S5_BLOCK_ST)
    a_r = jnp.broadcast_to(ar_ref[:, sl], shape)
    a_i = jnp.broadcast_to(ai_ref[:, sl], shape)
    if conj:
        a_i = -a_i
    a2 = _cmul(a_r, a_i, a_r, a_i)
    a4 = _cmul(*a2, *a2)
    row = lax.broadcasted_iota(I32, shape, 0)
    p_r, p_i, t_r, t_i = a_r, a_i, a_r, a_i
    for k in range(1, SUBLANES):
        p_r, p_i = _cmul(p_r, p_i, a_r, a_i)
        sel = (row == (SUBLANES - 1 - k)) if conj else (row == k)
        t_r, t_i = jnp.where(sel, p_r, t_r), jnp.where(sel, p_i, t_i)
    if conj:
        t_r, t_i = jnp.where(row == SUBLANES - 1, a_r, t_r), jnp.where(row == SUBLANES - 1, a_i, t_i)
    return row, (a_r, a_i), a2, a4, (t_r, t_i)


def _scan_group(xr, xi, row, a1, a2, a4, table, cr, ci, reverse):
    for s, (mr, mi) in ((1, a1), (2, a2), (4, a4)):
        if reverse:
            keep = row < SUBLANES - s
            sr, si = pltpu.roll(xr, SUBLANES - s, 0), pltpu.roll(xi, SUBLANES - s, 0)
        else:
            keep = row >= s
            sr, si = pltpu.roll(xr, s, 0), pltpu.roll(xi, s, 0)
        sr, si = jnp.where(keep, sr, 0.0), jnp.where(keep, si, 0.0)
        xr, xi = xr + mr * sr - mi * si, xi + mr * si + mi * sr
    tr, ti = table
    return xr + tr * cr - ti * ci, xi + tr * ci + ti * cr


def _s5_fwd(u, ar, ai, wr, wi, cr_w, ci_w, dsk, base, chunk=128):
    s_dim, d_dim = u.shape
    nb = d_dim // S5_BLOCK_CH
    n_st = nb * S5_BLOCK_ST
    chunk = _tile(s_dim, chunk, SUBLANES)
    nch = s_dim // chunk

    def body(u_ref, ar_ref, ai_ref, wr_ref, wi_ref, cr_ref, ci_ref, d_ref, y_ref, gy_ref, sr_ref, si_ref,
             xr_s, xi_s, car_r, car_i):
        @pl.when(pl.program_id(0) == 0)
        def _():
            car_r[...] = jnp.zeros_like(car_r)
            car_i[...] = jnp.zeros_like(car_i)

        sr_ref[...] = car_r[...]
        si_ref[...] = car_i[...]
        for b in range(nb):
            ub = u_ref[:, b * S5_BLOCK_CH:(b + 1) * S5_BLOCK_CH].astype(BF16)
            sl = slice(b * S5_BLOCK_ST, (b + 1) * S5_BLOCK_ST)
            xr_s[:, sl] = _dot(ub, wr_ref[b])
            xi_s[:, sl] = _dot(ub, wi_ref[b])
        for b in range(nb):
            sl = slice(b * S5_BLOCK_ST, (b + 1) * S5_BLOCK_ST)
            row, a1, a2, a4, table = _scan_tables(ar_ref, ai_ref, sl, False)

            def group(g, carry, sl=sl, row=row, a1=a1, a2=a2, a4=a4, table=table):
                rows = pl.ds(pl.multiple_of(g * SUBLANES, SUBLANES), SUBLANES)
                xr, xi = _scan_group(xr_s[rows, sl], xi_s[rows, sl], row, a1, a2, a4, table, carry[0], carry[1], False)
                xr_s[rows, sl] = xr
                xi_s[rows, sl] = xi
                last = (SUBLANES, S5_BLOCK_ST)
                return (jnp.broadcast_to(xr[SUBLANES - 1:, :], last), jnp.broadcast_to(xi[SUBLANES - 1:, :], last))

            c_r, c_i = lax.fori_loop(0, chunk // SUBLANES, group, (car_r[:, sl], car_i[:, sl]))
            car_r[:, sl] = c_r
            car_i[:, sl] = c_i
        for b in range(nb):
            sl = slice(b * S5_BLOCK_ST, (b + 1) * S5_BLOCK_ST)
            ch = slice(b * S5_BLOCK_CH, (b + 1) * S5_BLOCK_CH)
            y = _dot(xr_s[:, sl].astype(BF16), cr_ref[b]) - _dot(xi_s[:, sl].astype(BF16), ci_ref[b])
            y = y + d_ref[:, ch] * u_ref[:, ch]
            y_ref[:, ch] = y
            gy_ref[:, ch] = jax.nn.gelu(y).astype(BF16)

    rows = pl.BlockSpec((chunk, d_dim), lambda i: (i, 0))
    vec = pl.BlockSpec((1, n_st), lambda i: (0, 0))
    w_in = pl.BlockSpec((nb, S5_BLOCK_CH, S5_BLOCK_ST), lambda i: (0, 0, 0))
    w_out = pl.BlockSpec((nb, S5_BLOCK_ST, S5_BLOCK_CH), lambda i: (0, 0, 0))
    st = pl.BlockSpec((SUBLANES, n_st), lambda i: (i, 0))
    return pl.pallas_call(
        body, name=_name(base), grid=(nch,),
        in_specs=[rows, vec, vec, w_in, w_in, w_out, w_out, pl.BlockSpec((1, d_dim), lambda i: (0, 0))],
        out_specs=[rows, rows, st, st],
        out_shape=[SDS((s_dim, d_dim), F32), SDS((s_dim, d_dim), BF16),
                   SDS((nch * SUBLANES, n_st), F32), SDS((nch * SUBLANES, n_st), F32)],
        scratch_shapes=[pltpu.VMEM((chunk, n_st), F32), pltpu.VMEM((chunk, n_st), F32),
                        pltpu.VMEM((SUBLANES, n_st), F32), pltpu.VMEM((SUBLANES, n_st), F32)],
        compiler_params=_params(("arbitrary",)),
    )(u, ar, ai, wr, wi, cr_w, ci_w, dsk)


def _s5_bwd(u, dy, st_r, st_i, ar, ai, wr, wi, cr_w, ci_w, dsk, base, chunk):
    s_dim, d_dim = u.shape
    nb = d_dim // S5_BLOCK_CH
    n_st = nb * S5_BLOCK_ST
    nch = s_dim // chunk
    ng = chunk // SUBLANES

    def body(u_ref, dy_ref, sr_ref, si_ref, ar_ref, ai_ref, wr_ref, wi_ref, cr_ref, ci_ref, d_ref,
             du_ref, dwr_ref, dwi_ref, dcr_ref, dci_ref, dar_ref, dai_ref, dd_ref,
             xr_s, xi_s, lr_s, li_s, car_r, car_i):
        first = pl.program_id(0) == 0

        @pl.when(first)
        def _():
            car_r[...] = jnp.zeros_like(car_r)
            car_i[...] = jnp.zeros_like(car_i)
            for ref in (dwr_ref, dwi_ref, dcr_ref, dci_ref, dar_ref, dai_ref, dd_ref):
                ref[...] = jnp.zeros_like(ref)

        xr_s[pl.ds(0, SUBLANES), :] = sr_ref[...]
        xi_s[pl.ds(0, SUBLANES), :] = si_ref[...]
        body_rows = pl.ds(SUBLANES, chunk)
        dyb = dy_ref[...]
        dd_ref[...] += jnp.sum(dyb * u_ref[...], axis=0, keepdims=True)
        for b in range(nb):
            ch = slice(b * S5_BLOCK_CH, (b + 1) * S5_BLOCK_CH)
            sl = slice(b * S5_BLOCK_ST, (b + 1) * S5_BLOCK_ST)
            ub = u_ref[:, ch].astype(BF16)
            dyc = dyb[:, ch].astype(BF16)
            xr_s[body_rows, sl] = _dot(ub, wr_ref[b])
            xi_s[body_rows, sl] = _dot(ub, wi_ref[b])
            lr_s[:, sl] = _dot(dyc, cr_ref[b], "nt")
            li_s[:, sl] = -_dot(dyc, ci_ref[b], "nt")
        for b in range(nb):
            sl = slice(b * S5_BLOCK_ST, (b + 1) * S5_BLOCK_ST)
            row, a1, a2, a4, table = _scan_tables(ar_ref, ai_ref, sl, False)

            def fwd_group(g, carry, sl=sl, row=row, a1=a1, a2=a2, a4=a4, table=table):
                rows = pl.ds(pl.multiple_of((g + 1) * SUBLANES, SUBLANES), SUBLANES)
                xr, xi = _scan_group(xr_s[rows, sl], xi_s[rows, sl], row, a1, a2, a4, table, carry[0], carry[1], False)
                xr_s[rows, sl] = xr
                xi_s[rows, sl] = xi
                last = (SUBLANES, S5_BLOCK_ST)
                return (jnp.broadcast_to(xr[SUBLANES - 1:, :], last), jnp.broadcast_to(xi[SUBLANES - 1:, :], last))

            lax.fori_loop(0, ng, fwd_group, (sr_ref[:, sl], si_ref[:, sl]))
            row, a1, a2, a4, table = _scan_tables(ar_ref, ai_ref, sl, True)

            def bwd_group(n, carry, sl=sl, row=row, a1=a1, a2=a2, a4=a4, table=table):
                g = ng - 1 - n
                c_r, c_i, acc_r, acc_i = carry
                rows = pl.ds(pl.multiple_of(g * SUBLANES, SUBLANES), SUBLANES)
                lr, li = _scan_group(lr_s[rows, sl], li_s[rows, sl], row, a1, a2, a4, table, c_r, c_i, True)
                lr_s[rows, sl] = lr
                li_s[rows, sl] = li
                prev = pl.ds(pl.multiple_of(g * SUBLANES, SUBLANES), SUBLANES)
                here = pl.ds(pl.multiple_of((g + 1) * SUBLANES, SUBLANES), SUBLANES)
                px_r = jnp.where(row >= 1, pltpu.roll(xr_s[here, sl], 1, 0), pltpu.roll(xr_s[prev, sl], 1, 0))
                px_i = jnp.where(row >= 1, pltpu.roll(xi_s[here, sl], 1, 0), pltpu.roll(xi_s[prev, sl], 1, 0))
                acc_r = acc_r + lr * px_r + li * px_i
                acc_i = acc_i + li * px_r - lr * px_i
                first_row = (SUBLANES, S5_BLOCK_ST)
                return (jnp.broadcast_to(lr[:1, :], first_row), jnp.broadcast_to(li[:1, :], first_row), acc_r, acc_i)

            zero = jnp.zeros((SUBLANES, S5_BLOCK_ST), F32)
            c_r, c_i, acc_r, acc_i = lax.fori_loop(0, ng, bwd_group, (car_r[:, sl], car_i[:, sl], zero, zero))
            car_r[:, sl] = c_r
            car_i[:, sl] = c_i
            dar_ref[:, sl] += acc_r
            dai_ref[:, sl] += acc_i
        for b in range(nb):
            ch = slice(b * S5_BLOCK_CH, (b + 1) * S5_BLOCK_CH)
            sl = slice(b * S5_BLOCK_ST, (b + 1) * S5_BLOCK_ST)
            ub = u_ref[:, ch].astype(BF16)
            dyc = dyb[:, ch].astype(BF16)
            lrb, lib = lr_s[:, sl].astype(BF16), li_s[:, sl].astype(BF16)
            du_ref[:, ch] = (_dot(lrb, wr_ref[b], "nt") + _dot(lib, wi_ref[b], "nt") + d_ref[:, ch] * dyb[:, ch])
            dwr_ref[b] += _dot(ub, lrb, "tn")
            dwi_ref[b] += _dot(ub, lib, "tn")
            dcr_ref[b] += _dot(xr_s[body_rows, sl].astype(BF16), dyc, "tn")
            dci_ref[b] -= _dot(xi_s[body_rows, sl].astype(BF16), dyc, "tn")

    rev = lambda i: (nch - 1 - i, 0)
    rows = pl.BlockSpec((chunk, d_dim), rev)
    st = pl.BlockSpec((SUBLANES, n_st), rev)
    vec = pl.BlockSpec((1, n_st), lambda i: (0, 0))
    w_in = pl.BlockSpec((nb, S5_BLOCK_CH, S5_BLOCK_ST), lambda i: (0, 0, 0))
    w_out = pl.BlockSpec((nb, S5_BLOCK_ST, S5_BLOCK_CH), lambda i: (0, 0, 0))
    acc8 = pl.BlockSpec((SUBLANES, n_st), lambda i: (0, 0))
    dvec = pl.BlockSpec((1, d_dim), lambda i: (0, 0))
    return pl.pallas_call(
        body, name=_name(base), grid=(nch,),
        in_specs=[rows, rows, st, st, vec, vec, w_in, w_in, w_out, w_out, dvec],
        out_specs=[rows, w_in, w_in, w_out, w_out, acc8, acc8, dvec],
        out_shape=[SDS((s_dim, d_dim), F32),
                   SDS((nb, S5_BLOCK_CH, S5_BLOCK_ST), F32), SDS((nb, S5_BLOCK_CH, S5_BLOCK_ST), F32),
                   SDS((nb, S5_BLOCK_ST, S5_BLOCK_CH), F32), SDS((nb, S5_BLOCK_ST, S5_BLOCK_CH), F32),
                   SDS((SUBLANES, n_st), F32), SDS((SUBLANES, n_st), F32), SDS((1, d_dim), F32)],
        scratch_shapes=[pltpu.VMEM((chunk + SUBLANES, n_st), F32), pltpu.VMEM((chunk + SUBLANES, n_st), F32),
                        pltpu.VMEM((chunk, n_st), F32), pltpu.VMEM((chunk, n_st), F32),
                        pltpu.VMEM((SUBLANES, n_st), F32), pltpu.VMEM((SUBLANES, n_st), F32)],
        compiler_params=_params(("arbitrary",)),
    )(u, dy, st_r, st_i, ar, ai, wr, wi, cr_w, ci_w, dsk)


def _block_diag(w, transpose):
    g = w.shape[0]
    nb = g // S5_BLOCK_GROUPS
    eye = jnp.eye(S5_BLOCK_GROUPS, dtype=w.dtype)
    if transpose:
        w = w.reshape(nb, S5_BLOCK_GROUPS, S5_GROUP, S5_STATE)
        return jnp.einsum("bgcp,gh->bgphc", w, eye).reshape(nb, S5_BLOCK_ST, S5_BLOCK_CH)
    w = w.reshape(nb, S5_BLOCK_GROUPS, S5_STATE, S5_GROUP)
    return jnp.einsum("bgpc,gh->bgchp", w, eye).reshape(nb, S5_BLOCK_CH, S5_BLOCK_ST)


def _block_diag_extract(w, transpose):
    nb = w.shape[0]
    eye = jnp.eye(S5_BLOCK_GROUPS, dtype=w.dtype)
    if transpose:
        w = w.reshape(nb, S5_BLOCK_GROUPS, S5_STATE, S5_BLOCK_GROUPS, S5_GROUP)
        return jnp.einsum("bgphc,gh->bgcp", w, eye).reshape(nb * S5_BLOCK_GROUPS, S5_GROUP, S5_STATE)
    w = w.reshape(nb, S5_BLOCK_GROUPS, S5_GROUP, S5_BLOCK_GROUPS, S5_STATE)
    return jnp.einsum("bgchp,gh->bgpc", w, eye).reshape(nb * S5_BLOCK_GROUPS, S5_STATE, S5_GROUP)


def _dwconv_fwd(h, w, b, base, tr=512):
    s_dim, d_dim = h.shape
    tr = _tile(s_dim, tr, CONV_HALO)
    per = tr // CONV_HALO

    def body(h_ref, halo_ref, w_ref, b_ref, o_ref, buf):
        i = pl.program_id(0)
        buf[pl.ds(0, CONV_HALO), :] = jnp.where(i > 0, halo_ref[...], 0.0)
        buf[pl.ds(CONV_HALO, tr), :] = h_ref[...]
        acc = jnp.broadcast_to(b_ref[...], (tr, d_dim))
        for k in range(CONV_WIDTH):
            acc = acc + w_ref[pl.ds(k, 1), :] * buf[pl.ds(CONV_HALO - (CONV_WIDTH - 1) + k, tr), :]
        o_ref[...] = acc

    return pl.pallas_call(
        body, name=_name(base), grid=(s_dim // tr,),
        in_specs=[pl.BlockSpec((tr, d_dim), lambda i: (i, 0)),
                  pl.BlockSpec((CONV_HALO, d_dim), lambda i: (jnp.maximum(i * per - 1, 0), 0)),
                  pl.BlockSpec((CONV_WIDTH, d_dim), lambda i: (0, 0)), pl.BlockSpec((1, d_dim), lambda i: (0, 0))],
        out_specs=pl.BlockSpec((tr, d_dim), lambda i: (i, 0)), out_shape=SDS((s_dim, d_dim), F32),
        scratch_shapes=[pltpu.VMEM((tr + CONV_HALO, d_dim), F32)], compiler_params=_params(("arbitrary",)),
    )(h, h, w, b)


def _dwconv_bwd(h, dout, w, base, tr=512):
    s_dim, d_dim = h.shape
    tr = _tile(s_dim, tr, CONV_HALO)
    per = tr // CONV_HALO
    n = s_dim // tr
    last_halo = s_dim // CONV_HALO - 1

    def body(h_ref, hhalo_ref, d_ref, dhalo_ref, w_ref, dh_ref, dw_ref, db_ref, hbuf, dbuf):
        i = pl.program_id(0)
        hbuf[pl.ds(0, CONV_HALO), :] = jnp.where(i > 0, hhalo_ref[...], 0.0)
        hbuf[pl.ds(CONV_HALO, tr), :] = h_ref[...]
        dbuf[pl.ds(0, tr), :] = d_ref[...]
        dbuf[pl.ds(tr, CONV_HALO), :] = jnp.where(i < n - 1, dhalo_ref[...], 0.0)
        dout_t = d_ref[...]
        acc = jnp.zeros((tr, d_dim), F32)
        dws = []
        for k in range(CONV_WIDTH):
            acc = acc + w_ref[pl.ds(k, 1), :] * dbuf[pl.ds(CONV_WIDTH - 1 - k, tr), :]
            shifted = hbuf[pl.ds(CONV_HALO - (CONV_WIDTH - 1) + k, tr), :]
            dws.append(jnp.sum(dout_t * shifted, axis=0, keepdims=True))
        dh_ref[...] = acc
        dw_t = jnp.concatenate(dws, axis=0)
        db_t = jnp.sum(dout_t, axis=0, keepdims=True)

        @pl.when(i == 0)
        def _():
            dw_ref[...] = dw_t
            db_ref[...] = db_t

        @pl.when(i > 0)
        def _():
            dw_ref[...] += dw_t
            db_ref[...] += db_t

    tile = pl.BlockSpec((tr, d_dim), lambda i: (i, 0))
    before = pl.BlockSpec((CONV_HALO, d_dim), lambda i: (jnp.maximum(i * per - 1, 0), 0))
    after = pl.BlockSpec((CONV_HALO, d_dim), lambda i: (jnp.minimum((i + 1) * per, last_halo), 0))
    return pl.pallas_call(
        body, name=_name(base), grid=(n,),
        in_specs=[tile, before, tile, after, pl.BlockSpec((CONV_WIDTH, d_dim), lambda i: (0, 0))],
        out_specs=[tile, pl.BlockSpec((CONV_WIDTH, d_dim), lambda i: (0, 0)), pl.BlockSpec((1, d_dim), lambda i: (0, 0))],
        out_shape=[SDS((s_dim, d_dim), F32), SDS((CONV_WIDTH, d_dim), F32), SDS((1, d_dim), F32)],
        scratch_shapes=[pltpu.VMEM((tr + CONV_HALO, d_dim), F32), pltpu.VMEM((tr + CONV_HALO, d_dim), F32)],
        compiler_params=_params(("arbitrary",)),
    )(h, h, dout, dout, w)


def _heads(t, h_dim):
    s_dim = t.shape[0]
    return t.reshape(s_dim, h_dim, HEAD_DIM).transpose(1, 0, 2)


def _unheads(t):
    h_dim, s_dim, hd = t.shape
    return t.transpose(1, 0, 2).reshape(s_dim, h_dim * hd)


def _local_step(x, mod, w, target):
    s_dim, d_dim = x.shape
    depth = mod.shape[0]
    h_dim = d_dim // HEAD_DIM
    row = lambda dt=F32, n=d_dim: SDS((s_dim, n), dt)
    vec = lambda n=d_dim: SDS((1, n), F32)
    mods = [[mod[l:l + 1, k * d_dim:(k + 1) * d_dim] for k in range(6)] for l in range(depth)]
    ng = lambda l, k: w["norm_g"][l, k:k + 1, :]
    saved = []
    h = x
    for l in range(depth):
        sh_m, sc_m, g_m, sh_f, sc_f, g_f = mods[l]
        kind, j = l % 3, l // 3
        s = {"h0": h}
        if kind == 1:
            (u,) = _rowwise(lambda h, g, a, b: ((_pre(h, g, a, b),), ()), [h], [ng(l, 0), sh_m, sc_m], [row()], [], "pre")
        else:
            (u,) = _rowwise(lambda h, g, a, b: ((_pre(h, g, a, b),), ()), [h], [ng(l, 0), sh_m, sc_m], [row(BF16)], [], "pre")
        s["u"] = u
        bias = None
        if kind == 0:
            qkv = _matmul(u, w["sb_w_qkv"][j], "nn", BF16, "qkv")
            q, k, v = (_heads(qkv[:, i * d_dim:(i + 1) * d_dim], h_dim) for i in range(3))
            o = _unheads(_sb_fwd(q, k, v, "sb_fwd")).astype(BF16)
            m = _matmul(o, w["sb_w_o"][j], "nn", F32, "wo")
            s.update(q=q, k=k, v=v, o=o)
        elif kind == 1:
            p = w["s5"][j]
            y_pre, gy, st_r, st_i = _s5_fwd(u, p["ar"], p["ai"], p["wr"], p["wi"], p["cr"], p["ci"], w["s5_d"][j:j + 1],
                                            "s5_fwd")
            t = _matmul(gy, w["s5_w_glu"][j], "nn", F32, "s5_glu")
            b_glu = w["s5_b_glu"][j:j + 1]
            (m,) = _rowwise(lambda t, b: ((_glu(t, b),), ()), [t], [b_glu], [row()], [], "s5_gate")
            s.update(y_pre=y_pre, gy=gy, st_r=st_r, st_i=st_i, t=t)
        else:
            t = _matmul(u, w["cv_w_pw1"][j], "nn", F32, "pw1")
            (hg,) = _rowwise(lambda t, b: ((_glu(t, b),), ()), [t], [w["cv_b_pw1"][j:j + 1]], [row()], [], "cv_glu")
            hc = _dwconv_fwd(hg, w["cv_w_dw"][j], w["cv_b_dw"][j:j + 1], "dwconv")
            (hs,) = _rowwise(lambda h, g, b: ((_ln_silu(h, g, b),), ()), [hc],
                             [w["cv_ln_g"][j:j + 1], w["cv_ln_b"][j:j + 1]], [row(BF16)], [], "cv_ln")
            m = _matmul(hs, w["cv_w_pw2"][j], "nn", F32, "pw2")
            bias = w["cv_b_pw2"][j:j + 1]
            s.update(t=t, hg=hg, hc=hc, hs=hs)
        s["m"] = m
        if bias is None:
            h1, u2 = _rowwise(
                lambda h, m, gate, g1, g2, a, b: ((lambda h1: (h1, _pre(h1, g2, a, b)))(_post(h, m, gate, g1)), ()),
                [h, m], [g_m, ng(l, 1), ng(l, 2), sh_f, sc_f], [row(), row(BF16)], [], "post_pre")
        else:
            h1, u2 = _rowwise(
                lambda h, m, gate, g1, g2, a, b, bb: ((lambda h1: (h1, _pre(h1, g2, a, b)))(_post(h, m + bb, gate, g1)), ()),
                [h, m], [g_m, ng(l, 1), ng(l, 2), sh_f, sc_f, bias], [row(), row(BF16)], [], "post_pre")
        f_dim = w["ffn_w_gu"].shape[2] // 2
        gu = _matmul(u2, w["ffn_w_gu"][l], "nn", BF16, "ffn_gu")
        (act,) = _rowwise(lambda gu: ((_swish_gate(_f(gu)),), ()), [gu], [], [row(BF16, f_dim)], [], "ffn_act")
        f = _matmul(act, w["ffn_w_down"][l], "nn", F32, "ffn_down")
        (h,) = _rowwise(lambda h, m, gate, g: ((_post(h, m, gate, g),), ()), [h1, f], [g_f, ng(l, 3)], [row()], [], "post")
        s.update(h1=h1, u2=u2, gu=gu, act=act, f=f, bias=bias)
        saved.append(s)
    dh, loss_cols = _rowwise(
        lambda y, t: ((((y - t) * (1.0 / d_dim)),), (jnp.sum(jnp.square(y - t), axis=0, keepdims=True),)),
        [h, target], [], [row()], [vec()], "loss")
    grads = {}
    dmod = [[None] * 6 for _ in range(depth)]
    dnorm = [[None] * 4 for _ in range(depth)]

    def add(name, idx, val):
        grads.setdefault(name, {})[idx] = val

    for l in reversed(range(depth)):
        sh_m, sc_m, g_m, sh_f, sc_f, g_f = mods[l]
        kind, j = l % 3, l // 3
        s = saved[l]

        def post_bwd(h, m, dh, gate, g):
            _, vjp = jax.vjp(_post, h, m, gate, g)
            _, dm, dgate, dg = vjp(dh)
            return (dm,), (dgate, dg)

        df, dmod[l][5], dnorm[l][3] = _rowwise(post_bwd, [s["h1"], s["f"], dh], [g_f, ng(l, 3)],
                                               [row(BF16)], [vec(), vec()], "post_bwd")
        f_dim = s["act"].shape[1]
        add("ffn_w_down", l, _matmul(s["act"], df, "tn", F32, "d_down"))
        dact = _matmul(df, w["ffn_w_down"][l], "nt", BF16, "d_act")

        def act_bwd(gu, da):
            _, vjp = jax.vjp(_swish2, *_halves(_f(gu)))
            return (jnp.concatenate(vjp(_f(da)), axis=1),), ()

        (dgu,) = _rowwise(act_bwd, [s["gu"], dact], [], [row(BF16, 2 * f_dim)], [], "act_bwd")
        add("ffn_w_gu", l, _matmul(s["u2"], dgu, "tn", F32, "d_gu"))
        du2 = _matmul(dgu, w["ffn_w_gu"][l], "nt", F32, "d_u2")
        bias = s["bias"]

        def post_pre_bwd(h, m, dh1, du2, gate, g1, g2, a, b, *bb):
            def fwd(h, m, gate, g1, g2, a, b):
                mm = m + bb[0] if bb else m
                h1 = _post(h, mm, gate, g1)
                return h1, _pre(h1, g2, a, b)
            _, vjp = jax.vjp(fwd, h, m, gate, g1, g2, a, b)
            dh0, dm, dgate, dg1, dg2, da, db = vjp((dh1, du2))
            return (dh0, dm), (dgate, dg1, dg2, da, db)

        vecs = [g_m, ng(l, 1), ng(l, 2), sh_f, sc_f] + ([bias] if bias is not None else [])
        dh, dm, dmod[l][2], dnorm[l][1], dnorm[l][2], dmod[l][3], dmod[l][4] = _rowwise(
            post_pre_bwd, [s["h0"], s["m"], dh, du2], vecs, [row(), row()], [vec()] * 5, "post_pre_bwd")
        if kind == 0:
            (dmb,) = _rowwise(lambda a: ((a,), ()), [dm], [], [row(BF16)], [], "cast")
            add("sb_w_o", j, _matmul(s["o"], dmb, "tn", F32, "d_wo"))
            do = _heads(_matmul(dmb, w["sb_w_o"][j], "nt", BF16, "d_o"), h_dim)
            dq, dk, dv = _sb_bwd(s["q"], s["k"], s["v"], do, "sb_bwd")
            dqkv = jnp.concatenate([_unheads(t) for t in (dq, dk, dv)], axis=1).astype(BF16)
            add("sb_w_qkv", j, _matmul(s["u"], dqkv, "tn", F32, "d_wqkv"))
            du = _matmul(dqkv, w["sb_w_qkv"][j], "nt", F32, "d_u")
        elif kind == 1:
            p = w["s5"][j]
            b_glu = w["s5_b_glu"][j:j + 1]

            def gate_bwd(t, dm, b):
                dt, db = _glu_bwd(t, dm, b)
                return (dt,), (db,)

            dt, db_glu = _rowwise(gate_bwd, [s["t"], dm], [b_glu], [row(BF16, 2 * d_dim)], [vec(2 * d_dim)], "s5_gate_bwd")
            add("s5_b_glu", j, db_glu)
            add("s5_w_glu", j, _matmul(s["gy"], dt, "tn", F32, "d_wglu"))
            dgy = _matmul(dt, w["s5_w_glu"][j], "nt", F32, "d_gy")

            def gelu_bwd(y, dg):
                _, vjp = jax.vjp(jax.nn.gelu, y)
                return (vjp(dg)[0],), ()

            (dy,) = _rowwise(gelu_bwd, [s["y_pre"], dgy], [], [row()], [], "gelu_bwd")
            chunk = s_dim // (s["st_r"].shape[0] // SUBLANES)
            du, dwr, dwi, dcr, dci, dar, dai, dd = _s5_bwd(
                s["u"], dy, s["st_r"], s["st_i"], p["ar"], p["ai"], p["wr"], p["wi"], p["cr"], p["ci"],
                w["s5_d"][j:j + 1], "s5_bwd", chunk)
            add("s5_d", j, dd)
            add("s5_raw", j, (dwr, dwi, dcr, dci, dar, dai))
        else:
            (dmb,) = _rowwise(lambda a: ((a,), ()), [dm], [], [row(BF16)], [], "cast")
            add("cv_w_pw2", j, _matmul(s["hs"], dmb, "tn", F32, "d_pw2"))
            dhs = _matmul(dmb, w["cv_w_pw2"][j], "nt", F32, "d_hs")
            ln_g, ln_b = w["cv_ln_g"][j:j + 1], w["cv_ln_b"][j:j + 1]

            def ln_bwd(hc, dhs, dm, g, b):
                _, vjp = jax.vjp(_ln_silu, hc, g, b)
                dhc, dg, db = vjp(dhs)
                return (dhc,), (dg, db, jnp.sum(dm, axis=0, keepdims=True))

            dhc, dln_g, dln_b, db_pw2 = _rowwise(ln_bwd, [s["hc"], dhs, dm], [ln_g, ln_b], [row()], [vec()] * 3, "ln_bwd")
            add("cv_ln_g", j, dln_g)
            add("cv_ln_b", j, dln_b)
            add("cv_b_pw2", j, db_pw2)
            dhg, dw_dw, db_dw = _dwconv_bwd(s["hg"], dhc, w["cv_w_dw"][j], "dwconv_bwd")
            add("cv_w_dw", j, dw_dw)
            add("cv_b_dw", j, db_dw)
            b_pw1 = w["cv_b_pw1"][j:j + 1]

            def glu_bwd(t, dhg, b):
                dt, db = _glu_bwd(t, dhg, b)
                return (dt,), (db,)

            dt, db_pw1 = _rowwise(glu_bwd, [s["t"], dhg], [b_pw1], [row(BF16, 2 * d_dim)], [vec(2 * d_dim)], "cv_glu_bwd")
            add("cv_b_pw1", j, db_pw1)
            add("cv_w_pw1", j, _matmul(s["u"], dt, "tn", F32, "d_pw1"))
            du = _matmul(dt, w["cv_w_pw1"][j], "nt", F32, "d_u")

        def pre_bwd(h, dh, du, g, a, b):
            _, vjp = jax.vjp(_pre, h, g, a, b)
            dh0, dg, da, db = vjp(du)
            return (dh + dh0,), (dg, da, db)

        dh, dnorm[l][0], dmod[l][0], dmod[l][1] = _rowwise(pre_bwd, [s["h0"], dh, du], [ng(l, 0), sh_m, sc_m],
                                                            [row()], [vec()] * 3, "pre_bwd")
    grads["norm_g"] = jnp.stack([jnp.concatenate(r, axis=0) for r in dnorm])
    dmod_arr = jnp.concatenate([jnp.concatenate(r, axis=1) for r in dmod], axis=0)
    return loss_cols, dh, dmod_arr, grads


def _position():
    return lax.axis_index("x"), lax.axis_index("y"), lax.axis_index("c")


def _all_gather8(x, base, in_vmem, half_of_two=False):
    m = x.shape[0] // 2 if half_of_two else x.shape[0]
    n = x.shape[1]
    space = pltpu.VMEM if in_vmem else pl.ANY

    def body(x_ref, out_ref, send_sems, recv_sems, local_sem):
        px, py, pc = _position()
        me, sibling = (px, py, pc), (px, py, 1 - pc)
        chips = [(1 - px, py), (px, 1 - py), (1 - px, 1 - py)]
        src_mine = x_ref.at[pl.ds(pc * m, m), :] if half_of_two else x_ref

        def rows(bx, by, bc):
            return out_ref.at[pl.ds((4 * bx + 2 * by + bc) * m, m), :]

        def copy(k, block, to, src=None):
            return pltpu.make_async_remote_copy(
                src_ref=rows(*block) if src is None else src, dst_ref=rows(*block),
                send_sem=send_sems.at[k], recv_sem=recv_sems.at[k], device_id=to, device_id_type=MESH)

        mine = pltpu.make_async_copy(src_mine, rows(*me), local_sem)
        mine.start()
        first = [copy(0, me, sibling, src=src_mine)]
        first += [copy(1 + j, me, (*chip, pc), src=src_mine) for j, chip in enumerate(chips)]
        for cp in first:
            cp.start()
        passed = [copy(4 + j, (*chip, pc), sibling) for j, chip in enumerate(chips)]
        for j, chip in enumerate(chips):
            copy(1 + j, (*chip, pc), me).wait_recv()
            passed[j].start()
        copy(0, sibling, me).wait_recv()
        for j, chip in enumerate(chips):
            copy(4 + j, (*chip, 1 - pc), me).wait_recv()
        for cp in first + passed:
            cp.wait_send()
        mine.wait()

    return pl.pallas_call(
        body, name=_name(base), out_shape=SDS((8 * m, n), x.dtype),
        in_specs=[pl.BlockSpec(memory_space=space)], out_specs=pl.BlockSpec(memory_space=space),
        scratch_shapes=[pltpu.SemaphoreType.DMA((7,)), pltpu.SemaphoreType.DMA((7,)), pltpu.SemaphoreType.DMA],
    )(x)


def _sibling_halves(g, base):
    _, r, n = g.shape

    def body(g_ref, recv_ref, send_sems, recv_sems):
        px, py, pc = _position()
        copies = [pltpu.make_async_remote_copy(
            src_ref=g_ref.at[2 * j + (1 - pc)], dst_ref=recv_ref.at[j], send_sem=send_sems.at[j],
            recv_sem=recv_sems.at[j], device_id=(px, py, 1 - pc), device_id_type=MESH) for j in range(4)]
        for cp in copies:
            cp.start()
        for cp in copies:
            cp.wait_recv()
        for cp in copies:
            cp.wait_send()

    return pl.pallas_call(
        body, name=_name(base), out_shape=SDS((4, r, n), g.dtype),
        in_specs=[pl.BlockSpec(memory_space=pl.ANY)], out_specs=pl.BlockSpec(memory_space=pl.ANY),
        scratch_shapes=[pltpu.SemaphoreType.DMA((4,)), pltpu.SemaphoreType.DMA((4,))],
    )(g)


def _chip_exchange(t, base):
    _, r, n = t.shape

    def body(t_ref, recv_ref, send_sems, recv_sems):
        px, py, pc = _position()
        chips = [(1 - px, py), (px, 1 - py), (1 - px, 1 - py)]
        copies = [pltpu.make_async_remote_copy(
            src_ref=t_ref.at[2 * cx + cy], dst_ref=recv_ref.at[k], send_sem=send_sems.at[k],
            recv_sem=recv_sems.at[k], device_id=(cx, cy, pc), device_id_type=MESH) for k, (cx, cy) in enumerate(chips)]
        for cp in copies:
            cp.start()
        for cp in copies:
            cp.wait_recv()
        for cp in copies:
            cp.wait_send()

    return pl.pallas_call(
        body, name=_name(base), out_shape=SDS((3, r, n), t.dtype),
        in_specs=[pl.BlockSpec(memory_space=pl.ANY)], out_specs=pl.BlockSpec(memory_space=pl.ANY),
        scratch_shapes=[pltpu.SemaphoreType.DMA((3,)), pltpu.SemaphoreType.DMA((3,))],
    )(t)


def _sibling_share(r_half, base):
    r, n = r_half.shape

    def body(r_ref, out_ref, send_sem, recv_sem, local_sem):
        px, py, pc = _position()
        mine = pltpu.make_async_copy(r_ref, out_ref.at[pc], local_sem)
        mine.start()
        send = pltpu.make_async_remote_copy(src_ref=r_ref, dst_ref=out_ref.at[pc], send_sem=send_sem, recv_sem=recv_sem,
                                            device_id=(px, py, 1 - pc), device_id_type=MESH)
        send.start()
        pltpu.make_async_remote_copy(src_ref=r_ref, dst_ref=out_ref.at[1 - pc], send_sem=send_sem, recv_sem=recv_sem,
                                     device_id=(px, py, 1 - pc), device_id_type=MESH).wait_recv()
        send.wait_send()
        mine.wait()

    return pl.pallas_call(
        body, name=_name(base), out_shape=SDS((2, r, n), r_half.dtype),
        in_specs=[pl.BlockSpec(memory_space=pl.ANY)], out_specs=pl.BlockSpec(memory_space=pl.ANY),
        scratch_shapes=[pltpu.SemaphoreType.DMA, pltpu.SemaphoreType.DMA, pltpu.SemaphoreType.DMA],
    )(r_half)


def _add_own_half(g, recv, core, base, tr=512):
    _, r, n = g.shape
    tr = _tile(r, tr, SUBLANES)

    def body(core_ref, g_ref, r_ref, o_ref):
        o_ref[...] = g_ref[...] + r_ref[...]

    spec = pltpu.PrefetchScalarGridSpec(
        num_scalar_prefetch=1, grid=(4, r // tr),
        in_specs=[pl.BlockSpec((1, tr, n), lambda j, i, core: (2 * j + core[0], i, 0)),
                  pl.BlockSpec((1, tr, n), lambda j, i, core: (j, i, 0))],
        out_specs=pl.BlockSpec((1, tr, n), lambda j, i, core: (j, i, 0)))
    return pl.pallas_call(body, name=_name(base), grid_spec=spec, out_shape=SDS((4, r, n), F32),
                          compiler_params=_params(("parallel", "parallel")))(core, g, recv)


def _add_chips(t, recv, chip, base, tr=512):
    _, r, n = t.shape
    tr = _tile(r, tr, SUBLANES)

    def body(chip_ref, t_ref, r_ref, o_ref):
        o_ref[...] = ((t_ref[0] + r_ref[0]) + r_ref[1]) + r_ref[2]

    spec = pltpu.PrefetchScalarGridSpec(
        num_scalar_prefetch=1, grid=(r // tr,),
        in_specs=[pl.BlockSpec((1, tr, n), lambda i, chip: (chip[0], i, 0)),
                  pl.BlockSpec((3, tr, n), lambda i, chip: (0, i, 0))],
        out_specs=pl.BlockSpec((tr, n), lambda i, chip: (i, 0)))
    return pl.pallas_call(body, name=_name(base), grid_spec=spec, out_shape=SDS((r, n), F32),
                          compiler_params=_params(("parallel",)))(chip, t, recv)


def _sum8(g, base):
    r, n = g.shape[0] // 8, g.shape[1]

    def body(g_ref, o_ref):
        acc = g_ref[pl.ds(0, r), :]
        for d in range(1, 8):
            acc = acc + g_ref[pl.ds(d * r, r), :]
        o_ref[...] = acc

    return pl.pallas_call(body, name=_name(base), out_shape=SDS((r, n), F32), compiler_params=_params())(g)


def _adamw_math(w, g, m, v):
    m = ADAM_B1 * m + (1.0 - ADAM_B1) * g
    v = ADAM_B2 * v + (1.0 - ADAM_B2) * jnp.square(g)
    m_hat = m / (1.0 - ADAM_B1 ** ADAM_STEP)
    v_hat = v / (1.0 - ADAM_B2 ** ADAM_STEP)
    delta = -ADAM_LR * (m_hat / (jnp.sqrt(v_hat) + ADAM_EPS) + ADAM_WD * w)
    return delta, m, v


def _adamw(w, g, m, v, base):
    out = SDS(w.shape, F32)
    return _rowwise(lambda w, g, m, v: (_adamw_math(w, g, m, v), ()), [w, g, m, v], [], [out, out, out], [], base)


def _mod_grad_adamw(sc_t, dmod, w, m, v, base, tr=256):
    l_dim, d_dim, n = w.shape
    tr = _tile(d_dim, tr, SUBLANES)

    def body(sc_ref, dm_ref, w_ref, m_ref, v_ref, g_ref, d_ref, nm_ref, nv_ref):
        g = lax.dot_general(sc_ref[...], dm_ref[0], _DIMS["nn"], preferred_element_type=F32,
                            precision=lax.Precision.HIGHEST)
        delta, nm, nv = _adamw_math(w_ref[0], g, m_ref[0], v_ref[0])
        g_ref[0], d_ref[0], nm_ref[0], nv_ref[0] = g, delta, nm, nv

    tile = pl.BlockSpec((1, tr, n), lambda l, i: (l, i, 0))
    out = SDS(w.shape, F32)
    return pl.pallas_call(
        body, name=_name(base), grid=(l_dim, d_dim // tr),
        in_specs=[pl.BlockSpec((tr, 8), lambda l, i: (i, 0)), pl.BlockSpec((1, 8, n), lambda l, i: (l, 0, 0)),
                  tile, tile, tile],
        out_specs=[tile] * 4, out_shape=[out] * 4, compiler_params=_params(("parallel", "parallel")),
    )(sc_t, dmod, w, m, v)


def _mod_forward(c_all, w_mod, b_mod, base):
    l_dim, d_dim, n = w_mod.shape

    def body(c_ref, w_ref, b_ref, o_ref):
        o_ref[0] = lax.dot_general(jax.nn.silu(c_ref[...]), w_ref[0], _DIMS["nn"], preferred_element_type=F32,
                                   precision=lax.Precision.HIGHEST) + b_ref[0]

    return pl.pallas_call(
        body, name=_name(base), grid=(l_dim,),
        in_specs=[pl.BlockSpec((8, d_dim), lambda l: (0, 0)), pl.BlockSpec((1, d_dim, n), lambda l: (l, 0, 0)),
                  pl.BlockSpec((1, 1, n), lambda l: (l, 0, 0))],
        out_specs=pl.BlockSpec((1, 8, n), lambda l: (l, 0, 0)), out_shape=SDS((l_dim, 8, n), F32),
        compiler_params=_params(("parallel",)),
    )(c_all, w_mod, b_mod)


def _pack(arrays, dtype, row_mult):
    flat = jnp.concatenate([a.reshape(-1).astype(dtype) for a in arrays])
    quantum = PACK_LANES * row_mult
    pad = (-flat.shape[0]) % quantum
    return jnp.pad(flat, (0, pad)).reshape(-1, PACK_LANES)


def _unpack(flat, shapes):
    out, off = [], 0
    for shp in shapes:
        n = math.prod(shp)
        out.append(flat[off:off + n].reshape(shp))
        off += n
    return out


_WEIGHTS = ["norm_g", "w_mod", "b_mod", "sb_w_qkv", "sb_w_o", "s5_lam_re", "s5_lam_im", "s5_log_dt", "s5_b_re",
            "s5_b_im", "s5_c_re", "s5_c_im", "s5_d", "s5_w_glu", "s5_b_glu", "cv_w_pw1", "cv_b_pw1", "cv_w_dw",
            "cv_b_dw", "cv_ln_g", "cv_ln_b", "cv_w_pw2", "cv_b_pw2", "ffn_w_gate", "ffn_w_up", "ffn_w_down"]
_BIG_COL = ["sb_w_qkv", "s5_w_glu", "cv_w_pw1", "ffn_w_gate", "ffn_w_up"]
_BIG_ROW = ["sb_w_o", "cv_w_pw2", "ffn_w_down"]
_BIG = _BIG_COL + _BIG_ROW
_SMALL_SHARDED = {"norm_g": 2, "cv_b_pw1": 1, "cv_w_dw": 2, "cv_b_dw": 1, "cv_ln_g": 1, "cv_ln_b": 1, "cv_b_pw2": 1}
_REPLICATED = ["b_mod", "s5_lam_re", "s5_lam_im", "s5_log_dt", "s5_b_re", "s5_b_im", "s5_c_re", "s5_c_im", "s5_d",
               "s5_b_glu"]


def _full_from_shards(stacked, name):
    if name in _BIG_COL or (name in _SMALL_SHARDED):
        axis = stacked.ndim - 1
    else:
        axis = stacked.ndim - 2
    moved = jnp.moveaxis(stacked, 0, axis - 1)
    shp = list(moved.shape)
    shp[axis - 1:axis + 1] = [shp[axis - 1] * shp[axis]]
    return moved.reshape(shp)


def _shards_from_full(full, name):
    axis = full.ndim - 1 if (name in _BIG_COL or name in _SMALL_SHARDED) else full.ndim - 2
    shp = list(full.shape)
    shp[axis:axis + 1] = [4, shp[axis] // 4]
    return jnp.moveaxis(full.reshape(shp), axis, 0)


def kernel(x, c, norm_g, w_mod, b_mod, sb_w_qkv, sb_w_o, s5_lam_re, s5_lam_im, s5_log_dt, s5_b_re, s5_b_im, s5_c_re, s5_c_im, s5_d, s5_w_glu, s5_b_glu, cv_w_pw1, cv_b_pw1, cv_w_dw, cv_b_dw, cv_ln_g, cv_ln_b, cv_w_pw2, cv_b_pw2, ffn_w_gate, ffn_w_up, ffn_w_down, loss_target, m_norm_g, m_w_mod, m_b_mod, m_sb_w_qkv, m_sb_w_o, m_s5_lam_re, m_s5_lam_im, m_s5_log_dt, m_s5_b_re, m_s5_b_im, m_s5_c_re, m_s5_c_im, m_s5_d, m_s5_w_glu, m_s5_b_glu, m_cv_w_pw1, m_cv_b_pw1, m_cv_w_dw, m_cv_b_dw, m_cv_ln_g, m_cv_ln_b, m_cv_w_pw2, m_cv_b_pw2, m_ffn_w_gate, m_ffn_w_up, m_ffn_w_down, v_norm_g, v_w_mod, v_b_mod, v_sb_w_qkv, v_sb_w_o, v_s5_lam_re, v_s5_lam_im, v_s5_log_dt, v_s5_b_re, v_s5_b_im, v_s5_c_re, v_s5_c_im, v_s5_d, v_s5_w_glu, v_s5_b_glu, v_cv_w_pw1, v_cv_b_pw1, v_cv_w_dw, v_cv_b_dw, v_cv_ln_g, v_cv_ln_b, v_cv_w_pw2, v_cv_b_pw2, v_ffn_w_gate, v_ffn_w_up, v_ffn_w_down):
    args = locals()
    wts = {n: args[n] for n in _WEIGHTS}
    mom = {n: args["m_" + n] for n in _WEIGHTS}
    var = {n: args["v_" + n] for n in _WEIGHTS}
    px, py, pc = lax.axis_index("x"), lax.axis_index("y"), lax.axis_index("c")
    chip = 2 * px + py
    dev = 2 * chip + pc
    s_dim, d_dim = x.shape[1], x.shape[2]
    depth = norm_g.shape[0]
    x2, tgt = x[0], loss_target[0]

    small_names = list(_SMALL_SHARDED)
    small_in = _pack([c] + [wts[n] for n in small_names], F32, SUBLANES)
    small_all = _all_gather8(small_in, "gather_small", True).reshape(8, -1)
    c_all = small_all[:, :d_dim]
    full = {}
    off = d_dim
    for n in small_names:
        size = math.prod(wts[n].shape)
        stacked = small_all[0::2, off:off + size].reshape((4,) + wts[n].shape)
        full[n] = _full_from_shards(stacked, n)
        off += size

    n_mod = w_mod.shape[2]
    b_mod_cols = lax.dynamic_slice_in_dim(b_mod, chip * n_mod, n_mod, axis=1)[:, None, :]
    mod_part = _mod_forward(c_all, w_mod, b_mod_cols, "mod_fwd")
    mod_all = _all_gather8(mod_part.reshape(depth * 8, n_mod), "gather_mod", True)
    mod_all = mod_all.reshape(4, 2, depth, 8, n_mod)[:, 0]
    mod_mine = lax.dynamic_index_in_dim(mod_all, dev, axis=2, keepdims=False)
    mod = mod_mine.transpose(1, 0, 2).reshape(depth, 4 * n_mod)

    big_in = _pack([wts[n] for n in _BIG], BF16, 4 * SUBLANES)
    big_all = _all_gather8(big_in, "gather_big", False, half_of_two=True).reshape(4, -1)
    off = 0
    for n in _BIG:
        size = math.prod(wts[n].shape)
        full[n] = _full_from_shards(big_all[:, off:off + size].reshape((4,) + wts[n].shape), n)
        off += size
    full["ffn_w_gu"] = jnp.concatenate([full.pop("ffn_w_gate"), full.pop("ffn_w_up")], axis=2)
    for n in _REPLICATED:
        full[n] = wts[n]

    n_s5 = s5_lam_re.shape[0]
    gp = s5_lam_re.shape[1] * s5_lam_re.shape[2]
    col = lambda a: a.reshape(gp, 1)
    s5_raw_in = []
    full["s5"] = []
    for j in range(n_s5):
        ldt = jnp.broadcast_to(s5_log_dt[j][:, None], s5_lam_re[j].shape)
        raw = (col(s5_lam_re[j]), col(s5_lam_im[j]), col(ldt), s5_b_re[j].reshape(gp, S5_GROUP), s5_b_im[j].reshape(gp, S5_GROUP))
        ar, ai, bbr, bbi = _s5_prep(*raw, "s5_prep")
        s5_raw_in.append(raw)
        shape3 = s5_b_re[j].shape
        full["s5"].append(dict(
            ar=ar.reshape(1, gp), ai=ai.reshape(1, gp),
            wr=_block_diag(bbr.reshape(shape3), False).astype(BF16), wi=_block_diag(bbi.reshape(shape3), False).astype(BF16),
            cr=_block_diag(s5_c_re[j], True).astype(BF16), ci=_block_diag(s5_c_im[j], True).astype(BF16)))

    loss_cols, grad_x, dmod, grads = _local_step(x2, mod, full, tgt)
    loss = lax.psum(0.5 * jnp.sum(loss_cols) / d_dim, ("x", "y", "c"))

    small_grads = {}
    for name in ("s5_lam_re", "s5_lam_im", "s5_log_dt", "s5_b_re", "s5_b_im", "s5_c_re", "s5_c_im"):
        small_grads[name] = []
    for j in range(n_s5):
        dwr, dwi, dcr, dci, dar, dai = grads["s5_raw"][j]
        shape3 = s5_b_re[j].shape
        cots = (jnp.sum(dar, axis=0).reshape(gp, 1), jnp.sum(dai, axis=0).reshape(gp, 1),
                _block_diag_extract(dwr, False).reshape(gp, S5_GROUP), _block_diag_extract(dwi, False).reshape(gp, S5_GROUP))
        dlr, dli, dldt, dbr, dbi = _s5_prep_bwd(*s5_raw_in[j], cots, "s5_prep_bwd")
        small_grads["s5_lam_re"].append(dlr.reshape(s5_lam_re[j].shape))
        small_grads["s5_lam_im"].append(dli.reshape(s5_lam_re[j].shape))
        small_grads["s5_log_dt"].append(jnp.sum(dldt.reshape(s5_lam_re[j].shape), axis=1))
        small_grads["s5_b_re"].append(dbr.reshape(shape3))
        small_grads["s5_b_im"].append(dbi.reshape(shape3))
        small_grads["s5_c_re"].append(_block_diag_extract(dcr, True))
        small_grads["s5_c_im"].append(_block_diag_extract(dci, True))
    small_grads = {n: jnp.stack(v) for n, v in small_grads.items()}
    stack = lambda name, count: jnp.stack([grads[name][j] for j in range(count)])
    small_grads["s5_d"] = stack("s5_d", n_s5)[:, 0]
    small_grads["s5_b_glu"] = stack("s5_b_glu", n_s5)[:, 0]
    n_cv = cv_w_pw1.shape[0]
    for name in ("cv_b_pw1", "cv_b_dw", "cv_ln_g", "cv_ln_b", "cv_b_pw2"):
        small_grads[name] = stack(name, n_cv)[:, 0]
    small_grads["cv_w_dw"] = stack("cv_w_dw", n_cv)
    small_grads["norm_g"] = grads["norm_g"]

    small_order = ["norm_g", "s5_lam_re", "s5_lam_im", "s5_log_dt", "s5_b_re", "s5_b_im", "s5_c_re", "s5_c_im", "s5_d",
                   "s5_b_glu", "cv_b_pw1", "cv_w_dw", "cv_b_dw", "cv_ln_g", "cv_ln_b", "cv_b_pw2"]
    small_shapes = [small_grads[n].shape for n in small_order]
    sg_in = _pack([dmod] + [small_grads[n] for n in small_order], F32, SUBLANES)
    sg_all = _all_gather8(sg_in, "gather_small_grads", True)
    sg_sum = _sum8(sg_all, "sum_small_grads").reshape(-1)
    dmod_all = sg_all.reshape(8, -1)[:, :dmod.size].reshape(8, depth, 4, n_mod)
    g_full = dict(zip(small_order, _unpack(sg_sum[dmod.size:], small_shapes), strict=True))
    g_full["b_mod"] = sg_sum[:dmod.size].reshape(dmod.shape)

    gu = stack("ffn_w_gu", depth)
    f_dim = gu.shape[2] // 2
    big_grads = {"sb_w_qkv": stack("sb_w_qkv", sb_w_qkv.shape[0]), "sb_w_o": stack("sb_w_o", sb_w_o.shape[0]),
                 "s5_w_glu": stack("s5_w_glu", n_s5), "cv_w_pw1": stack("cv_w_pw1", n_cv), "cv_w_pw2": stack("cv_w_pw2", n_cv),
                 "ffn_w_gate": gu[:, :, :f_dim], "ffn_w_up": gu[:, :, f_dim:], "ffn_w_down": stack("ffn_w_down", depth)}
    per_chip = jnp.concatenate([_shards_from_full(big_grads[n], n).reshape(4, -1) for n in _BIG], axis=1)
    n_big = per_chip.shape[1]
    quantum = 2 * PACK_LANES * SUBLANES
    per_chip = jnp.pad(per_chip, ((0, 0), (0, (-n_big) % quantum)))
    g8 = per_chip.reshape(8, -1, PACK_LANES)
    core_idx, chip_idx = pc.reshape(1).astype(I32), chip.reshape(1).astype(I32)
    from_sibling = _sibling_halves(g8, "reduce_sibling")
    chip_sums = _add_own_half(g8, from_sibling, core_idx, "reduce_add_sibling")
    from_chips = _chip_exchange(chip_sums, "reduce_chips")
    my_half = _add_chips(chip_sums, from_chips, chip_idx, "reduce_add_chips")
    shard_flat = _sibling_share(my_half, "reduce_share").reshape(-1)
    g_shard = dict(zip(_BIG, _unpack(shard_flat, [wts[n].shape for n in _BIG]), strict=True))

    out_g, out_d, out_m, out_v = {}, {}, {}, {}
    for n in _BIG:
        shp = wts[n].shape
        two = lambda a: a.reshape(-1, shp[-1])
        out_g[n] = g_shard[n]
        d, nm, nv = _adamw(two(wts[n]), two(g_shard[n]), two(mom[n]), two(var[n]), "adamw")
        out_d[n], out_m[n], out_v[n] = d.reshape(shp), nm.reshape(shp), nv.reshape(shp)
    sc_t = jax.nn.silu(c_all).T
    dmod_cols = lax.dynamic_index_in_dim(dmod_all, chip, axis=2, keepdims=False).transpose(1, 0, 2)
    out_g["w_mod"], out_d["w_mod"], out_m["w_mod"], out_v["w_mod"] = _mod_grad_adamw(
        sc_t, dmod_cols, w_mod, m_w_mod, v_w_mod, "w_mod_adamw")
    small_all_names = [n for n in _WEIGHTS if n not in _BIG and n != "w_mod"]
    local_g = []
    for n in small_all_names:
        g = g_full[n]
        if n in _SMALL_SHARDED:
            width = wts[n].shape[-1]
            g = lax.dynamic_slice_in_dim(g, chip * width, width, axis=g.ndim - 1)
        local_g.append(g)
        out_g[n] = g
    pk = lambda arrs: _pack(arrs, F32, SUBLANES)
    d, nm, nv = _adamw(pk([wts[n] for n in small_all_names]), pk(local_g), pk([mom[n] for n in small_all_names]),
                       pk([var[n] for n in small_all_names]), "adamw_small")
    shapes = [wts[n].shape for n in small_all_names]
    for n, a, b, cc in zip(small_all_names, _unpack(d.reshape(-1), shapes), _unpack(nm.reshape(-1), shapes),
                           _unpack(nv.reshape(-1), shapes), strict=True):
        out_d[n], out_m[n], out_v[n] = a, b, cc
    return (loss, grad_x[None], *[out_g[n] for n in _WEIGHTS], *[out_d[n] for n in _WEIGHTS],
            *[out_m[n] for n in _WEIGHTS], *[out_v[n] for n in _WEIGHTS])
```

```python
import functools
import math

import jax
import jax.numpy as jnp
from jax import lax
from jax.experimental import pallas as pl
from jax.experimental.pallas import tpu as pltpu

F32 = jnp.float32
BF16 = jnp.bfloat16
I32 = jnp.int32
SDS = jax.ShapeDtypeStruct
MESH = pl.DeviceIdType.MESH

HEAD_DIM = 64
S5_GROUP = 16
S5_STATE = 64
S5_BLOCK_GROUPS = 8
S5_BLOCK_CH = S5_GROUP * S5_BLOCK_GROUPS
S5_BLOCK_ST = S5_STATE * S5_BLOCK_GROUPS
CONV_WIDTH = 31
CONV_HALO = 32
EPS = 1e-6
ADAM_LR = 0.001
ADAM_B1 = 0.9
ADAM_B2 = 0.999
ADAM_EPS = 1e-08
ADAM_WD = 0.01
ADAM_STEP = 10

LANES = 128
SUBLANES = 8
VMEM_LIMIT = 56 * 2 ** 20
PACK_LANES = 512
ATT_BLOCK = 128
EXP_FLOOR = -104.0

_COUNTER = [0]


def _name(base):
    _COUNTER[0] += 1
    return f"{base}_{_COUNTER[0]}"


def _tile(dim, target, mult=LANES):
    t = min(dim, target)
    t -= t % mult
    while t > mult and dim % t:
        t -= mult
    return t if t > 0 and dim % t == 0 else dim


def _params(sem=None):
    return pltpu.CompilerParams(dimension_semantics=sem, vmem_limit_bytes=VMEM_LIMIT)


_DIMS = {"nn": (((1,), (0,)), ((), ())), "nt": (((1,), (1,)), ((), ())), "tn": (((0,), (0,)), ((), ()))}


def _matmul(a, b, form, out_dtype, base, tm=1024, tn=1024, tk=1024):
    if form == "tn":
        k_dim, m_dim = a.shape
    else:
        m_dim, k_dim = a.shape
    n_dim = b.shape[0] if form == "nt" else b.shape[1]
    tm, tn, tk = _tile(m_dim, tm), _tile(n_dim, tn), _tile(k_dim, tk)
    nk = k_dim // tk
    dims = _DIMS[form]

    def body(a_ref, b_ref, o_ref, *acc):
        part = lax.dot_general(a_ref[...].astype(BF16), b_ref[...].astype(BF16), dims, preferred_element_type=F32)
        if nk == 1:
            o_ref[...] = part.astype(o_ref.dtype)
            return
        acc_ref, = acc
        k = pl.program_id(2)

        @pl.when(k == 0)
        def _():
            acc_ref[...] = part

        @pl.when(jnp.logical_and(k > 0, k < nk - 1))
        def _():
            acc_ref[...] += part

        @pl.when(k == nk - 1)
        def _():
            o_ref[...] = (acc_ref[...] + part).astype(o_ref.dtype)

    a_spec = (pl.BlockSpec((tk, tm), lambda i, j, k: (k, i)) if form == "tn"
              else pl.BlockSpec((tm, tk), lambda i, j, k: (i, k)))
    b_spec = (pl.BlockSpec((tn, tk), lambda i, j, k: (j, k)) if form == "nt"
              else pl.BlockSpec((tk, tn), lambda i, j, k: (k, j)))
    return pl.pallas_call(
        body, name=_name(base), grid=(m_dim // tm, n_dim // tn, nk),
        in_specs=[a_spec, b_spec], out_specs=pl.BlockSpec((tm, tn), lambda i, j, k: (i, j)),
        out_shape=SDS((m_dim, n_dim), out_dtype), scratch_shapes=[pltpu.VMEM((tm, tn), F32)] if nk > 1 else [],
        compiler_params=_params(("parallel", "parallel", "arbitrary")),
    )(a, b)


def _rowwise(fn, rows, vecs, out_rows, out_vecs, base, tr=256):
    s_dim = rows[0].shape[0]
    tr = _tile(s_dim, tr, SUBLANES)
    nr, nv, nor = len(rows), len(vecs), len(out_rows)

    def body(*refs):
        r_in, v_in = refs[:nr], refs[nr:nr + nv]
        r_out, v_out = refs[nr + nv:nr + nv + nor], refs[nr + nv + nor:]
        ro, vo = fn(*[r[...] for r in r_in], *[v[...] for v in v_in])
        for ref, val in zip(r_out, ro, strict=True):
            ref[...] = val.astype(ref.dtype)
        if v_out:
            i = pl.program_id(0)

            @pl.when(i == 0)
            def _():
                for ref, val in zip(v_out, vo, strict=True):
                    ref[...] = val.astype(ref.dtype)

            @pl.when(i > 0)
            def _():
                for ref, val in zip(v_out, vo, strict=True):
                    ref[...] += val.astype(ref.dtype)

    in_specs = [pl.BlockSpec((tr, r.shape[1]), lambda i: (i, 0)) for r in rows]
    in_specs += [pl.BlockSpec(v.shape, lambda i: (0, 0)) for v in vecs]
    out_specs = [pl.BlockSpec((tr, o.shape[1]), lambda i: (i, 0)) for o in out_rows]
    out_specs += [pl.BlockSpec(o.shape, lambda i: (0, 0)) for o in out_vecs]
    outs = pl.pallas_call(
        body, name=_name(base), grid=(s_dim // tr,), in_specs=in_specs, out_specs=out_specs,
        out_shape=list(out_rows) + list(out_vecs),
        compiler_params=_params(("arbitrary",)),
    )(*rows, *vecs)
    return outs


def _f(x):
    return x.astype(F32)


def _rms(x, g):
    return x * lax.rsqrt(jnp.mean(x * x, axis=-1, keepdims=True) + EPS) * g


def _pre(h, g, sh, sc):
    return _rms(h, g) * (1 + sc) + sh


def _post(h, m, gate, g):
    return h + gate * _rms(m, g)


def _halves(t):
    d = t.shape[1] // 2
    return t[:, :d], t[:, d:]


def _glu2(ta, tb, ba, bb):
    return (ta + ba) * jax.nn.sigmoid(tb + bb)


def _glu(t, b):
    return _glu2(*_halves(t), *_halves(b))


def _glu_bwd(t, dout, b):
    _, vjp = jax.vjp(_glu2, *_halves(t), *_halves(b))
    dta, dtb, dba, dbb = vjp(dout)
    return jnp.concatenate([dta, dtb], axis=1), jnp.concatenate([dba, dbb], axis=1)


def _swish2(g, u):
    return jax.nn.silu(g) * u


def _swish_gate(gu):
    return _swish2(*_halves(gu))


def _ln_silu(h, g, b):
    mu = jnp.mean(h, axis=-1, keepdims=True)
    var = jnp.mean(jnp.square(h - mu), axis=-1, keepdims=True)
    return jax.nn.silu((h - mu) * lax.rsqrt(var + EPS) * g + b)


def _dot(a, b, form="nn"):
    return lax.dot_general(a, b, _DIMS[form], preferred_element_type=F32)


def _split_dot(x, tri):
    hi = x.astype(BF16)
    lo = (x - hi.astype(F32)).astype(BF16)
    return _dot(hi, tri) + _dot(lo, tri)


def _sb_scores(qb, kb, scale, mask):
    z = _dot(qb, kb, "nt") * scale
    sp = jnp.maximum(z, 0.0) + jnp.log(1.0 + jnp.exp(-jnp.abs(z)))
    lk = -sp
    if mask is not None:
        lk = jnp.where(mask, lk, 0.0)
    return lk, z - sp


def _walk_left(i, state, step):
    def cond(ns):
        return jnp.logical_and(ns[0] < i, jnp.max(ns[1][0]) > EXP_FLOOR)

    def body(ns):
        return ns[0] + 1, step(i - 1 - ns[0], ns[1])

    return lax.while_loop(cond, body, (jnp.int32(0), state))[1]


def _sb_fwd(q, k, v, base):
    h_dim, s_dim, hd = q.shape
    bq = ATT_BLOCK
    scale = hd ** -0.5

    def body(q_ref, k_ref, v_ref, o_ref):
        i = pl.program_id(1)
        qb = q_ref[0]
        row = lax.broadcasted_iota(I32, (bq, bq), 0)
        col = lax.broadcasted_iota(I32, (bq, bq), 1)
        later = (row > col).astype(BF16)
        causal = col < row

        def tile(j, c, acc, mask):
            ks = pl.ds(pl.multiple_of(j * bq, bq), bq)
            kb, vb = k_ref[0, ks, :], v_ref[0, ks, :]
            lk, lb = _sb_scores(qb, kb, scale, mask)
            w = jnp.exp(lb + _split_dot(lk, later) + c)
            if mask is not None:
                w = jnp.where(mask, w, 0.0)
            acc = acc + _dot(w.astype(BF16), vb)
            return c + jnp.sum(lk, axis=1, keepdims=True), acc

        c, acc = tile(i, jnp.zeros((bq, 1), F32), jnp.zeros((bq, hd), F32), causal)
        c, acc = _walk_left(i, (c, acc), lambda j, ca: tile(j, ca[0], ca[1], None))
        o_ref[0] = acc

    blk = pl.BlockSpec((1, bq, hd), lambda h, i: (h, i, 0))
    full = pl.BlockSpec((1, s_dim, hd), lambda h, i: (h, 0, 0))
    return pl.pallas_call(
        body, name=_name(base), grid=(h_dim, s_dim // bq), in_specs=[blk, full, full], out_specs=blk,
        out_shape=SDS((h_dim, s_dim, hd), F32), compiler_params=_params(("parallel", "arbitrary")),
    )(q, k, v)


def _sb_bwd(q, k, v, do, base):
    h_dim, s_dim, hd = q.shape
    bq = ATT_BLOCK
    nq = s_dim // bq
    scale = hd ** -0.5

    def body(q_ref, k_ref, v_ref, do_ref, dq_ref, dk_ref, dv_ref):
        i = pl.program_id(1)
        qb, dob = q_ref[0], do_ref[0]
        row = lax.broadcasted_iota(I32, (bq, bq), 0)
        col = lax.broadcasted_iota(I32, (bq, bq), 1)
        later = (row > col).astype(BF16)
        from_here = (row >= col).astype(BF16)
        causal = col < row

        @pl.when(i == 0)
        def _():
            dk_ref[...] = jnp.zeros_like(dk_ref)
            dv_ref[...] = jnp.zeros_like(dv_ref)

        def weights(j, c, mask):
            ks = pl.ds(pl.multiple_of(j * bq, bq), bq)
            kb, vb = k_ref[0, ks, :], v_ref[0, ks, :]
            lk, lb = _sb_scores(qb, kb, scale, mask)
            w = jnp.exp(lb + _split_dot(lk, later) + c)
            if mask is not None:
                w = jnp.where(mask, w, 0.0)
            de = _dot(dob, vb, "nt") * w
            return ks, kb, lk, lb, w, de

        def total(j, c, tot, mask):
            _, _, lk, _, _, de = weights(j, c, mask)
            return c + jnp.sum(lk, axis=1, keepdims=True), tot + jnp.sum(de, axis=1, keepdims=True)

        zero = jnp.zeros((bq, 1), F32)
        c, tot = total(i, zero, zero, causal)
        c, tot = _walk_left(i, (c, tot), lambda j, ct: total(j, ct[0], ct[1], None))

        def grads(j, c, r, dq, mask):
            ks, kb, lk, lb, w, de = weights(j, c, mask)
            before = tot - (r + _split_dot(de, from_here))
            dz = de * jnp.exp(lk) - jnp.exp(lb) * before
            if mask is not None:
                dz = jnp.where(mask, dz, 0.0)
            dzb = (dz * scale).astype(BF16)
            dk_ref[0, ks, :] += _dot(dzb, qb, "tn")
            dv_ref[0, ks, :] += _dot(w.astype(BF16), dob, "tn")
            dq = dq + _dot(dzb, kb)
            return c + jnp.sum(lk, axis=1, keepdims=True), r + jnp.sum(de, axis=1, keepdims=True), dq

        c, r, dq = grads(i, zero, zero, jnp.zeros((bq, hd), F32), causal)
        c, r, dq = _walk_left(i, (c, r, dq), lambda j, s: grads(j, s[0], s[1], s[2], None))
        dq_ref[0] = dq

    blk = pl.BlockSpec((1, bq, hd), lambda h, i: (h, i, 0))
    full = pl.BlockSpec((1, s_dim, hd), lambda h, i: (h, 0, 0))
    out = SDS((h_dim, s_dim, hd), F32)
    return pl.pallas_call(
        body, name=_name(base), grid=(h_dim, nq), in_specs=[blk, full, full, blk], out_specs=[blk, full, full],
        out_shape=[out, out, out], compiler_params=_params(("parallel", "arbitrary")),
    )(q, k, v, do)


def _cmul(ar, ai, br, bi):
    return ar * br - ai * bi, ar * bi + ai * br


def _s5_prep_fn(lr, li, ldt, br, bi):
    dt = jnp.exp(ldt)
    mag = jnp.exp(lr * dt)
    ar, ai = mag * jnp.cos(li * dt), mag * jnp.sin(li * dt)
    den = lr * lr + li * li
    er = ((ar - 1) * lr + ai * li) / den
    ei = (ai * lr - (ar - 1) * li) / den
    return ar, ai, er * br - ei * bi, er * bi + ei * br


def _s5_prep(lr, li, ldt, br, bi, base):
    col, mat = SDS(lr.shape, F32), SDS(br.shape, F32)
    return _rowwise(lambda *a: (_s5_prep_fn(*a), ()), [lr, li, ldt, br, bi], [], [col, col, mat, mat], [], base, tr=512)


def _s5_prep_bwd(lr, li, ldt, br, bi, cots, base):
    col, mat = SDS(lr.shape, F32), SDS(br.shape, F32)

    def bwd(lr, li, ldt, br, bi, c0, c1, c2, c3):
        _, vjp = jax.vjp(_s5_prep_fn, lr, li, ldt, br, bi)
        return vjp((c0, c1, c2, c3)), ()

    return _rowwise(bwd, [lr, li, ldt, br, bi, *cots], [], [col, col, col, mat, mat], [], base, tr=512)


def _scan_tables(ar_ref, ai_ref, sl, conj):
    shape = (SUBLANES, S5_BLOCK_ST)
    a_r = jnp.broadcast_to(ar_ref[:, sl], shape)
    a_i = jnp.broadcast_to(ai_ref[:, sl], shape)
    if conj:
        a_i = -a_i
    a2 = _cmul(a_r, a_i, a_r, a_i)
    a4 = _cmul(*a2, *a2)
    row = lax.broadcasted_iota(I32, shape, 0)
    p_r, p_i, t_r, t_i = a_r, a_i, a_r, a_i
    for k in range(1, SUBLANES):
        p_r, p_i = _cmul(p_r, p_i, a_r, a_i)
        sel = (row == (SUBLANES - 1 - k)) if conj else (row == k)
        t_r, t_i = jnp.where(sel, p_r, t_r), jnp.where(sel, p_i, t_i)
    if conj:
        t_r, t_i = jnp.where(row == SUBLANES - 1, a_r, t_r), jnp.where(row == SUBLANES - 1, a_i, t_i)
    return row, (a_r, a_i), a2, a4, (t_r, t_i)


def _scan_group(xr, xi, row, a1, a2, a4, table, cr, ci, reverse):
    for s, (mr, mi) in ((1, a1), (2, a2), (4, a4)):
        if reverse:
            keep = row < SUBLANES - s
            sr, si = pltpu.roll(xr, SUBLANES - s, 0), pltpu.roll(xi, SUBLANES - s, 0)
        else:
            keep = row >= s
            sr, si = pltpu.roll(xr, s, 0), pltpu.roll(xi, s, 0)
        sr, si = jnp.where(keep, sr, 0.0), jnp.where(keep, si, 0.0)
        xr, xi = xr + mr * sr - mi * si, xi + mr * si + mi * sr
    tr, ti = table
    return xr + tr * cr - ti * ci, xi + tr * ci + ti * cr


def _s5_fwd(u, ar, ai, wr, wi, cr_w, ci_w, dsk, base, chunk=128):
    s_dim, d_dim = u.shape
    nb = d_dim // S5_BLOCK_CH
    n_st = nb * S5_BLOCK_ST
    chunk = _tile(s_dim, chunk, SUBLANES)
    nch = s_dim // chunk

    def body(u_ref, ar_ref, ai_ref, wr_ref, wi_ref, cr_ref, ci_ref, d_ref, y_ref, gy_ref, sr_ref, si_ref,
             xr_s, xi_s, car_r, car_i):
        @pl.when(pl.program_id(0) == 0)
        def _():
            car_r[...] = jnp.zeros_like(car_r)
            car_i[...] = jnp.zeros_like(car_i)

        sr_ref[...] = car_r[...]
        si_ref[...] = car_i[...]
        for b in range(nb):
            ub = u_ref[:, b * S5_BLOCK_CH:(b + 1) * S5_BLOCK_CH].astype(BF16)
            sl = slice(b * S5_BLOCK_ST, (b + 1) * S5_BLOCK_ST)
            xr_s[:, sl] = _dot(ub, wr_ref[b])
            xi_s[:, sl] = _dot(ub, wi_ref[b])
        for b in range(nb):
            sl = slice(b * S5_BLOCK_ST, (b + 1) * S5_BLOCK_ST)
            row, a1, a2, a4, table = _scan_tables(ar_ref, ai_ref, sl, False)

            def group(g, carry, sl=sl, row=row, a1=a1, a2=a2, a4=a4, table=table):
                rows = pl.ds(pl.multiple_of(g * SUBLANES, SUBLANES), SUBLANES)
                xr, xi = _scan_group(xr_s[rows, sl], xi_s[rows, sl], row, a1, a2, a4, table, carry[0], carry[1], False)
                xr_s[rows, sl] = xr
                xi_s[rows, sl] = xi
                last = (SUBLANES, S5_BLOCK_ST)
                return (jnp.broadcast_to(xr[SUBLANES - 1:, :], last), jnp.broadcast_to(xi[SUBLANES - 1:, :], last))

            c_r, c_i = lax.fori_loop(0, chunk // SUBLANES, group, (car_r[:, sl], car_i[:, sl]))
            car_r[:, sl] = c_r
            car_i[:, sl] = c_i
        for b in range(nb):
            sl = slice(b * S5_BLOCK_ST, (b + 1) * S5_BLOCK_ST)
            ch = slice(b * S5_BLOCK_CH, (b + 1) * S5_BLOCK_CH)
            y = _dot(xr_s[:, sl].astype(BF16), cr_ref[b]) - _dot(xi_s[:, sl].astype(BF16), ci_ref[b])
            y = y + d_ref[:, ch] * u_ref[:, ch]
            y_ref[:, ch] = y
            gy_ref[:, ch] = jax.nn.gelu(y).astype(BF16)

    rows = pl.BlockSpec((chunk, d_dim), lambda i: (i, 0))
    vec = pl.BlockSpec((1, n_st), lambda i: (0, 0))
    w_in = pl.BlockSpec((nb, S5_BLOCK_CH, S5_BLOCK_ST), lambda i: (0, 0, 0))
    w_out = pl.BlockSpec((nb, S5_BLOCK_ST, S5_BLOCK_CH), lambda i: (0, 0, 0))
    st = pl.BlockSpec((SUBLANES, n_st), lambda i: (i, 0))
    return pl.pallas_call(
        body, name=_name(base), grid=(nch,),
        in_specs=[rows, vec, vec, w_in, w_in, w_out, w_out, pl.BlockSpec((1, d_dim), lambda i: (0, 0))],
        out_specs=[rows, rows, st, st],
        out_shape=[SDS((s_dim, d_dim), F32), SDS((s_dim, d_dim), BF16),
                   SDS((nch * SUBLANES, n_st), F32), SDS((nch * SUBLANES, n_st), F32)],
        scratch_shapes=[pltpu.VMEM((chunk, n_st), F32), pltpu.VMEM((chunk, n_st), F32),
                        pltpu.VMEM((SUBLANES, n_st), F32), pltpu.VMEM((SUBLANES, n_st), F32)],
        compiler_params=_params(("arbitrary",)),
    )(u, ar, ai, wr, wi, cr_w, ci_w, dsk)


def _s5_bwd(u, dy, st_r, st_i, ar, ai, wr, wi, cr_w, ci_w, dsk, base, chunk):
    s_dim, d_dim = u.shape
    nb = d_dim // S5_BLOCK_CH
    n_st = nb * S5_BLOCK_ST
    nch = s_dim // chunk
    ng = chunk // SUBLANES

    def body(u_ref, dy_ref, sr_ref, si_ref, ar_ref, ai_ref, wr_ref, wi_ref, cr_ref, ci_ref, d_ref,
             du_ref, dwr_ref, dwi_ref, dcr_ref, dci_ref, dar_ref, dai_ref, dd_ref,
             xr_s, xi_s, lr_s, li_s, car_r, car_i):
        first = pl.program_id(0) == 0

        @pl.when(first)
        def _():
            car_r[...] = jnp.zeros_like(car_r)
            car_i[...] = jnp.zeros_like(car_i)
            for ref in (dwr_ref, dwi_ref, dcr_ref, dci_ref, dar_ref, dai_ref, dd_ref):
                ref[...] = jnp.zeros_like(ref)

        xr_s[pl.ds(0, SUBLANES), :] = sr_ref[...]
        xi_s[pl.ds(0, SUBLANES), :] = si_ref[...]
        body_rows = pl.ds(SUBLANES, chunk)
        dyb = dy_ref[...]
        dd_ref[...] += jnp.sum(dyb * u_ref[...], axis=0, keepdims=True)
        for b in range(nb):
            ch = slice(b * S5_BLOCK_CH, (b + 1) * S5_BLOCK_CH)
            sl = slice(b * S5_BLOCK_ST, (b + 1) * S5_BLOCK_ST)
            ub = u_ref[:, ch].astype(BF16)
            dyc = dyb[:, ch].astype(BF16)
            xr_s[body_rows, sl] = _dot(ub, wr_ref[b])
            xi_s[body_rows, sl] = _dot(ub, wi_ref[b])
            lr_s[:, sl] = _dot(dyc, cr_ref[b], "nt")
            li_s[:, sl] = -_dot(dyc, ci_ref[b], "nt")
        for b in range(nb):
            sl = slice(b * S5_BLOCK_ST, (b + 1) * S5_BLOCK_ST)
            row, a1, a2, a4, table = _scan_tables(ar_ref, ai_ref, sl, False)

            def fwd_group(g, carry, sl=sl, row=row, a1=a1, a2=a2, a4=a4, table=table):
                rows = pl.ds(pl.multiple_of((g + 1) * SUBLANES, SUBLANES), SUBLANES)
                xr, xi = _scan_group(xr_s[rows, sl], xi_s[rows, sl], row, a1, a2, a4, table, carry[0], carry[1], False)
                xr_s[rows, sl] = xr
                xi_s[rows, sl] = xi
                last = (SUBLANES, S5_BLOCK_ST)
                return (jnp.broadcast_to(xr[SUBLANES - 1:, :], last), jnp.broadcast_to(xi[SUBLANES - 1:, :], last))

            lax.fori_loop(0, ng, fwd_group, (sr_ref[:, sl], si_ref[:, sl]))
            row, a1, a2, a4, table = _scan_tables(ar_ref, ai_ref, sl, True)

            def bwd_group(n, carry, sl=sl, row=row, a1=a1, a2=a2, a4=a4, table=table):
                g = ng - 1 - n
                c_r, c_i, acc_r, acc_i = carry
                rows = pl.ds(pl.multiple_of(g * SUBLANES, SUBLANES), SUBLANES)
                lr, li = _scan_group(lr_s[rows, sl], li_s[rows, sl], row, a1, a2, a4, table, c_r, c_i, True)
                lr_s[rows, sl] = lr
                li_s[rows, sl] = li
                prev = pl.ds(pl.multiple_of(g * SUBLANES, SUBLANES), SUBLANES)
                here = pl.ds(pl.multiple_of((g + 1) * SUBLANES, SUBLANES), SUBLANES)
                px_r = jnp.where(row >= 1, pltpu.roll(xr_s[here, sl], 1, 0), pltpu.roll(xr_s[prev, sl], 1, 0))
                px_i = jnp.where(row >= 1, pltpu.roll(xi_s[here, sl], 1, 0), pltpu.roll(xi_s[prev, sl], 1, 0))
                acc_r = acc_r + lr * px_r + li * px_i
                acc_i = acc_i + li * px_r - lr * px_i
                first_row = (SUBLANES, S5_BLOCK_ST)
                return (jnp.broadcast_to(lr[:1, :], first_row), jnp.broadcast_to(li[:1, :], first_row), acc_r, acc_i)

            zero = jnp.zeros((SUBLANES, S5_BLOCK_ST), F32)
            c_r, c_i, acc_r, acc_i = lax.fori_loop(0, ng, bwd_group, (car_r[:, sl], car_i[:, sl], zero, zero))
            car_r[:, sl] = c_r
            car_i[:, sl] = c_i
            dar_ref[:, sl] += acc_r
            dai_ref[:, sl] += acc_i
        for b in range(nb):
            ch = slice(b * S5_BLOCK_CH, (b + 1) * S5_BLOCK_CH)
            sl = slice(b * S5_BLOCK_ST, (b + 1) * S5_BLOCK_ST)
            ub = u_ref[:, ch].astype(BF16)
            dyc = dyb[:, ch].astype(BF16)
            lrb, lib = lr_s[:, sl].astype(BF16), li_s[:, sl].astype(BF16)
            du_ref[:, ch] = (_dot(lrb, wr_ref[b], "nt") + _dot(lib, wi_ref[b], "nt") + d_ref[:, ch] * dyb[:, ch])
            dwr_ref[b] += _dot(ub, lrb, "tn")
            dwi_ref[b] += _dot(ub, lib, "tn")
            dcr_ref[b] += _dot(xr_s[body_rows, sl].astype(BF16), dyc, "tn")
            dci_ref[b] -= _dot(xi_s[body_rows, sl].astype(BF16), dyc, "tn")

    rev = lambda i: (nch - 1 - i, 0)
    rows = pl.BlockSpec((chunk, d_dim), rev)
    st = pl.BlockSpec((SUBLANES, n_st), rev)
    vec = pl.BlockSpec((1, n_st), lambda i: (0, 0))
    w_in = pl.BlockSpec((nb, S5_BLOCK_CH, S5_BLOCK_ST), lambda i: (0, 0, 0))
    w_out = pl.BlockSpec((nb, S5_BLOCK_ST, S5_BLOCK_CH), lambda i: (0, 0, 0))
    acc8 = pl.BlockSpec((SUBLANES, n_st), lambda i: (0, 0))
    dvec = pl.BlockSpec((1, d_dim), lambda i: (0, 0))
    return pl.pallas_call(
        body, name=_name(base), grid=(nch,),
        in_specs=[rows, rows, st, st, vec, vec, w_in, w_in, w_out, w_out, dvec],
        out_specs=[rows, w_in, w_in, w_out, w_out, acc8, acc8, dvec],
        out_shape=[SDS((s_dim, d_dim), F32),
                   SDS((nb, S5_BLOCK_CH, S5_BLOCK_ST), F32), SDS((nb, S5_BLOCK_CH, S5_BLOCK_ST), F32),
                   SDS((nb, S5_BLOCK_ST, S5_BLOCK_CH), F32), SDS((nb, S5_BLOCK_ST, S5_BLOCK_CH), F32),
                   SDS((SUBLANES, n_st), F32), SDS((SUBLANES, n_st), F32), SDS((1, d_dim), F32)],
        scratch_shapes=[pltpu.VMEM((chunk + SUBLANES, n_st), F32), pltpu.VMEM((chunk + SUBLANES, n_st), F32),
                        pltpu.VMEM((chunk, n_st), F32), pltpu.VMEM((chunk, n_st), F32),
                        pltpu.VMEM((SUBLANES, n_st), F32), pltpu.VMEM((SUBLANES, n_st), F32)],
        compiler_params=_params(("arbitrary",)),
    )(u, dy, st_r, st_i, ar, ai, wr, wi, cr_w, ci_w, dsk)


def _block_diag(w, transpose):
    g = w.shape[0]
    nb = g // S5_BLOCK_GROUPS
    eye = jnp.eye(S5_BLOCK_GROUPS, dtype=w.dtype)
    if transpose:
        w = w.reshape(nb, S5_BLOCK_GROUPS, S5_GROUP, S5_STATE)
        return jnp.einsum("bgcp,gh->bgphc", w, eye).reshape(nb, S5_BLOCK_ST, S5_BLOCK_CH)
    w = w.reshape(nb, S5_BLOCK_GROUPS, S5_STATE, S5_GROUP)
    return jnp.einsum("bgpc,gh->bgchp", w, eye).reshape(nb, S5_BLOCK_CH, S5_BLOCK_ST)


def _block_diag_extract(w, transpose):
    nb = w.shape[0]
    eye = jnp.eye(S5_BLOCK_GROUPS, dtype=w.dtype)
    if transpose:
        w = w.reshape(nb, S5_BLOCK_GROUPS, S5_STATE, S5_BLOCK_GROUPS, S5_GROUP)
        return jnp.einsum("bgphc,gh->bgcp", w, eye).reshape(nb * S5_BLOCK_GROUPS, S5_GROUP, S5_STATE)
    w = w.reshape(nb, S5_BLOCK_GROUPS, S5_GROUP, S5_BLOCK_GROUPS, S5_STATE)
    return jnp.einsum("bgchp,gh->bgpc", w, eye).reshape(nb * S5_BLOCK_GROUPS, S5_STATE, S5_GROUP)


def _dwconv_fwd(h, w, b, base, tr=512):
    s_dim, d_dim = h.shape
    tr = _tile(s_dim, tr, CONV_HALO)
    per = tr // CONV_HALO

    def body(h_ref, halo_ref, w_ref, b_ref, o_ref, buf):
        i = pl.program_id(0)
        buf[pl.ds(0, CONV_HALO), :] = jnp.where(i > 0, halo_ref[...], 0.0)
        buf[pl.ds(CONV_HALO, tr), :] = h_ref[...]
        acc = jnp.broadcast_to(b_ref[...], (tr, d_dim))
        for k in range(CONV_WIDTH):
            acc = acc + w_ref[pl.ds(k, 1), :] * buf[pl.ds(CONV_HALO - (CONV_WIDTH - 1) + k, tr), :]
        o_ref[...] = acc

    return pl.pallas_call(
        body, name=_name(base), grid=(s_dim // tr,),
        in_specs=[pl.BlockSpec((tr, d_dim), lambda i: (i, 0)),
                  pl.BlockSpec((CONV_HALO, d_dim), lambda i: (jnp.maximum(i * per - 1, 0), 0)),
                  pl.BlockSpec((CONV_WIDTH, d_dim), lambda i: (0, 0)), pl.BlockSpec((1, d_dim), lambda i: (0, 0))],
        out_specs=pl.BlockSpec((tr, d_dim), lambda i: (i, 0)), out_shape=SDS((s_dim, d_dim), F32),
        scratch_shapes=[pltpu.VMEM((tr + CONV_HALO, d_dim), F32)], compiler_params=_params(("arbitrary",)),
    )(h, h, w, b)


def _dwconv_bwd(h, dout, w, base, tr=512):
    s_dim, d_dim = h.shape
    tr = _tile(s_dim, tr, CONV_HALO)
    per = tr // CONV_HALO
    n = s_dim // tr
    last_halo = s_dim // CONV_HALO - 1

    def body(h_ref, hhalo_ref, d_ref, dhalo_ref, w_ref, dh_ref, dw_ref, db_ref, hbuf, dbuf):
        i = pl.program_id(0)
        hbuf[pl.ds(0, CONV_HALO), :] = jnp.where(i > 0, hhalo_ref[...], 0.0)
        hbuf[pl.ds(CONV_HALO, tr), :] = h_ref[...]
        dbuf[pl.ds(0, tr), :] = d_ref[...]
        dbuf[pl.ds(tr, CONV_HALO), :] = jnp.where(i < n - 1, dhalo_ref[...], 0.0)
        dout_t = d_ref[...]
        acc = jnp.zeros((tr, d_dim), F32)
        dws = []
        for k in range(CONV_WIDTH):
            acc = acc + w_ref[pl.ds(k, 1), :] * dbuf[pl.ds(CONV_WIDTH - 1 - k, tr), :]
            shifted = hbuf[pl.ds(CONV_HALO - (CONV_WIDTH - 1) + k, tr), :]
            dws.append(jnp.sum(dout_t * shifted, axis=0, keepdims=True))
        dh_ref[...] = acc
        dw_t = jnp.concatenate(dws, axis=0)
        db_t = jnp.sum(dout_t, axis=0, keepdims=True)

        @pl.when(i == 0)
        def _():
            dw_ref[...] = dw_t
            db_ref[...] = db_t

        @pl.when(i > 0)
        def _():
            dw_ref[...] += dw_t
            db_ref[...] += db_t

    tile = pl.BlockSpec((tr, d_dim), lambda i: (i, 0))
    before = pl.BlockSpec((CONV_HALO, d_dim), lambda i: (jnp.maximum(i * per - 1, 0), 0))
    after = pl.BlockSpec((CONV_HALO, d_dim), lambda i: (jnp.minimum((i + 1) * per, last_halo), 0))
    return pl.pallas_call(
        body, name=_name(base), grid=(n,),
        in_specs=[tile, before, tile, after, pl.BlockSpec((CONV_WIDTH, d_dim), lambda i: (0, 0))],
        out_specs=[tile, pl.BlockSpec((CONV_WIDTH, d_dim), lambda i: (0, 0)), pl.BlockSpec((1, d_dim), lambda i: (0, 0))],
        out_shape=[SDS((s_dim, d_dim), F32), SDS((CONV_WIDTH, d_dim), F32), SDS((1, d_dim), F32)],
        scratch_shapes=[pltpu.VMEM((tr + CONV_HALO, d_dim), F32), pltpu.VMEM((tr + CONV_HALO, d_dim), F32)],
        compiler_params=_params(("arbitrary",)),
    )(h, h, dout, dout, w)


def _heads(t, h_dim):
    s_dim = t.shape[0]
    return t.reshape(s_dim, h_dim, HEAD_DIM).transpose(1, 0, 2)


def _unheads(t):
    h_dim, s_dim, hd = t.shape
    return t.transpose(1, 0, 2).reshape(s_dim, h_dim * hd)


def _local_step(x, mod, w, target):
    s_dim, d_dim = x.shape
    depth = mod.shape[0]
    h_dim = d_dim // HEAD_DIM
    row = lambda dt=F32, n=d_dim: SDS((s_dim, n), dt)
    vec = lambda n=d_dim: SDS((1, n), F32)
    mods = [[mod[l:l + 1, k * d_dim:(k + 1) * d_dim] for k in range(6)] for l in range(depth)]
    ng = lambda l, k: w["norm_g"][l, k:k + 1, :]
    saved = []
    h = x
    for l in range(depth):
        sh_m, sc_m, g_m, sh_f, sc_f, g_f = mods[l]
        kind, j = l % 3, l // 3
        s = {"h0": h}
        if kind == 1:
            (u,) = _rowwise(lambda h, g, a, b: ((_pre(h, g, a, b),), ()), [h], [ng(l, 0), sh_m, sc_m], [row()], [], "pre")
        else:
            (u,) = _rowwise(lambda h, g, a, b: ((_pre(h, g, a, b),), ()), [h], [ng(l, 0), sh_m, sc_m], [row(BF16)], [], "pre")
        s["u"] = u
        bias = None
        if kind == 0:
            qkv = _matmul(u, w["sb_w_qkv"][j], "nn", BF16, "qkv")
            q, k, v = (_heads(qkv[:, i * d_dim:(i + 1) * d_dim], h_dim) for i in range(3))
            o = _unheads(_sb_fwd(q, k, v, "sb_fwd")).astype(BF16)
            m = _matmul(o, w["sb_w_o"][j], "nn", F32, "wo")
            s.update(q=q, k=k, v=v, o=o)
        elif kind == 1:
            p = w["s5"][j]
            y_pre, gy, st_r, st_i = _s5_fwd(u, p["ar"], p["ai"], p["wr"], p["wi"], p["cr"], p["ci"], w["s5_d"][j:j + 1],
                                            "s5_fwd")
            t = _matmul(gy, w["s5_w_glu"][j], "nn", F32, "s5_glu")
            b_glu = w["s5_b_glu"][j:j + 1]
            (m,) = _rowwise(lambda t, b: ((_glu(t, b),), ()), [t], [b_glu], [row()], [], "s5_gate")
            s.update(y_pre=y_pre, gy=gy, st_r=st_r, st_i=st_i, t=t)
        else:
            t = _matmul(u, w["cv_w_pw1"][j], "nn", F32, "pw1")
            (hg,) = _rowwise(lambda t, b: ((_glu(t, b),), ()), [t], [w["cv_b_pw1"][j:j + 1]], [row()], [], "cv_glu")
            hc = _dwconv_fwd(hg, w["cv_w_dw"][j], w["cv_b_dw"][j:j + 1], "dwconv")
            (hs,) = _rowwise(lambda h, g, b: ((_ln_silu(h, g, b),), ()), [hc],
                             [w["cv_ln_g"][j:j + 1], w["cv_ln_b"][j:j + 1]], [row(BF16)], [], "cv_ln")
            m = _matmul(hs, w["cv_w_pw2"][j], "nn", F32, "pw2")
            bias = w["cv_b_pw2"][j:j + 1]
            s.update(t=t, hg=hg, hc=hc, hs=hs)
        s["m"] = m
        if bias is None:
            h1, u2 = _rowwise(
                lambda h, m, gate, g1, g2, a, b: ((lambda h1: (h1, _pre(h1, g2, a, b)))(_post(h, m, gate, g1)), ()),
                [h, m], [g_m, ng(l, 1), ng(l, 2), sh_f, sc_f], [row(), row(BF16)], [], "post_pre")
        else:
            h1, u2 = _rowwise(
                lambda h, m, gate, g1, g2, a, b, bb: ((lambda h1: (h1, _pre(h1, g2, a, b)))(_post(h, m + bb, gate, g1)), ()),
                [h, m], [g_m, ng(l, 1), ng(l, 2), sh_f, sc_f, bias], [row(), row(BF16)], [], "post_pre")
        f_dim = w["ffn_w_gu"].shape[2] // 2
        gu = _matmul(u2, w["ffn_w_gu"][l], "nn", BF16, "ffn_gu")
        (act,) = _rowwise(lambda gu: ((_swish_gate(_f(gu)),), ()), [gu], [], [row(BF16, f_dim)], [], "ffn_act")
        f = _matmul(act, w["ffn_w_down"][l], "nn", F32, "ffn_down")
        (h,) = _rowwise(lambda h, m, gate, g: ((_post(h, m, gate, g),), ()), [h1, f], [g_f, ng(l, 3)], [row()], [], "post")
        s.update(h1=h1, u2=u2, gu=gu, act=act, f=f, bias=bias)
        saved.append(s)
    dh, loss_cols = _rowwise(
        lambda y, t: ((((y - t) * (1.0 / d_dim)),), (jnp.sum(jnp.square(y - t), axis=0, keepdims=True),)),
        [h, target], [], [row()], [vec()], "loss")
    grads = {}
    dmod = [[None] * 6 for _ in range(depth)]
    dnorm = [[None] * 4 for _ in range(depth)]

    def add(name, idx, val):
        grads.setdefault(name, {})[idx] = val

    for l in reversed(range(depth)):
        sh_m, sc_m, g_m, sh_f, sc_f, g_f = mods[l]
        kind, j = l % 3, l // 3
        s = saved[l]

        def post_bwd(h, m, dh, gate, g):
            _, vjp = jax.vjp(_post, h, m, gate, g)
            _, dm, dgate, dg = vjp(dh)
            return (dm,), (dgate, dg)

        df, dmod[l][5], dnorm[l][3] = _rowwise(post_bwd, [s["h1"], s["f"], dh], [g_f, ng(l, 3)],
                                               [row(BF16)], [vec(), vec()], "post_bwd")
        f_dim = s["act"].shape[1]
        add("ffn_w_down", l, _matmul(s["act"], df, "tn", F32, "d_down"))
        dact = _matmul(df, w["ffn_w_down"][l], "nt", BF16, "d_act")

        def act_bwd(gu, da):
            _, vjp = jax.vjp(_swish2, *_halves(_f(gu)))
            return (jnp.concatenate(vjp(_f(da)), axis=1),), ()

        (dgu,) = _rowwise(act_bwd, [s["gu"], dact], [], [row(BF16, 2 * f_dim)], [], "act_bwd")
        add("ffn_w_gu", l, _matmul(s["u2"], dgu, "tn", F32, "d_gu"))
        du2 = _matmul(dgu, w["ffn_w_gu"][l], "nt", F32, "d_u2")
        bias = s["bias"]

        def post_pre_bwd(h, m, dh1, du2, gate, g1, g2, a, b, *bb):
            def fwd(h, m, gate, g1, g2, a, b):
                mm = m + bb[0] if bb else m
                h1 = _post(h, mm, gate, g1)
                return h1, _pre(h1, g2, a, b)
            _, vjp = jax.vjp(fwd, h, m, gate, g1, g2, a, b)
            dh0, dm, dgate, dg1, dg2, da, db = vjp((dh1, du2))
            return (dh0, dm), (dgate, dg1, dg2, da, db)

        vecs = [g_m, ng(l, 1), ng(l, 2), sh_f, sc_f] + ([bias] if bias is not None else [])
        dh, dm, dmod[l][2], dnorm[l][1], dnorm[l][2], dmod[l][3], dmod[l][4] = _rowwise(
            post_pre_bwd, [s["h0"], s["m"], dh, du2], vecs, [row(), row()], [vec()] * 5, "post_pre_bwd")
        if kind == 0:
            (dmb,) = _rowwise(lambda a: ((a,), ()), [dm], [], [row(BF16)], [], "cast")
            add("sb_w_o", j, _matmul(s["o"], dmb, "tn", F32, "d_wo"))
            do = _heads(_matmul(dmb, w["sb_w_o"][j], "nt", BF16, "d_o"), h_dim)
            dq, dk, dv = _sb_bwd(s["q"], s["k"], s["v"], do, "sb_bwd")
            dqkv = jnp.concatenate([_unheads(t) for t in (dq, dk, dv)], axis=1).astype(BF16)
            add("sb_w_qkv", j, _matmul(s["u"], dqkv, "tn", F32, "d_wqkv"))
            du = _matmul(dqkv, w["sb_w_qkv"][j], "nt", F32, "d_u")
        elif kind == 1:
            p = w["s5"][j]
            b_glu = w["s5_b_glu"][j:j + 1]

            def gate_bwd(t, dm, b):
                dt, db = _glu_bwd(t, dm, b)
                return (dt,), (db,)

            dt, db_glu = _rowwise(gate_bwd, [s["t"], dm], [b_glu], [row(BF16, 2 * d_dim)], [vec(2 * d_dim)], "s5_gate_bwd")
            add("s5_b_glu", j, db_glu)
            add("s5_w_glu", j, _matmul(s["gy"], dt, "tn", F32, "d_wglu"))
            dgy = _matmul(dt, w["s5_w_glu"][j], "nt", F32, "d_gy")

            def gelu_bwd(y, dg):
                _, vjp = jax.vjp(jax.nn.gelu, y)
                return (vjp(dg)[0],), ()

            (dy,) = _rowwise(gelu_bwd, [s["y_pre"], dgy], [], [row()], [], "gelu_bwd")
            chunk = s_dim // (s["st_r"].shape[0] // SUBLANES)
            du, dwr, dwi, dcr, dci, dar, dai, dd = _s5_bwd(
                s["u"], dy, s["st_r"], s["st_i"], p["ar"], p["ai"], p["wr"], p["wi"], p["cr"], p["ci"],
                w["s5_d"][j:j + 1], "s5_bwd", chunk)
            add("s5_d", j, dd)
            add("s5_raw", j, (dwr, dwi, dcr, dci, dar, dai))
        else:
            (dmb,) = _rowwise(lambda a: ((a,), ()), [dm], [], [row(BF16)], [], "cast")
            add("cv_w_pw2", j, _matmul(s["hs"], dmb, "tn", F32, "d_pw2"))
            dhs = _matmul(dmb, w["cv_w_pw2"][j], "nt", F32, "d_hs")
            ln_g, ln_b = w["cv_ln_g"][j:j + 1], w["cv_ln_b"][j:j + 1]

            def ln_bwd(hc, dhs, dm, g, b):
                _, vjp = jax.vjp(_ln_silu, hc, g, b)
                dhc, dg, db = vjp(dhs)
                return (dhc,), (dg, db, jnp.sum(dm, axis=0, keepdims=True))

            dhc, dln_g, dln_b, db_pw2 = _rowwise(ln_bwd, [s["hc"], dhs, dm], [ln_g, ln_b], [row()], [vec()] * 3, "ln_bwd")
            add("cv_ln_g", j, dln_g)
            add("cv_ln_b", j, dln_b)
            add("cv_b_pw2", j, db_pw2)
            dhg, dw_dw, db_dw = _dwconv_bwd(s["hg"], dhc, w["cv_w_dw"][j], "dwconv_bwd")
            add("cv_w_dw", j, dw_dw)
            add("cv_b_dw", j, db_dw)
            b_pw1 = w["cv_b_pw1"][j:j + 1]

            def glu_bwd(t, dhg, b):
                dt, db = _glu_bwd(t, dhg, b)
                return (dt,), (db,)

            dt, db_pw1 = _rowwise(glu_bwd, [s["t"], dhg], [b_pw1], [row(BF16, 2 * d_dim)], [vec(2 * d_dim)], "cv_glu_bwd")
            add("cv_b_pw1", j, db_pw1)
            add("cv_w_pw1", j, _matmul(s["u"], dt, "tn", F32, "d_pw1"))
            du = _matmul(dt, w["cv_w_pw1"][j], "nt", F32, "d_u")

        def pre_bwd(h, dh, du, g, a, b):
            _, vjp = jax.vjp(_pre, h, g, a, b)
            dh0, dg, da, db = vjp(du)
            return (dh + dh0,), (dg, da, db)

        dh, dnorm[l][0], dmod[l][0], dmod[l][1] = _rowwise(pre_bwd, [s["h0"], dh, du], [ng(l, 0), sh_m, sc_m],
                                                            [row()], [vec()] * 3, "pre_bwd")
    grads["norm_g"] = jnp.stack([jnp.concatenate(r, axis=0) for r in dnorm])
    dmod_arr = jnp.concatenate([jnp.concatenate(r, axis=1) for r in dmod], axis=0)
    return loss_cols, dh, dmod_arr, grads


def _position():
    return lax.axis_index("x"), lax.axis_index("y"), lax.axis_index("c")


def _all_gather8(x, base):
    m, n = x.shape
    space = pltpu.VMEM

    def body(x_ref, out_ref, send_sems, recv_sems, local_sem):
        px, py, pc = _position()
        me, sibling = (px, py, pc), (px, py, 1 - pc)
        chips = [(1 - px, py), (px, 1 - py), (1 - px, 1 - py)]
        src_mine = x_ref

        def rows(bx, by, bc):
            return out_ref.at[pl.ds((4 * bx + 2 * by + bc) * m, m), :]

        def copy(k, block, to, src=None):
            return pltpu.make_async_remote_copy(
                src_ref=rows(*block) if src is None else src, dst_ref=rows(*block),
                send_sem=send_sems.at[k], recv_sem=recv_sems.at[k], device_id=to, device_id_type=MESH)

        mine = pltpu.make_async_copy(src_mine, rows(*me), local_sem)
        mine.start()
        first = [copy(0, me, sibling, src=src_mine)]
        first += [copy(1 + j, me, (*chip, pc), src=src_mine) for j, chip in enumerate(chips)]
        for cp in first:
            cp.start()
        passed = [copy(4 + j, (*chip, pc), sibling) for j, chip in enumerate(chips)]
        for j, chip in enumerate(chips):
            copy(1 + j, (*chip, pc), me).wait_recv()
            passed[j].start()
        copy(0, sibling, me).wait_recv()
        for j, chip in enumerate(chips):
            copy(4 + j, (*chip, 1 - pc), me).wait_recv()
        for cp in first + passed:
            cp.wait_send()
        mine.wait()

    return pl.pallas_call(
        body, name=_name(base), out_shape=SDS((8 * m, n), x.dtype),
        in_specs=[pl.BlockSpec(memory_space=space)], out_specs=pl.BlockSpec(memory_space=space),
        scratch_shapes=[pltpu.SemaphoreType.DMA((7,)), pltpu.SemaphoreType.DMA((7,)), pltpu.SemaphoreType.DMA],
    )(x)


_ANY = pl.BlockSpec(memory_space=pl.ANY)


def _half(ref, core, lead):
    k_half = ref.shape[lead] // 2
    return ref.at[(slice(None),) * lead + (pl.ds(core * k_half, k_half), slice(None))]


def _gather_weights(shards, base):
    nw = len(shards)

    def body(*refs):
        x_refs, out_refs = refs[:nw], refs[nw:2 * nw]
        send_sems, recv_sems, local_sems = refs[2 * nw:]
        px, py, pc = _position()
        me, sibling = (px, py, pc), (px, py, 1 - pc)
        chips = [(1 - px, py), (px, 1 - py), (1 - px, 1 - py)]

        def block(i, bx, by, bc):
            return _half(out_refs[i].at[2 * bx + by], bc, 1)

        def copy(i, k, blk, to, src=None):
            dst = block(i, *blk)
            return pltpu.make_async_remote_copy(
                src_ref=dst if src is None else src, dst_ref=dst, send_sem=send_sems.at[7 * i + k],
                recv_sem=recv_sems.at[7 * i + k], device_id=to, device_id_type=MESH)

        mine = [_half(x_refs[i], pc, 1) for i in range(nw)]
        local = [pltpu.make_async_copy(mine[i], block(i, *me), local_sems.at[i]) for i in range(nw)]
        for cp in local:
            cp.start()
        first = []
        for i in range(nw):
            first.append(copy(i, 0, me, sibling, src=mine[i]))
            first += [copy(i, 1 + j, me, (*chip, pc), src=mine[i]) for j, chip in enumerate(chips)]
        for cp in first:
            cp.start()
        passed = []
        for j, chip in enumerate(chips):
            for i in range(nw):
                copy(i, 1 + j, (*chip, pc), me).wait_recv()
                passed.append(copy(i, 4 + j, (*chip, pc), sibling))
                passed[-1].start()
        for i in range(nw):
            copy(i, 0, sibling, me).wait_recv()
        for j, chip in enumerate(chips):
            for i in range(nw):
                copy(i, 4 + j, (*chip, 1 - pc), me).wait_recv()
        for cp in first + passed:
            cp.wait_send()
        for cp in local:
            cp.wait()

    return pl.pallas_call(
        body, name=_name(base), out_shape=[SDS((4,) + s.shape, s.dtype) for s in shards],
        in_specs=[_ANY] * nw, out_specs=[_ANY] * nw,
        scratch_shapes=[pltpu.SemaphoreType.DMA((7 * nw,)), pltpu.SemaphoreType.DMA((7 * nw,)),
                        pltpu.SemaphoreType.DMA((nw,))],
    )(*shards)


def _reduce_to_sibling(gs, base):
    nw = len(gs)

    def body(*refs):
        g_refs, recv_refs, send_sems, recv_sems = refs[:nw], refs[nw:2 * nw], refs[2 * nw], refs[2 * nw + 1]
        px, py, pc = _position()
        copies = [pltpu.make_async_remote_copy(
            src_ref=_half(g_refs[i], 1 - pc, 2), dst_ref=recv_refs[i], send_sem=send_sems.at[i],
            recv_sem=recv_sems.at[i], device_id=(px, py, 1 - pc), device_id_type=MESH) for i in range(nw)]
        for cp in copies:
            cp.start()
        for cp in copies:
            cp.wait_recv()
        for cp in copies:
            cp.wait_send()

    outs = [SDS(g.shape[:2] + (g.shape[2] // 2, g.shape[3]), g.dtype) for g in gs]
    return pl.pallas_call(
        body, name=_name(base), out_shape=outs, in_specs=[_ANY] * nw, out_specs=[_ANY] * nw,
        scratch_shapes=[pltpu.SemaphoreType.DMA((nw,)), pltpu.SemaphoreType.DMA((nw,))],
    )(*gs)


def _exchange_chips(ts, base):
    nw = len(ts)

    def body(*refs):
        t_refs, recv_refs, send_sems, recv_sems = refs[:nw], refs[nw:2 * nw], refs[2 * nw], refs[2 * nw + 1]
        px, py, pc = _position()
        chips = [(1 - px, py), (px, 1 - py), (1 - px, 1 - py)]
        copies = [pltpu.make_async_remote_copy(
            src_ref=t_refs[i].at[2 * cx + cy], dst_ref=recv_refs[i].at[k], send_sem=send_sems.at[3 * i + k],
            recv_sem=recv_sems.at[3 * i + k], device_id=(cx, cy, pc), device_id_type=MESH)
            for k, (cx, cy) in enumerate(chips) for i in range(nw)]
        for cp in copies:
            cp.start()
        for cp in copies:
            cp.wait_recv()
        for cp in copies:
            cp.wait_send()

    outs = [SDS((3,) + t.shape[1:], t.dtype) for t in ts]
    return pl.pallas_call(
        body, name=_name(base), out_shape=outs, in_specs=[_ANY] * nw, out_specs=[_ANY] * nw,
        scratch_shapes=[pltpu.SemaphoreType.DMA((3 * nw,)), pltpu.SemaphoreType.DMA((3 * nw,))],
    )(*ts)


def _share_with_sibling(rs, base):
    nw = len(rs)

    def body(*refs):
        r_refs, out_refs = refs[:nw], refs[nw:2 * nw]
        send_sems, recv_sems, local_sems = refs[2 * nw:]
        px, py, pc = _position()
        local = [pltpu.make_async_copy(r_refs[i], _half(out_refs[i], pc, 1), local_sems.at[i]) for i in range(nw)]
        send = [pltpu.make_async_remote_copy(
            src_ref=r_refs[i], dst_ref=_half(out_refs[i], pc, 1), send_sem=send_sems.at[i], recv_sem=recv_sems.at[i],
            device_id=(px, py, 1 - pc), device_id_type=MESH) for i in range(nw)]
        for cp in local + send:
            cp.start()
        for i in range(nw):
            pltpu.make_async_remote_copy(
                src_ref=r_refs[i], dst_ref=_half(out_refs[i], 1 - pc, 1), send_sem=send_sems.at[i],
                recv_sem=recv_sems.at[i], device_id=(px, py, 1 - pc), device_id_type=MESH).wait_recv()
        for cp in send:
            cp.wait_send()
        for cp in local:
            cp.wait()

    outs = [SDS((r.shape[0], 2 * r.shape[1], r.shape[2]), r.dtype) for r in rs]
    return pl.pallas_call(
        body, name=_name(base), out_shape=outs, in_specs=[_ANY] * nw, out_specs=[_ANY] * nw,
        scratch_shapes=[pltpu.SemaphoreType.DMA((nw,)), pltpu.SemaphoreType.DMA((nw,)), pltpu.SemaphoreType.DMA((nw,))],
    )(*rs)


def _add_own_half(g, recv, core, base, tr=512):
    _, l_dim, k_half, n = recv.shape
    tr = _tile(k_half, tr, SUBLANES)
    per = k_half // tr

    def body(core_ref, g_ref, r_ref, o_ref):
        o_ref[...] = g_ref[...] + r_ref[...]

    blk = (1, 1, tr, n)
    spec = pltpu.PrefetchScalarGridSpec(
        num_scalar_prefetch=1, grid=(4, l_dim, per),
        in_specs=[pl.BlockSpec(blk, lambda j, l, i, core: (j, l, core[0] * per + i, 0)),
                  pl.BlockSpec(blk, lambda j, l, i, core: (j, l, i, 0))],
        out_specs=pl.BlockSpec(blk, lambda j, l, i, core: (j, l, i, 0)))
    return pl.pallas_call(body, name=_name(base), grid_spec=spec, out_shape=SDS(recv.shape, F32),
                          compiler_params=_params(("parallel", "parallel", "parallel")))(core, g, recv)


def _add_chips(t, recv, chip, base, tr=512):
    _, l_dim, k_half, n = t.shape
    tr = _tile(k_half, tr, SUBLANES)

    def body(chip_ref, t_ref, r_ref, o_ref):
        o_ref[0] = ((t_ref[0, 0] + r_ref[0, 0]) + r_ref[1, 0]) + r_ref[2, 0]

    spec = pltpu.PrefetchScalarGridSpec(
        num_scalar_prefetch=1, grid=(l_dim, k_half // tr),
        in_specs=[pl.BlockSpec((1, 1, tr, n), lambda l, i, chip: (chip[0], l, i, 0)),
                  pl.BlockSpec((3, 1, tr, n), lambda l, i, chip: (0, l, i, 0))],
        out_specs=pl.BlockSpec((1, tr, n), lambda l, i, chip: (l, i, 0)))
    return pl.pallas_call(body, name=_name(base), grid_spec=spec, out_shape=SDS((l_dim, k_half, n), F32),
                          compiler_params=_params(("parallel", "parallel")))(chip, t, recv)


def _sum8(g, base):
    r, n = g.shape[0] // 8, g.shape[1]

    def body(g_ref, o_ref):
        acc = g_ref[pl.ds(0, r), :]
        for d in range(1, 8):
            acc = acc + g_ref[pl.ds(d * r, r), :]
        o_ref[...] = acc

    return pl.pallas_call(body, name=_name(base), out_shape=SDS((r, n), F32), compiler_params=_params())(g)


def _adamw_math(w, g, m, v):
    m = ADAM_B1 * m + (1.0 - ADAM_B1) * g
    v = ADAM_B2 * v + (1.0 - ADAM_B2) * jnp.square(g)
    m_hat = m / (1.0 - ADAM_B1 ** ADAM_STEP)
    v_hat = v / (1.0 - ADAM_B2 ** ADAM_STEP)
    delta = -ADAM_LR * (m_hat / (jnp.sqrt(v_hat) + ADAM_EPS) + ADAM_WD * w)
    return delta, m, v


def _adamw(w, g, m, v, base):
    out = SDS(w.shape, F32)
    return _rowwise(lambda w, g, m, v: (_adamw_math(w, g, m, v), ()), [w, g, m, v], [], [out, out, out], [], base)


def _mod_grad_adamw(sc_t, dmod, w, m, v, base, tr=256):
    l_dim, d_dim, n = w.shape
    tr = _tile(d_dim, tr, SUBLANES)

    def body(sc_ref, dm_ref, w_ref, m_ref, v_ref, g_ref, d_ref, nm_ref, nv_ref):
        g = lax.dot_general(sc_ref[...], dm_ref[0], _DIMS["nn"], preferred_element_type=F32,
                            precision=lax.Precision.HIGHEST)
        delta, nm, nv = _adamw_math(w_ref[0], g, m_ref[0], v_ref[0])
        g_ref[0], d_ref[0], nm_ref[0], nv_ref[0] = g, delta, nm, nv

    tile = pl.BlockSpec((1, tr, n), lambda l, i: (l, i, 0))
    out = SDS(w.shape, F32)
    return pl.pallas_call(
        body, name=_name(base), grid=(l_dim, d_dim // tr),
        in_specs=[pl.BlockSpec((tr, 8), lambda l, i: (i, 0)), pl.BlockSpec((1, 8, n), lambda l, i: (l, 0, 0)),
                  tile, tile, tile],
        out_specs=[tile] * 4, out_shape=[out] * 4, compiler_params=_params(("parallel", "parallel")),
    )(sc_t, dmod, w, m, v)


def _mod_forward(c_all, w_mod, b_mod, base):
    l_dim, d_dim, n = w_mod.shape

    def body(c_ref, w_ref, b_ref, o_ref):
        o_ref[0] = lax.dot_general(jax.nn.silu(c_ref[...]), w_ref[0], _DIMS["nn"], preferred_element_type=F32,
                                   precision=lax.Precision.HIGHEST) + b_ref[0]

    return pl.pallas_call(
        body, name=_name(base), grid=(l_dim,),
        in_specs=[pl.BlockSpec((8, d_dim), lambda l: (0, 0)), pl.BlockSpec((1, d_dim, n), lambda l: (l, 0, 0)),
                  pl.BlockSpec((1, 1, n), lambda l: (l, 0, 0))],
        out_specs=pl.BlockSpec((1, 8, n), lambda l: (l, 0, 0)), out_shape=SDS((l_dim, 8, n), F32),
        compiler_params=_params(("parallel",)),
    )(c_all, w_mod, b_mod)


def _pack(arrays, dtype, row_mult):
    flat = jnp.concatenate([a.reshape(-1).astype(dtype) for a in arrays])
    quantum = PACK_LANES * row_mult
    pad = (-flat.shape[0]) % quantum
    return jnp.pad(flat, (0, pad)).reshape(-1, PACK_LANES)


def _unpack(flat, shapes):
    out, off = [], 0
    for shp in shapes:
        n = math.prod(shp)
        out.append(flat[off:off + n].reshape(shp))
        off += n
    return out


_WEIGHTS = ["norm_g", "w_mod", "b_mod", "sb_w_qkv", "sb_w_o", "s5_lam_re", "s5_lam_im", "s5_log_dt", "s5_b_re",
            "s5_b_im", "s5_c_re", "s5_c_im", "s5_d", "s5_w_glu", "s5_b_glu", "cv_w_pw1", "cv_b_pw1", "cv_w_dw",
            "cv_b_dw", "cv_ln_g", "cv_ln_b", "cv_w_pw2", "cv_b_pw2", "ffn_w_gate", "ffn_w_up", "ffn_w_down"]
_BIG_COL = ["sb_w_qkv", "s5_w_glu", "cv_w_pw1", "ffn_w_gate", "ffn_w_up"]
_BIG_ROW = ["sb_w_o", "cv_w_pw2", "ffn_w_down"]
_BIG = _BIG_COL + _BIG_ROW
_SMALL_SHARDED = {"norm_g": 2, "cv_b_pw1": 1, "cv_w_dw": 2, "cv_b_dw": 1, "cv_ln_g": 1, "cv_ln_b": 1, "cv_b_pw2": 1}
_REPLICATED = ["b_mod", "s5_lam_re", "s5_lam_im", "s5_log_dt", "s5_b_re", "s5_b_im", "s5_c_re", "s5_c_im", "s5_d",
               "s5_b_glu"]


def _full_from_shards(stacked, name):
    if name in _BIG_COL or (name in _SMALL_SHARDED):
        axis = stacked.ndim - 1
    else:
        axis = stacked.ndim - 2
    moved = jnp.moveaxis(stacked, 0, axis - 1)
    shp = list(moved.shape)
    shp[axis - 1:axis + 1] = [shp[axis - 1] * shp[axis]]
    return moved.reshape(shp)


def _shards_from_full(full, name):
    axis = full.ndim - 1 if (name in _BIG_COL or name in _SMALL_SHARDED) else full.ndim - 2
    shp = list(full.shape)
    shp[axis:axis + 1] = [4, shp[axis] // 4]
    return jnp.moveaxis(full.reshape(shp), axis, 0)


def kernel(x, c, norm_g, w_mod, b_mod, sb_w_qkv, sb_w_o, s5_lam_re, s5_lam_im, s5_log_dt, s5_b_re, s5_b_im, s5_c_re, s5_c_im, s5_d, s5_w_glu, s5_b_glu, cv_w_pw1, cv_b_pw1, cv_w_dw, cv_b_dw, cv_ln_g, cv_ln_b, cv_w_pw2, cv_b_pw2, ffn_w_gate, ffn_w_up, ffn_w_down, loss_target, m_norm_g, m_w_mod, m_b_mod, m_sb_w_qkv, m_sb_w_o, m_s5_lam_re, m_s5_lam_im, m_s5_log_dt, m_s5_b_re, m_s5_b_im, m_s5_c_re, m_s5_c_im, m_s5_d, m_s5_w_glu, m_s5_b_glu, m_cv_w_pw1, m_cv_b_pw1, m_cv_w_dw, m_cv_b_dw, m_cv_ln_g, m_cv_ln_b, m_cv_w_pw2, m_cv_b_pw2, m_ffn_w_gate, m_ffn_w_up, m_ffn_w_down, v_norm_g, v_w_mod, v_b_mod, v_sb_w_qkv, v_sb_w_o, v_s5_lam_re, v_s5_lam_im, v_s5_log_dt, v_s5_b_re, v_s5_b_im, v_s5_c_re, v_s5_c_im, v_s5_d, v_s5_w_glu, v_s5_b_glu, v_cv_w_pw1, v_cv_b_pw1, v_cv_w_dw, v_cv_b_dw, v_cv_ln_g, v_cv_ln_b, v_cv_w_pw2, v_cv_b_pw2, v_ffn_w_gate, v_ffn_w_up, v_ffn_w_down):
    args = locals()
    wts = {n: args[n] for n in _WEIGHTS}
    mom = {n: args["m_" + n] for n in _WEIGHTS}
    var = {n: args["v_" + n] for n in _WEIGHTS}
    px, py, pc = lax.axis_index("x"), lax.axis_index("y"), lax.axis_index("c")
    chip = 2 * px + py
    dev = 2 * chip + pc
    s_dim, d_dim = x.shape[1], x.shape[2]
    depth = norm_g.shape[0]
    x2, tgt = x[0], loss_target[0]

    small_names = list(_SMALL_SHARDED)
    small_in = _pack([c] + [wts[n] for n in small_names], F32, SUBLANES)
    small_all = _all_gather8(small_in, "gather_small").reshape(8, -1)
    c_all = small_all[:, :d_dim]
    full = {}
    off = d_dim
    for n in small_names:
        size = math.prod(wts[n].shape)
        stacked = small_all[0::2, off:off + size].reshape((4,) + wts[n].shape)
        full[n] = _full_from_shards(stacked, n)
        off += size

    n_mod = w_mod.shape[2]
    b_mod_cols = lax.dynamic_slice_in_dim(b_mod, chip * n_mod, n_mod, axis=1)[:, None, :]
    mod_part = _mod_forward(c_all, w_mod, b_mod_cols, "mod_fwd")
    mod_all = _all_gather8(mod_part.reshape(depth * 8, n_mod), "gather_mod")
    mod_all = mod_all.reshape(4, 2, depth, 8, n_mod)[:, 0]
    mod_mine = lax.dynamic_index_in_dim(mod_all, dev, axis=2, keepdims=False)
    mod = mod_mine.transpose(1, 0, 2).reshape(depth, 4 * n_mod)

    stacked = _gather_weights([wts[n].astype(BF16) for n in _BIG], "gather_big")
    for n, st in zip(_BIG, stacked, strict=True):
        full[n] = _full_from_shards(st, n)
    full["ffn_w_gu"] = jnp.concatenate([full.pop("ffn_w_gate"), full.pop("ffn_w_up")], axis=2)
    for n in _REPLICATED:
        full[n] = wts[n]

    n_s5 = s5_lam_re.shape[0]
    gp = s5_lam_re.shape[1] * s5_lam_re.shape[2]
    col = lambda a: a.reshape(gp, 1)
    s5_raw_in = []
    full["s5"] = []
    for j in range(n_s5):
        ldt = jnp.broadcast_to(s5_log_dt[j][:, None], s5_lam_re[j].shape)
        raw = (col(s5_lam_re[j]), col(s5_lam_im[j]), col(ldt), s5_b_re[j].reshape(gp, S5_GROUP), s5_b_im[j].reshape(gp, S5_GROUP))
        ar, ai, bbr, bbi = _s5_prep(*raw, "s5_prep")
        s5_raw_in.append(raw)
        shape3 = s5_b_re[j].shape
        full["s5"].append(dict(
            ar=ar.reshape(1, gp), ai=ai.reshape(1, gp),
            wr=_block_diag(bbr.reshape(shape3), False).astype(BF16), wi=_block_diag(bbi.reshape(shape3), False).astype(BF16),
            cr=_block_diag(s5_c_re[j], True).astype(BF16), ci=_block_diag(s5_c_im[j], True).astype(BF16)))

    loss_cols, grad_x, dmod, grads = _local_step(x2, mod, full, tgt)
    loss = lax.psum(0.5 * jnp.sum(loss_cols) / d_dim, ("x", "y", "c"))

    small_grads = {}
    for name in ("s5_lam_re", "s5_lam_im", "s5_log_dt", "s5_b_re", "s5_b_im", "s5_c_re", "s5_c_im"):
        small_grads[name] = []
    for j in range(n_s5):
        dwr, dwi, dcr, dci, dar, dai = grads["s5_raw"][j]
        shape3 = s5_b_re[j].shape
        cots = (jnp.sum(dar, axis=0).reshape(gp, 1), jnp.sum(dai, axis=0).reshape(gp, 1),
                _block_diag_extract(dwr, False).reshape(gp, S5_GROUP), _block_diag_extract(dwi, False).reshape(gp, S5_GROUP))
        dlr, dli, dldt, dbr, dbi = _s5_prep_bwd(*s5_raw_in[j], cots, "s5_prep_bwd")
        small_grads["s5_lam_re"].append(dlr.reshape(s5_lam_re[j].shape))
        small_grads["s5_lam_im"].append(dli.reshape(s5_lam_re[j].shape))
        small_grads["s5_log_dt"].append(jnp.sum(dldt.reshape(s5_lam_re[j].shape), axis=1))
        small_grads["s5_b_re"].append(dbr.reshape(shape3))
        small_grads["s5_b_im"].append(dbi.reshape(shape3))
        small_grads["s5_c_re"].append(_block_diag_extract(dcr, True))
        small_grads["s5_c_im"].append(_block_diag_extract(dci, True))
    small_grads = {n: jnp.stack(v) for n, v in small_grads.items()}
    stack = lambda name, count: jnp.stack([grads[name][j] for j in range(count)])
    small_grads["s5_d"] = stack("s5_d", n_s5)[:, 0]
    small_grads["s5_b_glu"] = stack("s5_b_glu", n_s5)[:, 0]
    n_cv = cv_w_pw1.shape[0]
    for name in ("cv_b_pw1", "cv_b_dw", "cv_ln_g", "cv_ln_b", "cv_b_pw2"):
        small_grads[name] = stack(name, n_cv)[:, 0]
    small_grads["cv_w_dw"] = stack("cv_w_dw", n_cv)
    small_grads["norm_g"] = grads["norm_g"]

    small_order = ["norm_g", "s5_lam_re", "s5_lam_im", "s5_log_dt", "s5_b_re", "s5_b_im", "s5_c_re", "s5_c_im", "s5_d",
                   "s5_b_glu", "cv_b_pw1", "cv_w_dw", "cv_b_dw", "cv_ln_g", "cv_ln_b", "cv_b_pw2"]
    small_shapes = [small_grads[n].shape for n in small_order]
    sg_in = _pack([dmod] + [small_grads[n] for n in small_order], F32, SUBLANES)
    sg_all = _all_gather8(sg_in, "gather_small_grads")
    sg_sum = _sum8(sg_all, "sum_small_grads").reshape(-1)
    dmod_all = sg_all.reshape(8, -1)[:, :dmod.size].reshape(8, depth, 4, n_mod)
    g_full = dict(zip(small_order, _unpack(sg_sum[dmod.size:], small_shapes), strict=True))
    g_full["b_mod"] = sg_sum[:dmod.size].reshape(dmod.shape)

    gu = stack("ffn_w_gu", depth)
    f_dim = gu.shape[2] // 2
    big_grads = {"sb_w_qkv": stack("sb_w_qkv", sb_w_qkv.shape[0]), "sb_w_o": stack("sb_w_o", sb_w_o.shape[0]),
                 "s5_w_glu": stack("s5_w_glu", n_s5), "cv_w_pw1": stack("cv_w_pw1", n_cv), "cv_w_pw2": stack("cv_w_pw2", n_cv),
                 "ffn_w_gate": gu[:, :, :f_dim], "ffn_w_up": gu[:, :, f_dim:], "ffn_w_down": stack("ffn_w_down", depth)}
    chip_major = [_shards_from_full(big_grads[n], n) for n in _BIG]
    core_idx, chip_idx = pc.reshape(1).astype(I32), chip.reshape(1).astype(I32)
    from_sibling = _reduce_to_sibling(chip_major, "reduce_sibling")
    chip_sums = [_add_own_half(g, r, core_idx, "reduce_add_sibling") for g, r in zip(chip_major, from_sibling, strict=True)]
    from_chips = _exchange_chips(chip_sums, "reduce_chips")
    my_half = [_add_chips(t, r, chip_idx, "reduce_add_chips") for t, r in zip(chip_sums, from_chips, strict=True)]
    g_shard = dict(zip(_BIG, _share_with_sibling(my_half, "reduce_share"), strict=True))

    out_g, out_d, out_m, out_v = {}, {}, {}, {}
    for n in _BIG:
        shp = wts[n].shape
        two = lambda a: a.reshape(-1, shp[-1])
        out_g[n] = g_shard[n]
        d, nm, nv = _adamw(two(wts[n]), two(g_shard[n]), two(mom[n]), two(var[n]), "adamw")
        out_d[n], out_m[n], out_v[n] = d.reshape(shp), nm.reshape(shp), nv.reshape(shp)
    sc_t = jax.nn.silu(c_all).T
    dmod_cols = lax.dynamic_index_in_dim(dmod_all, chip, axis=2, keepdims=False).transpose(1, 0, 2)
    out_g["w_mod"], out_d["w_mod"], out_m["w_mod"], out_v["w_mod"] = _mod_grad_adamw(
        sc_t, dmod_cols, w_mod, m_w_mod, v_w_mod, "w_mod_adamw")
    small_all_names = [n for n in _WEIGHTS if n not in _BIG and n != "w_mod"]
    local_g = []
    for n in small_all_names:
        g = g_full[n]
        if n in _SMALL_SHARDED:
            width = wts[n].shape[-1]
            g = lax.dynamic_slice_in_dim(g, chip * width, width, axis=g.ndim - 1)
        local_g.append(g)
        out_g[n] = g
    pk = lambda arrs: _pack(arrs, F32, SUBLANES)
    d, nm, nv = _adamw(pk([wts[n] for n in small_all_names]), pk(local_g), pk([mom[n] for n in small_all_names]),
                       pk([var[n] for n in small_all_names]), "adamw_small")
    shapes = [wts[n].shape for n in small_all_names]
    for n, a, b, cc in zip(small_all_names, _unpack(d.reshape(-1), shapes), _unpack(nm.reshape(-1), shapes),
                           _unpack(nv.reshape(-1), shapes), strict=True):
        out_d[n], out_m[n], out_v[n] = a, b, cc
    return (loss, grad_x[None], *[out_g[n] for n in _WEIGHTS], *[out_d[n] for n in _WEIGHTS],
            *[out_m[n] for n in _WEIGHTS], *[out_v[n] for n in _WEIGHTS])
```

```python
import functools
import math

import jax
import jax.numpy as jnp
from jax import lax
from jax.experimental import pallas as pl
from jax.experimental.pallas import tpu as pltpu

F32 = jnp.float32
BF16 = jnp.bfloat16
I32 = jnp.int32
SDS = jax.ShapeDtypeStruct
MESH = pl.DeviceIdType.MESH

HEAD_DIM = 64
HEAD_PAIR = 2 * HEAD_DIM
S5_GROUP = 16
S5_STATE = 64
S5_BLOCK_GROUPS = 8
S5_BLOCK_CH = S5_GROUP * S5_BLOCK_GROUPS
S5_BLOCK_ST = S5_STATE * S5_BLOCK_GROUPS
CONV_WIDTH = 31
CONV_HALO = 32
EPS = 1e-6
ADAM_LR = 0.001
ADAM_B1 = 0.9
ADAM_B2 = 0.999
ADAM_EPS = 1e-08
ADAM_WD = 0.01
ADAM_STEP = 10

LANES = 128
SUBLANES = 8
VMEM_LIMIT = 56 * 2 ** 20
PACK_LANES = 512
ATT_BLOCK = 128
EXP_FLOOR = -104.0

_COUNTER = [0]


def _name(base):
    _COUNTER[0] += 1
    return f"{base}_{_COUNTER[0]}"


def _tile(dim, target, mult=LANES):
    t = min(dim, target)
    t -= t % mult
    while t > mult and dim % t:
        t -= mult
    return t if t > 0 and dim % t == 0 else dim


def _params(sem=None):
    return pltpu.CompilerParams(dimension_semantics=sem, vmem_limit_bytes=VMEM_LIMIT)


_DIMS = {"nn": (((1,), (0,)), ((), ())), "nt": (((1,), (1,)), ((), ())), "tn": (((0,), (0,)), ((), ()))}


def _matmul(a, b, form, out_dtype, base, tm=1024, tn=1024, tk=1024):
    if form == "tn":
        k_dim, m_dim = a.shape
    else:
        m_dim, k_dim = a.shape
    n_dim = b.shape[0] if form == "nt" else b.shape[1]
    tm, tn, tk = _tile(m_dim, tm), _tile(n_dim, tn), _tile(k_dim, tk)
    nk = k_dim // tk
    dims = _DIMS[form]

    def body(a_ref, b_ref, o_ref, *acc):
        part = lax.dot_general(a_ref[...].astype(BF16), b_ref[...].astype(BF16), dims, preferred_element_type=F32)
        if nk == 1:
            o_ref[...] = part.astype(o_ref.dtype)
            return
        acc_ref, = acc
        k = pl.program_id(2)

        @pl.when(k == 0)
        def _():
            acc_ref[...] = part

        @pl.when(jnp.logical_and(k > 0, k < nk - 1))
        def _():
            acc_ref[...] += part

        @pl.when(k == nk - 1)
        def _():
            o_ref[...] = (acc_ref[...] + part).astype(o_ref.dtype)

    a_spec = (pl.BlockSpec((tk, tm), lambda i, j, k: (k, i)) if form == "tn"
              else pl.BlockSpec((tm, tk), lambda i, j, k: (i, k)))
    b_spec = (pl.BlockSpec((tn, tk), lambda i, j, k: (j, k)) if form == "nt"
              else pl.BlockSpec((tk, tn), lambda i, j, k: (k, j)))
    return pl.pallas_call(
        body, name=_name(base), grid=(m_dim // tm, n_dim // tn, nk),
        in_specs=[a_spec, b_spec], out_specs=pl.BlockSpec((tm, tn), lambda i, j, k: (i, j)),
        out_shape=SDS((m_dim, n_dim), out_dtype), scratch_shapes=[pltpu.VMEM((tm, tn), F32)] if nk > 1 else [],
        compiler_params=_params(("parallel", "parallel", "arbitrary")),
    )(a, b)


def _rowwise(fn, rows, vecs, out_rows, out_vecs, base, tr=256):
    s_dim = rows[0].shape[0]
    tr = _tile(s_dim, tr, SUBLANES)
    nr, nv, nor = len(rows), len(vecs), len(out_rows)

    def body(*refs):
        r_in, v_in = refs[:nr], refs[nr:nr + nv]
        r_out, v_out = refs[nr + nv:nr + nv + nor], refs[nr + nv + nor:]
        ro, vo = fn(*[r[...] for r in r_in], *[v[...] for v in v_in])
        for ref, val in zip(r_out, ro, strict=True):
            ref[...] = val.astype(ref.dtype)
        if v_out:
            i = pl.program_id(0)

            @pl.when(i == 0)
            def _():
                for ref, val in zip(v_out, vo, strict=True):
                    ref[...] = val.astype(ref.dtype)

            @pl.when(i > 0)
            def _():
                for ref, val in zip(v_out, vo, strict=True):
                    ref[...] += val.astype(ref.dtype)

    in_specs = [pl.BlockSpec((tr, r.shape[1]), lambda i: (i, 0)) for r in rows]
    in_specs += [pl.BlockSpec(v.shape, lambda i: (0, 0)) for v in vecs]
    out_specs = [pl.BlockSpec((tr, o.shape[1]), lambda i: (i, 0)) for o in out_rows]
    out_specs += [pl.BlockSpec(o.shape, lambda i: (0, 0)) for o in out_vecs]
    outs = pl.pallas_call(
        body, name=_name(base), grid=(s_dim // tr,), in_specs=in_specs, out_specs=out_specs,
        out_shape=list(out_rows) + list(out_vecs),
        compiler_params=_params(("arbitrary",)),
    )(*rows, *vecs)
    return outs


def _f(x):
    return x.astype(F32)


def _rms(x, g):
    return x * lax.rsqrt(jnp.mean(x * x, axis=-1, keepdims=True) + EPS) * g


def _pre(h, g, sh, sc):
    return _rms(h, g) * (1 + sc) + sh


def _post(h, m, gate, g):
    return h + gate * _rms(m, g)


def _halves(t):
    d = t.shape[1] // 2
    return t[:, :d], t[:, d:]


def _glu2(ta, tb, ba, bb):
    return (ta + ba) * jax.nn.sigmoid(tb + bb)


def _glu(t, b):
    return _glu2(*_halves(t), *_halves(b))


def _glu_bwd(t, dout, b):
    _, vjp = jax.vjp(_glu2, *_halves(t), *_halves(b))
    dta, dtb, dba, dbb = vjp(dout)
    return jnp.concatenate([dta, dtb], axis=1), jnp.concatenate([dba, dbb], axis=1)


def _swish2(g, u):
    return jax.nn.silu(g) * u


def _swish_gate(gu):
    return _swish2(*_halves(gu))


def _ln_silu(h, g, b):
    mu = jnp.mean(h, axis=-1, keepdims=True)
    var = jnp.mean(jnp.square(h - mu), axis=-1, keepdims=True)
    return jax.nn.silu((h - mu) * lax.rsqrt(var + EPS) * g + b)


def _dot(a, b, form="nn"):
    return lax.dot_general(a, b, _DIMS[form], preferred_element_type=F32)


def _split_dot(x, tri):
    hi = x.astype(BF16)
    lo = (x - hi.astype(F32)).astype(BF16)
    return _dot(hi, tri) + _dot(lo, tri)


def _sb_scores(qb, kb, scale, mask):
    z = _dot(qb, kb, "nt") * scale
    sp = jnp.maximum(z, 0.0) + jnp.log(1.0 + jnp.exp(-jnp.abs(z)))
    lk = -sp
    if mask is not None:
        lk = jnp.where(mask, lk, 0.0)
    return lk, z - sp


def _walk_left(i, state, step):
    def cond(ns):
        live = functools.reduce(jnp.maximum, [jnp.max(c) for c in ns[1][0]])
        return jnp.logical_and(ns[0] < i, live > EXP_FLOOR)

    def body(ns):
        return ns[0] + 1, step(i - 1 - ns[0], ns[1])

    return lax.while_loop(cond, body, (jnp.int32(0), state))[1]


def _pair_split(x):
    first = lax.broadcasted_iota(I32, x.shape, 1) < HEAD_DIM
    zero = jnp.zeros_like(x)
    return (jnp.where(first, x, zero), jnp.where(first, zero, x)), first


def _pair_specs(s_dim, d_dim, bq):
    npair = d_dim // HEAD_PAIR
    blk = pl.BlockSpec((bq, HEAD_PAIR), lambda p, i: (i, p))
    k_all = pl.BlockSpec((s_dim, HEAD_PAIR), lambda p, i: (0, npair + p))
    v_all = pl.BlockSpec((s_dim, HEAD_PAIR), lambda p, i: (0, 2 * npair + p))
    return npair, blk, k_all, v_all


def _sb_fwd(qkv, base):
    s_dim, d_dim = qkv.shape[0], qkv.shape[1] // 3
    bq = ATT_BLOCK
    scale = HEAD_DIM ** -0.5
    npair, blk, k_all, v_all = _pair_specs(s_dim, d_dim, bq)

    def body(q_ref, k_ref, v_ref, o_ref):
        i = pl.program_id(1)
        qs, first = _pair_split(q_ref[...])
        row = lax.broadcasted_iota(I32, (bq, bq), 0)
        col = lax.broadcasted_iota(I32, (bq, bq), 1)
        later = (row > col).astype(BF16)
        causal = col < row

        def tile(j, cs, accs, mask):
            ks = pl.ds(pl.multiple_of(j * bq, bq), bq)
            kb, vb = k_ref[ks, :], v_ref[ks, :]
            new_c, new_acc = [], []
            for qh, c, acc in zip(qs, cs, accs, strict=True):
                lk, lb = _sb_scores(qh, kb, scale, mask)
                w = jnp.exp(lb + _split_dot(lk, later) + c)
                if mask is not None:
                    w = jnp.where(mask, w, 0.0)
                new_acc.append(acc + _dot(w.astype(BF16), vb))
                new_c.append(c + jnp.sum(lk, axis=1, keepdims=True))
            return tuple(new_c), tuple(new_acc)

        zero_c, zero_acc = jnp.zeros((bq, 1), F32), jnp.zeros((bq, HEAD_PAIR), F32)
        cs, accs = tile(i, (zero_c, zero_c), (zero_acc, zero_acc), causal)
        cs, accs = _walk_left(i, (cs, accs), lambda j, s: tile(j, s[0], s[1], None))
        o_ref[...] = jnp.where(first, accs[0], accs[1]).astype(o_ref.dtype)

    return pl.pallas_call(
        body, name=_name(base), grid=(npair, s_dim // bq), in_specs=[blk, k_all, v_all], out_specs=blk,
        out_shape=SDS((s_dim, d_dim), BF16), compiler_params=_params(("parallel", "arbitrary")),
    )(qkv, qkv, qkv)


def _sb_bwd(qkv, do, base):
    s_dim, d_dim = do.shape
    bq = ATT_BLOCK
    scale = HEAD_DIM ** -0.5
    npair, blk, k_all, v_all = _pair_specs(s_dim, d_dim, bq)

    def body(q_ref, k_ref, v_ref, do_ref, dq_ref, dk_ref, dv_ref):
        i = pl.program_id(1)
        qs, first = _pair_split(q_ref[...])
        dos, _ = _pair_split(do_ref[...])
        heads = (0, 1)
        row = lax.broadcasted_iota(I32, (bq, bq), 0)
        col = lax.broadcasted_iota(I32, (bq, bq), 1)
        later = (row > col).astype(BF16)
        from_here = (row >= col).astype(BF16)
        causal = col < row

        @pl.when(i == 0)
        def _():
            dk_ref[...] = jnp.zeros_like(dk_ref)
            dv_ref[...] = jnp.zeros_like(dv_ref)

        def weights(h, kb, vb, c, mask):
            lk, lb = _sb_scores(qs[h], kb, scale, mask)
            w = jnp.exp(lb + _split_dot(lk, later) + c)
            if mask is not None:
                w = jnp.where(mask, w, 0.0)
            return lk, lb, w, _dot(dos[h], vb, "nt") * w

        def total(j, cs, tots, mask):
            ks = pl.ds(pl.multiple_of(j * bq, bq), bq)
            kb, vb = k_ref[ks, :], v_ref[ks, :]
            new_c, new_tot = [], []
            for h in heads:
                lk, _, _, de = weights(h, kb, vb, cs[h], mask)
                new_c.append(cs[h] + jnp.sum(lk, axis=1, keepdims=True))
                new_tot.append(tots[h] + jnp.sum(de, axis=1, keepdims=True))
            return tuple(new_c), tuple(new_tot)

        zero = jnp.zeros((bq, 1), F32)
        cs, tots = total(i, (zero, zero), (zero, zero), causal)
        cs, tots = _walk_left(i, (cs, tots), lambda j, s: total(j, s[0], s[1], None))

        def grads(j, cs, rs, dqs, mask):
            ks = pl.ds(pl.multiple_of(j * bq, bq), bq)
            kb, vb = k_ref[ks, :], v_ref[ks, :]
            new_c, new_r, new_dq = [], [], []
            dk_t = dv_t = None
            for h in heads:
                lk, lb, w, de = weights(h, kb, vb, cs[h], mask)
                before = tots[h] - (rs[h] + _split_dot(de, from_here))
                dz = de * jnp.exp(lk) - jnp.exp(lb) * before
                if mask is not None:
                    dz = jnp.where(mask, dz, 0.0)
                dzb = (dz * scale).astype(BF16)
                dk_h, dv_h = _dot(dzb, qs[h], "tn"), _dot(w.astype(BF16), dos[h], "tn")
                dk_t, dv_t = (dk_h, dv_h) if dk_t is None else (dk_t + dk_h, dv_t + dv_h)
                new_dq.append(dqs[h] + _dot(dzb, kb))
                new_c.append(cs[h] + jnp.sum(lk, axis=1, keepdims=True))
                new_r.append(rs[h] + jnp.sum(de, axis=1, keepdims=True))
            dk_ref[ks, :] += dk_t
            dv_ref[ks, :] += dv_t
            return tuple(new_c), tuple(new_r), tuple(new_dq)

        zero_dq = jnp.zeros((bq, HEAD_PAIR), F32)
        state = grads(i, (zero, zero), (zero, zero), (zero_dq, zero_dq), causal)
        _, _, dqs = _walk_left(i, state, lambda j, s: grads(j, s[0], s[1], s[2], None))
        dq_ref[...] = jnp.where(first, dqs[0], dqs[1])

    cols = pl.BlockSpec((s_dim, HEAD_PAIR), lambda p, i: (0, p))
    out = SDS((s_dim, d_dim), F32)
    return pl.pallas_call(
        body, name=_name(base), grid=(npair, s_dim // bq), in_specs=[blk, k_all, v_all, blk],
        out_specs=[blk, cols, cols], out_shape=[out, out, out], compiler_params=_params(("parallel", "arbitrary")),
    )(qkv, qkv, qkv, do)


def _cmul(ar, ai, br, bi):
    return ar * br - ai * bi, ar * bi + ai * br


def _s5_prep_fn(lr, li, ldt, br, bi):
    dt = jnp.exp(ldt)
    mag = jnp.exp(lr * dt)
    ar, ai = mag * jnp.cos(li * dt), mag * jnp.sin(li * dt)
    den = lr * lr + li * li
    er = ((ar - 1) * lr + ai * li) / den
    ei = (ai * lr - (ar - 1) * li) / den
    return ar, ai, er * br - ei * bi, er * bi + ei * br


def _s5_prep(lr, li, ldt, br, bi, base):
    col, mat = SDS(lr.shape, F32), SDS(br.shape, F32)
    return _rowwise(lambda *a: (_s5_prep_fn(*a), ()), [lr, li, ldt, br, bi], [], [col, col, mat, mat], [], base, tr=512)


def _s5_prep_bwd(lr, li, ldt, br, bi, cots, base):
    col, mat = SDS(lr.shape, F32), SDS(br.shape, F32)

    def bwd(lr, li, ldt, br, bi, c0, c1, c2, c3):
        _, vjp = jax.vjp(_s5_prep_fn, lr, li, ldt, br, bi)
        return vjp((c0, c1, c2, c3)), ()

    return _rowwise(bwd, [lr, li, ldt, br, bi, *cots], [], [col, col, col, mat, mat], [], base, tr=512)


def _scan_tables(ar_ref, ai_ref, sl, conj):
    shape = (SUBLANES, S5_BLOCK_ST)
    a_r = jnp.broadcast_to(ar_ref[:, sl], shape)
    a_i = jnp.broadcast_to(ai_ref[:, sl], shape)
    if conj:
        a_i = -a_i
    a2 = _cmul(a_r, a_i, a_r, a_i)
    a4 = _cmul(*a2, *a2)
    row = lax.broadcasted_iota(I32, shape, 0)
    p_r, p_i, t_r, t_i = a_r, a_i, a_r, a_i
    for k in range(1, SUBLANES):
        p_r, p_i = _cmul(p_r, p_i, a_r, a_i)
        sel = (row == (SUBLANES - 1 - k)) if conj else (row == k)
        t_r, t_i = jnp.where(sel, p_r, t_r), jnp.where(sel, p_i, t_i)
    if conj:
        t_r, t_i = jnp.where(row == SUBLANES - 1, a_r, t_r), jnp.where(row == SUBLANES - 1, a_i, t_i)
    return row, (a_r, a_i), a2, a4, (t_r, t_i)


def _scan_group(xr, xi, row, a1, a2, a4, table, cr, ci, reverse):
    for s, (mr, mi) in ((1, a1), (2, a2), (4, a4)):
        if reverse:
            keep = row < SUBLANES - s
            sr, si = pltpu.roll(xr, SUBLANES - s, 0), pltpu.roll(xi, SUBLANES - s, 0)
        else:
            keep = row >= s
            sr, si = pltpu.roll(xr, s, 0), pltpu.roll(xi, s, 0)
        sr, si = jnp.where(keep, sr, 0.0), jnp.where(keep, si, 0.0)
        xr, xi = xr + mr * sr - mi * si, xi + mr * si + mi * sr
    tr, ti = table
    return xr + tr * cr - ti * ci, xi + tr * ci + ti * cr


def _s5_fwd(u, ar, ai, wr, wi, cr_w, ci_w, dsk, base, chunk=128):
    s_dim, d_dim = u.shape
    nb = d_dim // S5_BLOCK_CH
    n_st = nb * S5_BLOCK_ST
    chunk = _tile(s_dim, chunk, SUBLANES)
    nch = s_dim // chunk

    def body(u_ref, ar_ref, ai_ref, wr_ref, wi_ref, cr_ref, ci_ref, d_ref, y_ref, gy_ref, sr_ref, si_ref,
             xr_s, xi_s, car_r, car_i):
        @pl.when(pl.program_id(0) == 0)
        def _():
            car_r[...] = jnp.zeros_like(car_r)
            car_i[...] = jnp.zeros_like(car_i)

        sr_ref[...] = car_r[...]
        si_ref[...] = car_i[...]
        for b in range(nb):
            ub = u_ref[:, b * S5_BLOCK_CH:(b + 1) * S5_BLOCK_CH].astype(BF16)
            sl = slice(b * S5_BLOCK_ST, (b + 1) * S5_BLOCK_ST)
            xr_s[:, sl] = _dot(ub, wr_ref[b])
            xi_s[:, sl] = _dot(ub, wi_ref[b])
        for b in range(nb):
            sl = slice(b * S5_BLOCK_ST, (b + 1) * S5_BLOCK_ST)
            row, a1, a2, a4, table = _scan_tables(ar_ref, ai_ref, sl, False)

            def group(g, carry, sl=sl, row=row, a1=a1, a2=a2, a4=a4, table=table):
                rows = pl.ds(pl.multiple_of(g * SUBLANES, SUBLANES), SUBLANES)
                xr, xi = _scan_group(xr_s[rows, sl], xi_s[rows, sl], row, a1, a2, a4, table, carry[0], carry[1], False)
                xr_s[rows, sl] = xr
                xi_s[rows, sl] = xi
                last = (SUBLANES, S5_BLOCK_ST)
                return (jnp.broadcast_to(xr[SUBLANES - 1:, :], last), jnp.broadcast_to(xi[SUBLANES - 1:, :], last))

            c_r, c_i = lax.fori_loop(0, chunk // SUBLANES, group, (car_r[:, sl], car_i[:, sl]))
            car_r[:, sl] = c_r
            car_i[:, sl] = c_i
        for b in range(nb):
            sl = slice(b * S5_BLOCK_ST, (b + 1) * S5_BLOCK_ST)
            ch = slice(b * S5_BLOCK_CH, (b + 1) * S5_BLOCK_CH)
            y = _dot(xr_s[:, sl].astype(BF16), cr_ref[b]) - _dot(xi_s[:, sl].astype(BF16), ci_ref[b])
            y = y + d_ref[:, ch] * u_ref[:, ch]
            y_ref[:, ch] = y
            gy_ref[:, ch] = jax.nn.gelu(y).astype(BF16)

    rows = pl.BlockSpec((chunk, d_dim), lambda i: (i, 0))
    vec = pl.BlockSpec((1, n_st), lambda i: (0, 0))
    w_in = pl.BlockSpec((nb, S5_BLOCK_CH, S5_BLOCK_ST), lambda i: (0, 0, 0))
    w_out = pl.BlockSpec((nb, S5_BLOCK_ST, S5_BLOCK_CH), lambda i: (0, 0, 0))
    st = pl.BlockSpec((SUBLANES, n_st), lambda i: (i, 0))
    return pl.pallas_call(
        body, name=_name(base), grid=(nch,),
        in_specs=[rows, vec, vec, w_in, w_in, w_out, w_out, pl.BlockSpec((1, d_dim), lambda i: (0, 0))],
        out_specs=[rows, rows, st, st],
        out_shape=[SDS((s_dim, d_dim), F32), SDS((s_dim, d_dim), BF16),
                   SDS((nch * SUBLANES, n_st), F32), SDS((nch * SUBLANES, n_st), F32)],
        scratch_shapes=[pltpu.VMEM((chunk, n_st), F32), pltpu.VMEM((chunk, n_st), F32),
                        pltpu.VMEM((SUBLANES, n_st), F32), pltpu.VMEM((SUBLANES, n_st), F32)],
        compiler_params=_params(("arbitrary",)),
    )(u, ar, ai, wr, wi, cr_w, ci_w, dsk)


def _s5_bwd(u, dy, st_r, st_i, ar, ai, wr, wi, cr_w, ci_w, dsk, base, chunk):
    s_dim, d_dim = u.shape
    nb = d_dim // S5_BLOCK_CH
    n_st = nb * S5_BLOCK_ST
    nch = s_dim // chunk
    ng = chunk // SUBLANES

    def body(u_ref, dy_ref, sr_ref, si_ref, ar_ref, ai_ref, wr_ref, wi_ref, cr_ref, ci_ref, d_ref,
             du_ref, dwr_ref, dwi_ref, dcr_ref, dci_ref, dar_ref, dai_ref, dd_ref,
             xr_s, xi_s, lr_s, li_s, car_r, car_i):
        first = pl.program_id(0) == 0

        @pl.when(first)
        def _():
            car_r[...] = jnp.zeros_like(car_r)
            car_i[...] = jnp.zeros_like(car_i)
            for ref in (dwr_ref, dwi_ref, dcr_ref, dci_ref, dar_ref, dai_ref, dd_ref):
                ref[...] = jnp.zeros_like(ref)

        xr_s[pl.ds(0, SUBLANES), :] = sr_ref[...]
        xi_s[pl.ds(0, SUBLANES), :] = si_ref[...]
        body_rows = pl.ds(SUBLANES, chunk)
        dyb = dy_ref[...]
        dd_ref[...] += jnp.sum(dyb * u_ref[...], axis=0, keepdims=True)
        for b in range(nb):
            ch = slice(b * S5_BLOCK_CH, (b + 1) * S5_BLOCK_CH)
            sl = slice(b * S5_BLOCK_ST, (b + 1) * S5_BLOCK_ST)
            ub = u_ref[:, ch].astype(BF16)
            dyc = dyb[:, ch].astype(BF16)
            xr_s[body_rows, sl] = _dot(ub, wr_ref[b])
            xi_s[body_rows, sl] = _dot(ub, wi_ref[b])
            lr_s[:, sl] = _dot(dyc, cr_ref[b], "nt")
            li_s[:, sl] = -_dot(dyc, ci_ref[b], "nt")
        for b in range(nb):
            sl = slice(b * S5_BLOCK_ST, (b + 1) * S5_BLOCK_ST)
            row, a1, a2, a4, table = _scan_tables(ar_ref, ai_ref, sl, False)

            def fwd_group(g, carry, sl=sl, row=row, a1=a1, a2=a2, a4=a4, table=table):
                rows = pl.ds(pl.multiple_of((g + 1) * SUBLANES, SUBLANES), SUBLANES)
                xr, xi = _scan_group(xr_s[rows, sl], xi_s[rows, sl], row, a1, a2, a4, table, carry[0], carry[1], False)
                xr_s[rows, sl] = xr
                xi_s[rows, sl] = xi
                last = (SUBLANES, S5_BLOCK_ST)
                return (jnp.broadcast_to(xr[SUBLANES - 1:, :], last), jnp.broadcast_to(xi[SUBLANES - 1:, :], last))

            lax.fori_loop(0, ng, fwd_group, (sr_ref[:, sl], si_ref[:, sl]))
            row, a1, a2, a4, table = _scan_tables(ar_ref, ai_ref, sl, True)

            def bwd_group(n, carry, sl=sl, row=row, a1=a1, a2=a2, a4=a4, table=table):
                g = ng - 1 - n
                c_r, c_i, acc_r, acc_i = carry
                rows = pl.ds(pl.multiple_of(g * SUBLANES, SUBLANES), SUBLANES)
                lr, li = _scan_group(lr_s[rows, sl], li_s[rows, sl], row, a1, a2, a4, table, c_r, c_i, True)
                lr_s[rows, sl] = lr
                li_s[rows, sl] = li
                prev = pl.ds(pl.multiple_of(g * SUBLANES, SUBLANES), SUBLANES)
                here = pl.ds(pl.multiple_of((g + 1) * SUBLANES, SUBLANES), SUBLANES)
                px_r = jnp.where(row >= 1, pltpu.roll(xr_s[here, sl], 1, 0), pltpu.roll(xr_s[prev, sl], 1, 0))
                px_i = jnp.where(row >= 1, pltpu.roll(xi_s[here, sl], 1, 0), pltpu.roll(xi_s[prev, sl], 1, 0))
                acc_r = acc_r + lr * px_r + li * px_i
                acc_i = acc_i + li * px_r - lr * px_i
                first_row = (SUBLANES, S5_BLOCK_ST)
                return (jnp.broadcast_to(lr[:1, :], first_row), jnp.broadcast_to(li[:1, :], first_row), acc_r, acc_i)

            zero = jnp.zeros((SUBLANES, S5_BLOCK_ST), F32)
            c_r, c_i, acc_r, acc_i = lax.fori_loop(0, ng, bwd_group, (car_r[:, sl], car_i[:, sl], zero, zero))
            car_r[:, sl] = c_r
            car_i[:, sl] = c_i
            dar_ref[:, sl] += acc_r
            dai_ref[:, sl] += acc_i
        for b in range(nb):
            ch = slice(b * S5_BLOCK_CH, (b + 1) * S5_BLOCK_CH)
            sl = slice(b * S5_BLOCK_ST, (b + 1) * S5_BLOCK_ST)
            ub = u_ref[:, ch].astype(BF16)
            dyc = dyb[:, ch].astype(BF16)
            lrb, lib = lr_s[:, sl].astype(BF16), li_s[:, sl].astype(BF16)
            du_ref[:, ch] = (_dot(lrb, wr_ref[b], "nt") + _dot(lib, wi_ref[b], "nt") + d_ref[:, ch] * dyb[:, ch])
            dwr_ref[b] += _dot(ub, lrb, "tn")
            dwi_ref[b] += _dot(ub, lib, "tn")
            dcr_ref[b] += _dot(xr_s[body_rows, sl].astype(BF16), dyc, "tn")
            dci_ref[b] -= _dot(xi_s[body_rows, sl].astype(BF16), dyc, "tn")

    rev = lambda i: (nch - 1 - i, 0)
    rows = pl.BlockSpec((chunk, d_dim), rev)
    st = pl.BlockSpec((SUBLANES, n_st), rev)
    vec = pl.BlockSpec((1, n_st), lambda i: (0, 0))
    w_in = pl.BlockSpec((nb, S5_BLOCK_CH, S5_BLOCK_ST), lambda i: (0, 0, 0))
    w_out = pl.BlockSpec((nb, S5_BLOCK_ST, S5_BLOCK_CH), lambda i: (0, 0, 0))
    acc8 = pl.BlockSpec((SUBLANES, n_st), lambda i: (0, 0))
    dvec = pl.BlockSpec((1, d_dim), lambda i: (0, 0))
    return pl.pallas_call(
        body, name=_name(base), grid=(nch,),
        in_specs=[rows, rows, st, st, vec, vec, w_in, w_in, w_out, w_out, dvec],
        out_specs=[rows, w_in, w_in, w_out, w_out, acc8, acc8, dvec],
        out_shape=[SDS((s_dim, d_dim), F32),
                   SDS((nb, S5_BLOCK_CH, S5_BLOCK_ST), F32), SDS((nb, S5_BLOCK_CH, S5_BLOCK_ST), F32),
                   SDS((nb, S5_BLOCK_ST, S5_BLOCK_CH), F32), SDS((nb, S5_BLOCK_ST, S5_BLOCK_CH), F32),
                   SDS((SUBLANES, n_st), F32), SDS((SUBLANES, n_st), F32), SDS((1, d_dim), F32)],
        scratch_shapes=[pltpu.VMEM((chunk + SUBLANES, n_st), F32), pltpu.VMEM((chunk + SUBLANES, n_st), F32),
                        pltpu.VMEM((chunk, n_st), F32), pltpu.VMEM((chunk, n_st), F32),
                        pltpu.VMEM((SUBLANES, n_st), F32), pltpu.VMEM((SUBLANES, n_st), F32)],
        compiler_params=_params(("arbitrary",)),
    )(u, dy, st_r, st_i, ar, ai, wr, wi, cr_w, ci_w, dsk)


def _block_diag(w, transpose):
    g = w.shape[0]
    nb = g // S5_BLOCK_GROUPS
    eye = jnp.eye(S5_BLOCK_GROUPS, dtype=w.dtype)
    if transpose:
        w = w.reshape(nb, S5_BLOCK_GROUPS, S5_GROUP, S5_STATE)
        return jnp.einsum("bgcp,gh->bgphc", w, eye).reshape(nb, S5_BLOCK_ST, S5_BLOCK_CH)
    w = w.reshape(nb, S5_BLOCK_GROUPS, S5_STATE, S5_GROUP)
    return jnp.einsum("bgpc,gh->bgchp", w, eye).reshape(nb, S5_BLOCK_CH, S5_BLOCK_ST)


def _block_diag_extract(w, transpose):
    nb = w.shape[0]
    eye = jnp.eye(S5_BLOCK_GROUPS, dtype=w.dtype)
    if transpose:
        w = w.reshape(nb, S5_BLOCK_GROUPS, S5_STATE, S5_BLOCK_GROUPS, S5_GROUP)
        return jnp.einsum("bgphc,gh->bgcp", w, eye).reshape(nb * S5_BLOCK_GROUPS, S5_GROUP, S5_STATE)
    w = w.reshape(nb, S5_BLOCK_GROUPS, S5_GROUP, S5_BLOCK_GROUPS, S5_STATE)
    return jnp.einsum("bgchp,gh->bgpc", w, eye).reshape(nb * S5_BLOCK_GROUPS, S5_STATE, S5_GROUP)


def _dwconv_fwd(h, w, b, base, tr=512):
    s_dim, d_dim = h.shape
    tr = _tile(s_dim, tr, CONV_HALO)
    per = tr // CONV_HALO

    def body(h_ref, halo_ref, w_ref, b_ref, o_ref, buf):
        i = pl.program_id(0)
        buf[pl.ds(0, CONV_HALO), :] = jnp.where(i > 0, halo_ref[...], 0.0)
        buf[pl.ds(CONV_HALO, tr), :] = h_ref[...]
        acc = jnp.broadcast_to(b_ref[...], (tr, d_dim))
        for k in range(CONV_WIDTH):
            acc = acc + w_ref[pl.ds(k, 1), :] * buf[pl.ds(CONV_HALO - (CONV_WIDTH - 1) + k, tr), :]
        o_ref[...] = acc

    return pl.pallas_call(
        body, name=_name(base), grid=(s_dim // tr,),
        in_specs=[pl.BlockSpec((tr, d_dim), lambda i: (i, 0)),
                  pl.BlockSpec((CONV_HALO, d_dim), lambda i: (jnp.maximum(i * per - 1, 0), 0)),
                  pl.BlockSpec((CONV_WIDTH, d_dim), lambda i: (0, 0)), pl.BlockSpec((1, d_dim), lambda i: (0, 0))],
        out_specs=pl.BlockSpec((tr, d_dim), lambda i: (i, 0)), out_shape=SDS((s_dim, d_dim), F32),
        scratch_shapes=[pltpu.VMEM((tr + CONV_HALO, d_dim), F32)], compiler_params=_params(("arbitrary",)),
    )(h, h, w, b)


def _dwconv_bwd(h, dout, w, base, tr=512):
    s_dim, d_dim = h.shape
    tr = _tile(s_dim, tr, CONV_HALO)
    per = tr // CONV_HALO
    n = s_dim // tr
    last_halo = s_dim // CONV_HALO - 1

    def body(h_ref, hhalo_ref, d_ref, dhalo_ref, w_ref, dh_ref, dw_ref, db_ref, hbuf, dbuf):
        i = pl.program_id(0)
        hbuf[pl.ds(0, CONV_HALO), :] = jnp.where(i > 0, hhalo_ref[...], 0.0)
        hbuf[pl.ds(CONV_HALO, tr), :] = h_ref[...]
        dbuf[pl.ds(0, tr), :] = d_ref[...]
        dbuf[pl.ds(tr, CONV_HALO), :] = jnp.where(i < n - 1, dhalo_ref[...], 0.0)
        dout_t = d_ref[...]
        acc = jnp.zeros((tr, d_dim), F32)
        dws = []
        for k in range(CONV_WIDTH):
            acc = acc + w_ref[pl.ds(k, 1), :] * dbuf[pl.ds(CONV_WIDTH - 1 - k, tr), :]
            shifted = hbuf[pl.ds(CONV_HALO - (CONV_WIDTH - 1) + k, tr), :]
            dws.append(jnp.sum(dout_t * shifted, axis=0, keepdims=True))
        dh_ref[...] = acc
        dw_t = jnp.concatenate(dws, axis=0)
        db_t = jnp.sum(dout_t, axis=0, keepdims=True)

        @pl.when(i == 0)
        def _():
            dw_ref[...] = dw_t
            db_ref[...] = db_t

        @pl.when(i > 0)
        def _():
            dw_ref[...] += dw_t
            db_ref[...] += db_t

    tile = pl.BlockSpec((tr, d_dim), lambda i: (i, 0))
    before = pl.BlockSpec((CONV_HALO, d_dim), lambda i: (jnp.maximum(i * per - 1, 0), 0))
    after = pl.BlockSpec((CONV_HALO, d_dim), lambda i: (jnp.minimum((i + 1) * per, last_halo), 0))
    return pl.pallas_call(
        body, name=_name(base), grid=(n,),
        in_specs=[tile, before, tile, after, pl.BlockSpec((CONV_WIDTH, d_dim), lambda i: (0, 0))],
        out_specs=[tile, pl.BlockSpec((CONV_WIDTH, d_dim), lambda i: (0, 0)), pl.BlockSpec((1, d_dim), lambda i: (0, 0))],
        out_shape=[SDS((s_dim, d_dim), F32), SDS((CONV_WIDTH, d_dim), F32), SDS((1, d_dim), F32)],
        scratch_shapes=[pltpu.VMEM((tr + CONV_HALO, d_dim), F32), pltpu.VMEM((tr + CONV_HALO, d_dim), F32)],
        compiler_params=_params(("arbitrary",)),
    )(h, h, dout, dout, w)


def _local_step(x, mod, w, target):
    s_dim, d_dim = x.shape
    depth = mod.shape[0]
    row = lambda dt=F32, n=d_dim: SDS((s_dim, n), dt)
    vec = lambda n=d_dim: SDS((1, n), F32)
    mods = [[mod[l:l + 1, k * d_dim:(k + 1) * d_dim] for k in range(6)] for l in range(depth)]
    ng = lambda l, k: w["norm_g"][l, k:k + 1, :]
    saved = []
    h = x
    for l in range(depth):
        sh_m, sc_m, g_m, sh_f, sc_f, g_f = mods[l]
        kind, j = l % 3, l // 3
        s = {"h0": h}
        if kind == 1:
            (u,) = _rowwise(lambda h, g, a, b: ((_pre(h, g, a, b),), ()), [h], [ng(l, 0), sh_m, sc_m], [row()], [], "pre")
        else:
            (u,) = _rowwise(lambda h, g, a, b: ((_pre(h, g, a, b),), ()), [h], [ng(l, 0), sh_m, sc_m], [row(BF16)], [], "pre")
        s["u"] = u
        bias = None
        if kind == 0:
            qkv = _matmul(u, w["sb_w_qkv"][j], "nn", BF16, "qkv")
            o = _sb_fwd(qkv, "sb_fwd")
            m = _matmul(o, w["sb_w_o"][j], "nn", F32, "wo")
            s.update(qkv=qkv, o=o)
        elif kind == 1:
            p = w["s5"][j]
            y_pre, gy, st_r, st_i = _s5_fwd(u, p["ar"], p["ai"], p["wr"], p["wi"], p["cr"], p["ci"], w["s5_d"][j:j + 1],
                                            "s5_fwd")
            t = _matmul(gy, w["s5_w_glu"][j], "nn", F32, "s5_glu")
            b_glu = w["s5_b_glu"][j:j + 1]
            (m,) = _rowwise(lambda t, b: ((_glu(t, b),), ()), [t], [b_glu], [row()], [], "s5_gate")
            s.update(y_pre=y_pre, gy=gy, st_r=st_r, st_i=st_i, t=t)
        else:
            t = _matmul(u, w["cv_w_pw1"][j], "nn", F32, "pw1")
            (hg,) = _rowwise(lambda t, b: ((_glu(t, b),), ()), [t], [w["cv_b_pw1"][j:j + 1]], [row()], [], "cv_glu")
            hc = _dwconv_fwd(hg, w["cv_w_dw"][j], w["cv_b_dw"][j:j + 1], "dwconv")
            (hs,) = _rowwise(lambda h, g, b: ((_ln_silu(h, g, b),), ()), [hc],
                             [w["cv_ln_g"][j:j + 1], w["cv_ln_b"][j:j + 1]], [row(BF16)], [], "cv_ln")
            m = _matmul(hs, w["cv_w_pw2"][j], "nn", F32, "pw2")
            bias = w["cv_b_pw2"][j:j + 1]
            s.update(t=t, hg=hg, hc=hc, hs=hs)
        s["m"] = m
        if bias is None:
            h1, u2 = _rowwise(
                lambda h, m, gate, g1, g2, a, b: ((lambda h1: (h1, _pre(h1, g2, a, b)))(_post(h, m, gate, g1)), ()),
                [h, m], [g_m, ng(l, 1), ng(l, 2), sh_f, sc_f], [row(), row(BF16)], [], "post_pre")
        else:
            h1, u2 = _rowwise(
                lambda h, m, gate, g1, g2, a, b, bb: ((lambda h1: (h1, _pre(h1, g2, a, b)))(_post(h, m + bb, gate, g1)), ()),
                [h, m], [g_m, ng(l, 1), ng(l, 2), sh_f, sc_f, bias], [row(), row(BF16)], [], "post_pre")
        f_dim = w["ffn_w_gu"].shape[2] // 2
        gu = _matmul(u2, w["ffn_w_gu"][l], "nn", BF16, "ffn_gu")
        (act,) = _rowwise(lambda gu: ((_swish_gate(_f(gu)),), ()), [gu], [], [row(BF16, f_dim)], [], "ffn_act")
        f = _matmul(act, w["ffn_w_down"][l], "nn", F32, "ffn_down")
        (h,) = _rowwise(lambda h, m, gate, g: ((_post(h, m, gate, g),), ()), [h1, f], [g_f, ng(l, 3)], [row()], [], "post")
        s.update(h1=h1, u2=u2, gu=gu, act=act, f=f, bias=bias)
        saved.append(s)
    dh, loss_cols = _rowwise(
        lambda y, t: ((((y - t) * (1.0 / d_dim)),), (jnp.sum(jnp.square(y - t), axis=0, keepdims=True),)),
        [h, target], [], [row()], [vec()], "loss")
    grads = {}
    dmod = [[None] * 6 for _ in range(depth)]
    dnorm = [[None] * 4 for _ in range(depth)]

    def add(name, idx, val):
        grads.setdefault(name, {})[idx] = val

    for l in reversed(range(depth)):
        sh_m, sc_m, g_m, sh_f, sc_f, g_f = mods[l]
        kind, j = l % 3, l // 3
        s = saved[l]

        def post_bwd(h, m, dh, gate, g):
            _, vjp = jax.vjp(_post, h, m, gate, g)
            _, dm, dgate, dg = vjp(dh)
            return (dm,), (dgate, dg)

        df, dmod[l][5], dnorm[l][3] = _rowwise(post_bwd, [s["h1"], s["f"], dh], [g_f, ng(l, 3)],
                                               [row(BF16)], [vec(), vec()], "post_bwd")
        f_dim = s["act"].shape[1]
        add("ffn_w_down", l, _matmul(s["act"], df, "tn", F32, "d_down"))
        dact = _matmul(df, w["ffn_w_down"][l], "nt", BF16, "d_act")

        def act_bwd(gu, da):
            _, vjp = jax.vjp(_swish2, *_halves(_f(gu)))
            return (jnp.concatenate(vjp(_f(da)), axis=1),), ()

        (dgu,) = _rowwise(act_bwd, [s["gu"], dact], [], [row(BF16, 2 * f_dim)], [], "act_bwd")
        add("ffn_w_gu", l, _matmul(s["u2"], dgu, "tn", F32, "d_gu"))
        du2 = _matmul(dgu, w["ffn_w_gu"][l], "nt", F32, "d_u2")
        bias = s["bias"]

        def post_pre_bwd(h, m, dh1, du2, gate, g1, g2, a, b, *bb):
            def fwd(h, m, gate, g1, g2, a, b):
                mm = m + bb[0] if bb else m
                h1 = _post(h, mm, gate, g1)
                return h1, _pre(h1, g2, a, b)
            _, vjp = jax.vjp(fwd, h, m, gate, g1, g2, a, b)
            dh0, dm, dgate, dg1, dg2, da, db = vjp((dh1, du2))
            return (dh0, dm), (dgate, dg1, dg2, da, db)

        vecs = [g_m, ng(l, 1), ng(l, 2), sh_f, sc_f] + ([bias] if bias is not None else [])
        dh, dm, dmod[l][2], dnorm[l][1], dnorm[l][2], dmod[l][3], dmod[l][4] = _rowwise(
            post_pre_bwd, [s["h0"], s["m"], dh, du2], vecs, [row(), row()], [vec()] * 5, "post_pre_bwd")
        if kind == 0:
            (dmb,) = _rowwise(lambda a: ((a,), ()), [dm], [], [row(BF16)], [], "cast")
            add("sb_w_o", j, _matmul(s["o"], dmb, "tn", F32, "d_wo"))
            do = _matmul(dmb, w["sb_w_o"][j], "nt", BF16, "d_o")
            dqkv = jnp.concatenate(_sb_bwd(s["qkv"], do, "sb_bwd"), axis=1).astype(BF16)
            add("sb_w_qkv", j, _matmul(s["u"], dqkv, "tn", F32, "d_wqkv"))
            du = _matmul(dqkv, w["sb_w_qkv"][j], "nt", F32, "d_u")
        elif kind == 1:
            p = w["s5"][j]
            b_glu = w["s5_b_glu"][j:j + 1]

            def gate_bwd(t, dm, b):
                dt, db = _glu_bwd(t, dm, b)
                return (dt,), (db,)

            dt, db_glu = _rowwise(gate_bwd, [s["t"], dm], [b_glu], [row(BF16, 2 * d_dim)], [vec(2 * d_dim)], "s5_gate_bwd")
            add("s5_b_glu", j, db_glu)
            add("s5_w_glu", j, _matmul(s["gy"], dt, "tn", F32, "d_wglu"))
            dgy = _matmul(dt, w["s5_w_glu"][j], "nt", F32, "d_gy")

            def gelu_bwd(y, dg):
                _, vjp = jax.vjp(jax.nn.gelu, y)
                return (vjp(dg)[0],), ()

            (dy,) = _rowwise(gelu_bwd, [s["y_pre"], dgy], [], [row()], [], "gelu_bwd")
            chunk = s_dim // (s["st_r"].shape[0] // SUBLANES)
            du, dwr, dwi, dcr, dci, dar, dai, dd = _s5_bwd(
                s["u"], dy, s["st_r"], s["st_i"], p["ar"], p["ai"], p["wr"], p["wi"], p["cr"], p["ci"],
                w["s5_d"][j:j + 1], "s5_bwd", chunk)
            add("s5_d", j, dd)
            add("s5_raw", j, (dwr, dwi, dcr, dci, dar, dai))
        else:
            (dmb,) = _rowwise(lambda a: ((a,), ()), [dm], [], [row(BF16)], [], "cast")
            add("cv_w_pw2", j, _matmul(s["hs"], dmb, "tn", F32, "d_pw2"))
            dhs = _matmul(dmb, w["cv_w_pw2"][j], "nt", F32, "d_hs")
            ln_g, ln_b = w["cv_ln_g"][j:j + 1], w["cv_ln_b"][j:j + 1]

            def ln_bwd(hc, dhs, dm, g, b):
                _, vjp = jax.vjp(_ln_silu, hc, g, b)
                dhc, dg, db = vjp(dhs)
                return (dhc,), (dg, db, jnp.sum(dm, axis=0, keepdims=True))

            dhc, dln_g, dln_b, db_pw2 = _rowwise(ln_bwd, [s["hc"], dhs, dm], [ln_g, ln_b], [row()], [vec()] * 3, "ln_bwd")
            add("cv_ln_g", j, dln_g)
            add("cv_ln_b", j, dln_b)
            add("cv_b_pw2", j, db_pw2)
            dhg, dw_dw, db_dw = _dwconv_bwd(s["hg"], dhc, w["cv_w_dw"][j], "dwconv_bwd")
            add("cv_w_dw", j, dw_dw)
            add("cv_b_dw", j, db_dw)
            b_pw1 = w["cv_b_pw1"][j:j + 1]

            def glu_bwd(t, dhg, b):
                dt, db = _glu_bwd(t, dhg, b)
                return (dt,), (db,)

            dt, db_pw1 = _rowwise(glu_bwd, [s["t"], dhg], [b_pw1], [row(BF16, 2 * d_dim)], [vec(2 * d_dim)], "cv_glu_bwd")
            add("cv_b_pw1", j, db_pw1)
            add("cv_w_pw1", j, _matmul(s["u"], dt, "tn", F32, "d_pw1"))
            du = _matmul(dt, w["cv_w_pw1"][j], "nt", F32, "d_u")

        def pre_bwd(h, dh, du, g, a, b):
            _, vjp = jax.vjp(_pre, h, g, a, b)
            dh0, dg, da, db = vjp(du)
            return (dh + dh0,), (dg, da, db)

        dh, dnorm[l][0], dmod[l][0], dmod[l][1] = _rowwise(pre_bwd, [s["h0"], dh, du], [ng(l, 0), sh_m, sc_m],
                                                            [row()], [vec()] * 3, "pre_bwd")
    grads["norm_g"] = jnp.stack([jnp.concatenate(r, axis=0) for r in dnorm])
    dmod_arr = jnp.concatenate([jnp.concatenate(r, axis=1) for r in dmod], axis=0)
    return loss_cols, dh, dmod_arr, grads


def _position():
    return lax.axis_index("x"), lax.axis_index("y"), lax.axis_index("c")


def _all_gather8(x, base):
    m, n = x.shape
    space = pltpu.VMEM

    def body(x_ref, out_ref, send_sems, recv_sems, local_sem):
        px, py, pc = _position()
        me, sibling = (px, py, pc), (px, py, 1 - pc)
        chips = [(1 - px, py), (px, 1 - py), (1 - px, 1 - py)]
        src_mine = x_ref

        def rows(bx, by, bc):
            return out_ref.at[pl.ds((4 * bx + 2 * by + bc) * m, m), :]

        def copy(k, block, to, src=None):
            return pltpu.make_async_remote_copy(
                src_ref=rows(*block) if src is None else src, dst_ref=rows(*block),
                send_sem=send_sems.at[k], recv_sem=recv_sems.at[k], device_id=to, device_id_type=MESH)

        mine = pltpu.make_async_copy(src_mine, rows(*me), local_sem)
        mine.start()
        first = [copy(0, me, sibling, src=src_mine)]
        first += [copy(1 + j, me, (*chip, pc), src=src_mine) for j, chip in enumerate(chips)]
        for cp in first:
            cp.start()
        passed = [copy(4 + j, (*chip, pc), sibling) for j, chip in enumerate(chips)]
        for j, chip in enumerate(chips):
            copy(1 + j, (*chip, pc), me).wait_recv()
            passed[j].start()
        copy(0, sibling, me).wait_recv()
        for j, chip in enumerate(chips):
            copy(4 + j, (*chip, 1 - pc), me).wait_recv()
        for cp in first + passed:
            cp.wait_send()
        mine.wait()

    return pl.pallas_call(
        body, name=_name(base), out_shape=SDS((8 * m, n), x.dtype),
        in_specs=[pl.BlockSpec(memory_space=space)], out_specs=pl.BlockSpec(memory_space=space),
        scratch_shapes=[pltpu.SemaphoreType.DMA((7,)), pltpu.SemaphoreType.DMA((7,)), pltpu.SemaphoreType.DMA],
    )(x)


_ANY = pl.BlockSpec(memory_space=pl.ANY)


def _half(ref, core, lead):
    k_half = ref.shape[lead] // 2
    return ref.at[(slice(None),) * lead + (pl.ds(core * k_half, k_half), slice(None))]


def _gather_weights(shards, base):
    nw = len(shards)

    def body(*refs):
        x_refs, out_refs = refs[:nw], refs[nw:2 * nw]
        send_sems, recv_sems, local_sems = refs[2 * nw:]
        px, py, pc = _position()
        me, sibling = (px, py, pc), (px, py, 1 - pc)
        chips = [(1 - px, py), (px, 1 - py), (1 - px, 1 - py)]

        def block(i, bx, by, bc):
            return _half(out_refs[i].at[2 * bx + by], bc, 1)

        def copy(i, k, blk, to, src=None):
            dst = block(i, *blk)
            return pltpu.make_async_remote_copy(
                src_ref=dst if src is None else src, dst_ref=dst, send_sem=send_sems.at[7 * i + k],
                recv_sem=recv_sems.at[7 * i + k], device_id=to, device_id_type=MESH)

        mine = [_half(x_refs[i], pc, 1) for i in range(nw)]
        local = [pltpu.make_async_copy(mine[i], block(i, *me), local_sems.at[i]) for i in range(nw)]
        for cp in local:
            cp.start()
        first = []
        for i in range(nw):
            first.append(copy(i, 0, me, sibling, src=mine[i]))
            first += [copy(i, 1 + j, me, (*chip, pc), src=mine[i]) for j, chip in enumerate(chips)]
        for cp in first:
            cp.start()
        passed = []
        for j, chip in enumerate(chips):
            for i in range(nw):
                copy(i, 1 + j, (*chip, pc), me).wait_recv()
                passed.append(copy(i, 4 + j, (*chip, pc), sibling))
                passed[-1].start()
        for i in range(nw):
            copy(i, 0, sibling, me).wait_recv()
        for j, chip in enumerate(chips):
            for i in range(nw):
                copy(i, 4 + j, (*chip, 1 - pc), me).wait_recv()
        for cp in first + passed:
            cp.wait_send()
        for cp in local:
            cp.wait()

    return pl.pallas_call(
        body, name=_name(base), out_shape=[SDS((4,) + s.shape, s.dtype) for s in shards],
        in_specs=[_ANY] * nw, out_specs=[_ANY] * nw,
        scratch_shapes=[pltpu.SemaphoreType.DMA((7 * nw,)), pltpu.SemaphoreType.DMA((7 * nw,)),
                        pltpu.SemaphoreType.DMA((nw,))],
    )(*shards)


def _reduce_to_sibling(gs, base):
    nw = len(gs)

    def body(*refs):
        g_refs, recv_refs, send_sems, recv_sems = refs[:nw], refs[nw:2 * nw], refs[2 * nw], refs[2 * nw + 1]
        px, py, pc = _position()
        copies = [pltpu.make_async_remote_copy(
            src_ref=_half(g_refs[i], 1 - pc, 2), dst_ref=recv_refs[i], send_sem=send_sems.at[i],
            recv_sem=recv_sems.at[i], device_id=(px, py, 1 - pc), device_id_type=MESH) for i in range(nw)]
        for cp in copies:
            cp.start()
        for cp in copies:
            cp.wait_recv()
        for cp in copies:
            cp.wait_send()

    outs = [SDS(g.shape[:2] + (g.shape[2] // 2, g.shape[3]), g.dtype) for g in gs]
    return pl.pallas_call(
        body, name=_name(base), out_shape=outs, in_specs=[_ANY] * nw, out_specs=[_ANY] * nw,
        scratch_shapes=[pltpu.SemaphoreType.DMA((nw,)), pltpu.SemaphoreType.DMA((nw,))],
    )(*gs)


def _exchange_chips(ts, base):
    nw = len(ts)

    def body(*refs):
        t_refs, recv_refs, send_sems, recv_sems = refs[:nw], refs[nw:2 * nw], refs[2 * nw], refs[2 * nw + 1]
        px, py, pc = _position()
        chips = [(1 - px, py), (px, 1 - py), (1 - px, 1 - py)]
        copies = [pltpu.make_async_remote_copy(
            src_ref=t_refs[i].at[2 * cx + cy], dst_ref=recv_refs[i].at[k], send_sem=send_sems.at[3 * i + k],
            recv_sem=recv_sems.at[3 * i + k], device_id=(cx, cy, pc), device_id_type=MESH)
            for k, (cx, cy) in enumerate(chips) for i in range(nw)]
        for cp in copies:
            cp.start()
        for cp in copies:
            cp.wait_recv()
        for cp in copies:
            cp.wait_send()

    outs = [SDS((3,) + t.shape[1:], t.dtype) for t in ts]
    return pl.pallas_call(
        body, name=_name(base), out_shape=outs, in_specs=[_ANY] * nw, out_specs=[_ANY] * nw,
        scratch_shapes=[pltpu.SemaphoreType.DMA((3 * nw,)), pltpu.SemaphoreType.DMA((3 * nw,))],
    )(*ts)


def _share_with_sibling(rs, base):
    nw = len(rs)

    def body(*refs):
        r_refs, out_refs = refs[:nw], refs[nw:2 * nw]
        send_sems, recv_sems, local_sems = refs[2 * nw:]
        px, py, pc = _position()
        local = [pltpu.make_async_copy(r_refs[i], _half(out_refs[i], pc, 1), local_sems.at[i]) for i in range(nw)]
        send = [pltpu.make_async_remote_copy(
            src_ref=r_refs[i], dst_ref=_half(out_refs[i], pc, 1), send_sem=send_sems.at[i], recv_sem=recv_sems.at[i],
            device_id=(px, py, 1 - pc), device_id_type=MESH) for i in range(nw)]
        for cp in local + send:
            cp.start()
        for i in range(nw):
            pltpu.make_async_remote_copy(
                src_ref=r_refs[i], dst_ref=_half(out_refs[i], 1 - pc, 1), send_sem=send_sems.at[i],
                recv_sem=recv_sems.at[i], device_id=(px, py, 1 - pc), device_id_type=MESH).wait_recv()
        for cp in send:
            cp.wait_send()
        for cp in local:
            cp.wait()

    outs = [SDS((r.shape[0], 2 * r.shape[1], r.shape[2]), r.dtype) for r in rs]
    return pl.pallas_call(
        body, name=_name(base), out_shape=outs, in_specs=[_ANY] * nw, out_specs=[_ANY] * nw,
        scratch_shapes=[pltpu.SemaphoreType.DMA((nw,)), pltpu.SemaphoreType.DMA((nw,)), pltpu.SemaphoreType.DMA((nw,))],
    )(*rs)


def _add_own_half(g, recv, core, base, tr=512):
    _, l_dim, k_half, n = recv.shape
    tr = _tile(k_half, tr, 2 * SUBLANES)
    per = k_half // tr

    def body(core_ref, g_ref, r_ref, o_ref):
        o_ref[...] = (g_ref[...] + r_ref[...]).astype(o_ref.dtype)

    blk = (1, 1, tr, n)
    spec = pltpu.PrefetchScalarGridSpec(
        num_scalar_prefetch=1, grid=(4, l_dim, per),
        in_specs=[pl.BlockSpec(blk, lambda j, l, i, core: (j, l, core[0] * per + i, 0)),
                  pl.BlockSpec(blk, lambda j, l, i, core: (j, l, i, 0))],
        out_specs=pl.BlockSpec(blk, lambda j, l, i, core: (j, l, i, 0)))
    return pl.pallas_call(body, name=_name(base), grid_spec=spec, out_shape=SDS(recv.shape, BF16),
                          compiler_params=_params(("parallel", "parallel", "parallel")))(core, g, recv)


def _add_chips(g, from_sibling, recv, chip_core, base, tr=512):
    _, l_dim, k_half, n = from_sibling.shape
    tr = _tile(k_half, tr, 2 * SUBLANES)
    per = k_half // tr

    def body(idx_ref, g_ref, s_ref, r_ref, o_ref):
        own = g_ref[0, 0] + s_ref[0, 0]
        o_ref[0] = ((own + _f(r_ref[0, 0])) + _f(r_ref[1, 0])) + _f(r_ref[2, 0])

    spec = pltpu.PrefetchScalarGridSpec(
        num_scalar_prefetch=1, grid=(l_dim, per),
        in_specs=[pl.BlockSpec((1, 1, tr, n), lambda l, i, idx: (idx[0], l, idx[1] * per + i, 0)),
                  pl.BlockSpec((1, 1, tr, n), lambda l, i, idx: (idx[0], l, i, 0)),
                  pl.BlockSpec((3, 1, tr, n), lambda l, i, idx: (0, l, i, 0))],
        out_specs=pl.BlockSpec((1, tr, n), lambda l, i, idx: (l, i, 0)))
    return pl.pallas_call(body, name=_name(base), grid_spec=spec, out_shape=SDS((l_dim, k_half, n), F32),
                          compiler_params=_params(("parallel", "parallel")))(chip_core, g, from_sibling, recv)


def _sum8(g, base):
    r, n = g.shape[0] // 8, g.shape[1]

    def body(g_ref, o_ref):
        acc = g_ref[pl.ds(0, r), :]
        for d in range(1, 8):
            acc = acc + g_ref[pl.ds(d * r, r), :]
        o_ref[...] = acc

    return pl.pallas_call(body, name=_name(base), out_shape=SDS((r, n), F32), compiler_params=_params())(g)


def _adamw_math(w, g, m, v):
    m = ADAM_B1 * m + (1.0 - ADAM_B1) * g
    v = ADAM_B2 * v + (1.0 - ADAM_B2) * jnp.square(g)
    m_hat = m / (1.0 - ADAM_B1 ** ADAM_STEP)
    v_hat = v / (1.0 - ADAM_B2 ** ADAM_STEP)
    delta = -ADAM_LR * (m_hat / (jnp.sqrt(v_hat) + ADAM_EPS) + ADAM_WD * w)
    return delta, m, v


def _adamw(w, g, m, v, base):
    out = SDS(w.shape, F32)
    return _rowwise(lambda w, g, m, v: (_adamw_math(w, g, m, v), ()), [w, g, m, v], [], [out, out, out], [], base)


def _mod_grad_adamw(sc_t, dmod, w, m, v, base, tr=256):
    l_dim, d_dim, n = w.shape
    tr = _tile(d_dim, tr, SUBLANES)

    def body(sc_ref, dm_ref, w_ref, m_ref, v_ref, g_ref, d_ref, nm_ref, nv_ref):
        g = lax.dot_general(sc_ref[...], dm_ref[0], _DIMS["nn"], preferred_element_type=F32,
                            precision=lax.Precision.HIGHEST)
        delta, nm, nv = _adamw_math(w_ref[0], g, m_ref[0], v_ref[0])
        g_ref[0], d_ref[0], nm_ref[0], nv_ref[0] = g, delta, nm, nv

    tile = pl.BlockSpec((1, tr, n), lambda l, i: (l, i, 0))
    out = SDS(w.shape, F32)
    return pl.pallas_call(
        body, name=_name(base), grid=(l_dim, d_dim // tr),
        in_specs=[pl.BlockSpec((tr, 8), lambda l, i: (i, 0)), pl.BlockSpec((1, 8, n), lambda l, i: (l, 0, 0)),
                  tile, tile, tile],
        out_specs=[tile] * 4, out_shape=[out] * 4, compiler_params=_params(("parallel", "parallel")),
    )(sc_t, dmod, w, m, v)


def _mod_forward(c_all, w_mod, b_mod, base):
    l_dim, d_dim, n = w_mod.shape

    def body(c_ref, w_ref, b_ref, o_ref):
        o_ref[0] = lax.dot_general(jax.nn.silu(c_ref[...]), w_ref[0], _DIMS["nn"], preferred_element_type=F32,
                                   precision=lax.Precision.HIGHEST) + b_ref[0]

    return pl.pallas_call(
        body, name=_name(base), grid=(l_dim,),
        in_specs=[pl.BlockSpec((8, d_dim), lambda l: (0, 0)), pl.BlockSpec((1, d_dim, n), lambda l: (l, 0, 0)),
                  pl.BlockSpec((1, 1, n), lambda l: (l, 0, 0))],
        out_specs=pl.BlockSpec((1, 8, n), lambda l: (l, 0, 0)), out_shape=SDS((l_dim, 8, n), F32),
        compiler_params=_params(("parallel",)),
    )(c_all, w_mod, b_mod)


def _pack(arrays, dtype, row_mult):
    flat = jnp.concatenate([a.reshape(-1).astype(dtype) for a in arrays])
    quantum = PACK_LANES * row_mult
    pad = (-flat.shape[0]) % quantum
    return jnp.pad(flat, (0, pad)).reshape(-1, PACK_LANES)


def _unpack(flat, shapes):
    out, off = [], 0
    for shp in shapes:
        n = math.prod(shp)
        out.append(flat[off:off + n].reshape(shp))
        off += n
    return out


_WEIGHTS = ["norm_g", "w_mod", "b_mod", "sb_w_qkv", "sb_w_o", "s5_lam_re", "s5_lam_im", "s5_log_dt", "s5_b_re",
            "s5_b_im", "s5_c_re", "s5_c_im", "s5_d", "s5_w_glu", "s5_b_glu", "cv_w_pw1", "cv_b_pw1", "cv_w_dw",
            "cv_b_dw", "cv_ln_g", "cv_ln_b", "cv_w_pw2", "cv_b_pw2", "ffn_w_gate", "ffn_w_up", "ffn_w_down"]
_BIG_COL = ["sb_w_qkv", "s5_w_glu", "cv_w_pw1", "ffn_w_gate", "ffn_w_up"]
_BIG_ROW = ["sb_w_o", "cv_w_pw2", "ffn_w_down"]
_BIG = _BIG_COL + _BIG_ROW
_SMALL_SHARDED = {"norm_g": 2, "cv_b_pw1": 1, "cv_w_dw": 2, "cv_b_dw": 1, "cv_ln_g": 1, "cv_ln_b": 1, "cv_b_pw2": 1}
_REPLICATED = ["b_mod", "s5_lam_re", "s5_lam_im", "s5_log_dt", "s5_b_re", "s5_b_im", "s5_c_re", "s5_c_im", "s5_d",
               "s5_b_glu"]


def _full_from_shards(stacked, name):
    if name in _BIG_COL or (name in _SMALL_SHARDED):
        axis = stacked.ndim - 1
    else:
        axis = stacked.ndim - 2
    moved = jnp.moveaxis(stacked, 0, axis - 1)
    shp = list(moved.shape)
    shp[axis - 1:axis + 1] = [shp[axis - 1] * shp[axis]]
    return moved.reshape(shp)


def _shards_from_full(full, name):
    axis = full.ndim - 1 if (name in _BIG_COL or name in _SMALL_SHARDED) else full.ndim - 2
    shp = list(full.shape)
    shp[axis:axis + 1] = [4, shp[axis] // 4]
    return jnp.moveaxis(full.reshape(shp), axis, 0)


def kernel(x, c, norm_g, w_mod, b_mod, sb_w_qkv, sb_w_o, s5_lam_re, s5_lam_im, s5_log_dt, s5_b_re, s5_b_im, s5_c_re, s5_c_im, s5_d, s5_w_glu, s5_b_glu, cv_w_pw1, cv_b_pw1, cv_w_dw, cv_b_dw, cv_ln_g, cv_ln_b, cv_w_pw2, cv_b_pw2, ffn_w_gate, ffn_w_up, ffn_w_down, loss_target, m_norm_g, m_w_mod, m_b_mod, m_sb_w_qkv, m_sb_w_o, m_s5_lam_re, m_s5_lam_im, m_s5_log_dt, m_s5_b_re, m_s5_b_im, m_s5_c_re, m_s5_c_im, m_s5_d, m_s5_w_glu, m_s5_b_glu, m_cv_w_pw1, m_cv_b_pw1, m_cv_w_dw, m_cv_b_dw, m_cv_ln_g, m_cv_ln_b, m_cv_w_pw2, m_cv_b_pw2, m_ffn_w_gate, m_ffn_w_up, m_ffn_w_down, v_norm_g, v_w_mod, v_b_mod, v_sb_w_qkv, v_sb_w_o, v_s5_lam_re, v_s5_lam_im, v_s5_log_dt, v_s5_b_re, v_s5_b_im, v_s5_c_re, v_s5_c_im, v_s5_d, v_s5_w_glu, v_s5_b_glu, v_cv_w_pw1, v_cv_b_pw1, v_cv_w_dw, v_cv_b_dw, v_cv_ln_g, v_cv_ln_b, v_cv_w_pw2, v_cv_b_pw2, v_ffn_w_gate, v_ffn_w_up, v_ffn_w_down):
    args = locals()
    wts = {n: args[n] for n in _WEIGHTS}
    mom = {n: args["m_" + n] for n in _WEIGHTS}
    var = {n: args["v_" + n] for n in _WEIGHTS}
    px, py, pc = lax.axis_index("x"), lax.axis_index("y"), lax.axis_index("c")
    chip = 2 * px + py
    dev = 2 * chip + pc
    s_dim, d_dim = x.shape[1], x.shape[2]
    depth = norm_g.shape[0]
    x2, tgt = x[0], loss_target[0]

    small_names = list(_SMALL_SHARDED)
    small_in = _pack([c] + [wts[n] for n in small_names], F32, SUBLANES)
    small_all = _all_gather8(small_in, "gather_small").reshape(8, -1)
    c_all = small_all[:, :d_dim]
    full = {}
    off = d_dim
    for n in small_names:
        size = math.prod(wts[n].shape)
        stacked = small_all[0::2, off:off + size].reshape((4,) + wts[n].shape)
        full[n] = _full_from_shards(stacked, n)
        off += size

    n_mod = w_mod.shape[2]
    b_mod_cols = lax.dynamic_slice_in_dim(b_mod, chip * n_mod, n_mod, axis=1)[:, None, :]
    mod_part = _mod_forward(c_all, w_mod, b_mod_cols, "mod_fwd")
    mod_all = _all_gather8(mod_part.reshape(depth * 8, n_mod), "gather_mod")
    mod_all = mod_all.reshape(4, 2, depth, 8, n_mod)[:, 0]
    mod_mine = lax.dynamic_index_in_dim(mod_all, dev, axis=2, keepdims=False)
    mod = mod_mine.transpose(1, 0, 2).reshape(depth, 4 * n_mod)

    stacked = _gather_weights([wts[n].astype(BF16) for n in _BIG], "gather_big")
    for n, st in zip(_BIG, stacked, strict=True):
        full[n] = _full_from_shards(st, n)
    full["ffn_w_gu"] = jnp.concatenate([full.pop("ffn_w_gate"), full.pop("ffn_w_up")], axis=2)
    for n in _REPLICATED:
        full[n] = wts[n]

    n_s5 = s5_lam_re.shape[0]
    gp = s5_lam_re.shape[1] * s5_lam_re.shape[2]
    col = lambda a: a.reshape(gp, 1)
    s5_raw_in = []
    full["s5"] = []
    for j in range(n_s5):
        ldt = jnp.broadcast_to(s5_log_dt[j][:, None], s5_lam_re[j].shape)
        raw = (col(s5_lam_re[j]), col(s5_lam_im[j]), col(ldt), s5_b_re[j].reshape(gp, S5_GROUP), s5_b_im[j].reshape(gp, S5_GROUP))
        ar, ai, bbr, bbi = _s5_prep(*raw, "s5_prep")
        s5_raw_in.append(raw)
        shape3 = s5_b_re[j].shape
        full["s5"].append(dict(
            ar=ar.reshape(1, gp), ai=ai.reshape(1, gp),
            wr=_block_diag(bbr.reshape(shape3), False).astype(BF16), wi=_block_diag(bbi.reshape(shape3), False).astype(BF16),
            cr=_block_diag(s5_c_re[j], True).astype(BF16), ci=_block_diag(s5_c_im[j], True).astype(BF16)))

    loss_cols, grad_x, dmod, grads = _local_step(x2, mod, full, tgt)
    loss = lax.psum(0.5 * jnp.sum(loss_cols) / d_dim, ("x", "y", "c"))

    small_grads = {}
    for name in ("s5_lam_re", "s5_lam_im", "s5_log_dt", "s5_b_re", "s5_b_im", "s5_c_re", "s5_c_im"):
        small_grads[name] = []
    for j in range(n_s5):
        dwr, dwi, dcr, dci, dar, dai = grads["s5_raw"][j]
        shape3 = s5_b_re[j].shape
        cots = (jnp.sum(dar, axis=0).reshape(gp, 1), jnp.sum(dai, axis=0).reshape(gp, 1),
                _block_diag_extract(dwr, False).reshape(gp, S5_GROUP), _block_diag_extract(dwi, False).reshape(gp, S5_GROUP))
        dlr, dli, dldt, dbr, dbi = _s5_prep_bwd(*s5_raw_in[j], cots, "s5_prep_bwd")
        small_grads["s5_lam_re"].append(dlr.reshape(s5_lam_re[j].shape))
        small_grads["s5_lam_im"].append(dli.reshape(s5_lam_re[j].shape))
        small_grads["s5_log_dt"].append(jnp.sum(dldt.reshape(s5_lam_re[j].shape), axis=1))
        small_grads["s5_b_re"].append(dbr.reshape(shape3))
        small_grads["s5_b_im"].append(dbi.reshape(shape3))
        small_grads["s5_c_re"].append(_block_diag_extract(dcr, True))
        small_grads["s5_c_im"].append(_block_diag_extract(dci, True))
    small_grads = {n: jnp.stack(v) for n, v in small_grads.items()}
    stack = lambda name, count: jnp.stack([grads[name][j] for j in range(count)])
    small_grads["s5_d"] = stack("s5_d", n_s5)[:, 0]
    small_grads["s5_b_glu"] = stack("s5_b_glu", n_s5)[:, 0]
    n_cv = cv_w_pw1.shape[0]
    for name in ("cv_b_pw1", "cv_b_dw", "cv_ln_g", "cv_ln_b", "cv_b_pw2"):
        small_grads[name] = stack(name, n_cv)[:, 0]
    small_grads["cv_w_dw"] = stack("cv_w_dw", n_cv)
    small_grads["norm_g"] = grads["norm_g"]

    small_order = ["norm_g", "s5_lam_re", "s5_lam_im", "s5_log_dt", "s5_b_re", "s5_b_im", "s5_c_re", "s5_c_im", "s5_d",
                   "s5_b_glu", "cv_b_pw1", "cv_w_dw", "cv_b_dw", "cv_ln_g", "cv_ln_b", "cv_b_pw2"]
    small_shapes = [small_grads[n].shape for n in small_order]
    sg_in = _pack([dmod] + [small_grads[n] for n in small_order], F32, SUBLANES)
    sg_all = _all_gather8(sg_in, "gather_small_grads")
    sg_sum = _sum8(sg_all, "sum_small_grads").reshape(-1)
    dmod_all = sg_all.reshape(8, -1)[:, :dmod.size].reshape(8, depth, 4, n_mod)
    g_full = dict(zip(small_order, _unpack(sg_sum[dmod.size:], small_shapes), strict=True))
    g_full["b_mod"] = sg_sum[:dmod.size].reshape(dmod.shape)

    gu = stack("ffn_w_gu", depth)
    f_dim = gu.shape[2] // 2
    big_grads = {"sb_w_qkv": stack("sb_w_qkv", sb_w_qkv.shape[0]), "sb_w_o": stack("sb_w_o", sb_w_o.shape[0]),
                 "s5_w_glu": stack("s5_w_glu", n_s5), "cv_w_pw1": stack("cv_w_pw1", n_cv), "cv_w_pw2": stack("cv_w_pw2", n_cv),
                 "ffn_w_gate": gu[:, :, :f_dim], "ffn_w_up": gu[:, :, f_dim:], "ffn_w_down": stack("ffn_w_down", depth)}
    chip_major = [_shards_from_full(big_grads[n], n) for n in _BIG]
    core_idx, chip_core = pc.reshape(1).astype(I32), jnp.stack([chip, pc]).astype(I32)
    from_sibling = _reduce_to_sibling(chip_major, "reduce_sibling")
    chip_sums = [_add_own_half(g, r, core_idx, "reduce_add_sibling") for g, r in zip(chip_major, from_sibling, strict=True)]
    from_chips = _exchange_chips(chip_sums, "reduce_chips")
    my_half = [_add_chips(g, s, r, chip_core, "reduce_add_chips")
               for g, s, r in zip(chip_major, from_sibling, from_chips, strict=True)]
    g_shard = dict(zip(_BIG, _share_with_sibling(my_half, "reduce_share"), strict=True))

    out_g, out_d, out_m, out_v = {}, {}, {}, {}
    for n in _BIG:
        shp = wts[n].shape
        two = lambda a: a.reshape(-1, shp[-1])
        out_g[n] = g_shard[n]
        d, nm, nv = _adamw(two(wts[n]), two(g_shard[n]), two(mom[n]), two(var[n]), "adamw")
        out_d[n], out_m[n], out_v[n] = d.reshape(shp), nm.reshape(shp), nv.reshape(shp)
    sc_t = jax.nn.silu(c_all).T
    dmod_cols = lax.dynamic_index_in_dim(dmod_all, chip, axis=2, keepdims=False).transpose(1, 0, 2)
    out_g["w_mod"], out_d["w_mod"], out_m["w_mod"], out_v["w_mod"] = _mod_grad_adamw(
        sc_t, dmod_cols, w_mod, m_w_mod, v_w_mod, "w_mod_adamw")
    small_all_names = [n for n in _WEIGHTS if n not in _BIG and n != "w_mod"]
    local_g = []
    for n in small_all_names:
        g = g_full[n]
        if n in _SMALL_SHARDED:
            width = wts[n].shape[-1]
            g = lax.dynamic_slice_in_dim(g, chip * width, width, axis=g.ndim - 1)
        local_g.append(g)
        out_g[n] = g
    pk = lambda arrs: _pack(arrs, F32, SUBLANES)
    d, nm, nv = _adamw(pk([wts[n] for n in small_all_names]), pk(local_g), pk([mom[n] for n in small_all_names]),
                       pk([var[n] for n in small_all_names]), "adamw_small")
    shapes = [wts[n].shape for n in small_all_names]
    for n, a, b, cc in zip(small_all_names, _unpack(d.reshape(-1), shapes), _unpack(nm.reshape(-1), shapes),
                           _unpack(nv.reshape(-1), shapes), strict=True):
        out_d[n], out_m[n], out_v[n] = a, b, cc
    return (loss, grad_x[None], *[out_g[n] for n in _WEIGHTS], *[out_d[n] for n in _WEIGHTS],
            *[out_m[n] for n in _WEIGHTS], *[out_v[n] for n in _WEIGHTS])
```

```python
import functools
import math

import jax
import jax.numpy as jnp
from jax import lax
from jax.experimental import pallas as pl
from jax.experimental.pallas import tpu as pltpu

F32 = jnp.float32
BF16 = jnp.bfloat16
I32 = jnp.int32
SDS = jax.ShapeDtypeStruct
MESH = pl.DeviceIdType.MESH

HEAD_DIM = 64
HEAD_PAIR = 2 * HEAD_DIM
S5_GROUP = 16
S5_STATE = 64
S5_BLOCK_GROUPS = 8
S5_BLOCK_CH = S5_GROUP * S5_BLOCK_GROUPS
S5_BLOCK_ST = S5_STATE * S5_BLOCK_GROUPS
CONV_WIDTH = 31
CONV_HALO = 32
EPS = 1e-6
ADAM_LR = 0.001
ADAM_B1 = 0.9
ADAM_B2 = 0.999
ADAM_EPS = 1e-08
ADAM_WD = 0.01
ADAM_STEP = 10

LANES = 128
SUBLANES = 8
VMEM_LIMIT = 56 * 2 ** 20
PACK_LANES = 512
ATT_BLOCK = 128
EXP_FLOOR = -104.0
ATT_KEEP = 4
_KEPT = ("log_keep", "log_beta", "weight", "d_exponent")

_COUNTER = [0]


def _name(base):
    _COUNTER[0] += 1
    return f"{base}_{_COUNTER[0]}"


def _tile(dim, target, mult=LANES):
    t = min(dim, target)
    t -= t % mult
    while t > mult and dim % t:
        t -= mult
    return t if t > 0 and dim % t == 0 else dim


def _params(sem=None):
    return pltpu.CompilerParams(dimension_semantics=sem, vmem_limit_bytes=VMEM_LIMIT)


_DIMS = {"nn": (((1,), (0,)), ((), ())), "nt": (((1,), (1,)), ((), ())), "tn": (((0,), (0,)), ((), ()))}


def _matmul(a, b, form, out_dtype, base, tm=1024, tn=1536, tk=1536):
    if form == "tn":
        k_dim, m_dim = a.shape
    else:
        m_dim, k_dim = a.shape
    n_dim = b.shape[0] if form == "nt" else b.shape[1]
    tm, tn, tk = _tile(m_dim, tm), _tile(n_dim, tn), _tile(k_dim, tk)
    nk = k_dim // tk
    dims = _DIMS[form]

    def body(a_ref, b_ref, o_ref, *acc):
        part = lax.dot_general(a_ref[...].astype(BF16), b_ref[...].astype(BF16), dims, preferred_element_type=F32)
        if nk == 1:
            o_ref[...] = part.astype(o_ref.dtype)
            return
        acc_ref, = acc
        k = pl.program_id(2)

        @pl.when(k == 0)
        def _():
            acc_ref[...] = part

        @pl.when(jnp.logical_and(k > 0, k < nk - 1))
        def _():
            acc_ref[...] += part

        @pl.when(k == nk - 1)
        def _():
            o_ref[...] = (acc_ref[...] + part).astype(o_ref.dtype)

    a_spec = (pl.BlockSpec((tk, tm), lambda i, j, k: (k, i)) if form == "tn"
              else pl.BlockSpec((tm, tk), lambda i, j, k: (i, k)))
    b_spec = (pl.BlockSpec((tn, tk), lambda i, j, k: (j, k)) if form == "nt"
              else pl.BlockSpec((tk, tn), lambda i, j, k: (k, j)))
    return pl.pallas_call(
        body, name=_name(base), grid=(m_dim // tm, n_dim // tn, nk),
        in_specs=[a_spec, b_spec], out_specs=pl.BlockSpec((tm, tn), lambda i, j, k: (i, j)),
        out_shape=SDS((m_dim, n_dim), out_dtype), scratch_shapes=[pltpu.VMEM((tm, tn), F32)] if nk > 1 else [],
        compiler_params=_params(("parallel", "parallel", "arbitrary")),
    )(a, b)


def _rowwise(fn, rows, vecs, out_rows, out_vecs, base, tr=256):
    s_dim = rows[0].shape[0]
    tr = _tile(s_dim, tr, SUBLANES)
    nr, nv, nor = len(rows), len(vecs), len(out_rows)

    def body(*refs):
        r_in, v_in = refs[:nr], refs[nr:nr + nv]
        r_out, v_out = refs[nr + nv:nr + nv + nor], refs[nr + nv + nor:]
        ro, vo = fn(*[r[...] for r in r_in], *[v[...] for v in v_in])
        for ref, val in zip(r_out, ro, strict=True):
            ref[...] = val.astype(ref.dtype)
        if v_out:
            i = pl.program_id(0)

            @pl.when(i == 0)
            def _():
                for ref, val in zip(v_out, vo, strict=True):
                    ref[...] = val.astype(ref.dtype)

            @pl.when(i > 0)
            def _():
                for ref, val in zip(v_out, vo, strict=True):
                    ref[...] += val.astype(ref.dtype)

    in_specs = [pl.BlockSpec((tr, r.shape[1]), lambda i: (i, 0)) for r in rows]
    in_specs += [pl.BlockSpec(v.shape, lambda i: (0, 0)) for v in vecs]
    out_specs = [pl.BlockSpec((tr, o.shape[1]), lambda i: (i, 0)) for o in out_rows]
    out_specs += [pl.BlockSpec(o.shape, lambda i: (0, 0)) for o in out_vecs]
    outs = pl.pallas_call(
        body, name=_name(base), grid=(s_dim // tr,), in_specs=in_specs, out_specs=out_specs,
        out_shape=list(out_rows) + list(out_vecs),
        compiler_params=_params(("arbitrary",)),
    )(*rows, *vecs)
    return outs


def _f(x):
    return x.astype(F32)


def _rms(x, g):
    return x * lax.rsqrt(jnp.mean(x * x, axis=-1, keepdims=True) + EPS) * g


def _pre(h, g, sh, sc):
    return _rms(h, g) * (1 + sc) + sh


def _post(h, m, gate, g):
    return h + gate * _rms(m, g)


def _halves(t):
    d = t.shape[1] // 2
    return t[:, :d], t[:, d:]


def _glu2(ta, tb, ba, bb):
    return (ta + ba) * jax.nn.sigmoid(tb + bb)


def _glu(t, b):
    return _glu2(*_halves(t), *_halves(b))


def _glu_bwd(t, dout, b):
    _, vjp = jax.vjp(_glu2, *_halves(t), *_halves(b))
    dta, dtb, dba, dbb = vjp(dout)
    return jnp.concatenate([dta, dtb], axis=1), jnp.concatenate([dba, dbb], axis=1)


def _swish2(g, u):
    return jax.nn.silu(g) * u


def _swish_gate(gu):
    return _swish2(*_halves(gu))


def _ln_silu(h, g, b):
    mu = jnp.mean(h, axis=-1, keepdims=True)
    var = jnp.mean(jnp.square(h - mu), axis=-1, keepdims=True)
    return jax.nn.silu((h - mu) * lax.rsqrt(var + EPS) * g + b)


def _dot(a, b, form="nn"):
    return lax.dot_general(a, b, _DIMS[form], preferred_element_type=F32)


def _split_dot(x, tri):
    hi = x.astype(BF16)
    lo = (x - hi.astype(F32)).astype(BF16)
    return _dot(hi, tri) + _dot(lo, tri)


def _sb_scores(qb, kb, scale, mask):
    z = _dot(qb, kb, "nt") * scale
    sp = jnp.maximum(z, 0.0) + jnp.log(1.0 + jnp.exp(-jnp.abs(z)))
    lk = -sp
    if mask is not None:
        lk = jnp.where(mask, lk, 0.0)
    return lk, z - sp


def _walk_left(i, state, step):
    def cond(ns):
        live = functools.reduce(jnp.maximum, [jnp.max(c) for c in ns[1][0]])
        return jnp.logical_and(ns[0] < i, live > EXP_FLOOR)

    def body(ns):
        return ns[0] + 1, step(i - 1 - ns[0], ns[1])

    return lax.while_loop(cond, body, (jnp.int32(0), state))[1]


def _pair_split(x):
    first = lax.broadcasted_iota(I32, x.shape, 1) < HEAD_DIM
    zero = jnp.zeros_like(x)
    return (jnp.where(first, x, zero), jnp.where(first, zero, x)), first


def _pair_specs(s_dim, d_dim, bq):
    npair = d_dim // HEAD_PAIR
    blk = pl.BlockSpec((bq, HEAD_PAIR), lambda p, i: (i, p))
    k_all = pl.BlockSpec((s_dim, HEAD_PAIR), lambda p, i: (0, npair + p))
    v_all = pl.BlockSpec((s_dim, HEAD_PAIR), lambda p, i: (0, 2 * npair + p))
    return npair, blk, k_all, v_all


def _sb_fwd(qkv, base):
    s_dim, d_dim = qkv.shape[0], qkv.shape[1] // 3
    bq = ATT_BLOCK
    scale = HEAD_DIM ** -0.5
    npair, blk, k_all, v_all = _pair_specs(s_dim, d_dim, bq)

    def body(q_ref, k_ref, v_ref, o_ref):
        i = pl.program_id(1)
        qs, first = _pair_split(q_ref[...])
        row = lax.broadcasted_iota(I32, (bq, bq), 0)
        col = lax.broadcasted_iota(I32, (bq, bq), 1)
        later = (row > col).astype(BF16)
        causal = col < row

        def tile(j, cs, accs, mask):
            ks = pl.ds(pl.multiple_of(j * bq, bq), bq)
            kb, vb = k_ref[ks, :], v_ref[ks, :]
            new_c, new_acc = [], []
            for qh, c, acc in zip(qs, cs, accs, strict=True):
                lk, lb = _sb_scores(qh, kb, scale, mask)
                w = jnp.exp(lb + _split_dot(lk, later) + c)
                if mask is not None:
                    w = jnp.where(mask, w, 0.0)
                new_acc.append(acc + _dot(w.astype(BF16), vb))
                new_c.append(c + jnp.sum(lk, axis=1, keepdims=True))
            return tuple(new_c), tuple(new_acc)

        zero_c, zero_acc = jnp.zeros((bq, 1), F32), jnp.zeros((bq, HEAD_PAIR), F32)
        cs, accs = tile(i, (zero_c, zero_c), (zero_acc, zero_acc), causal)
        cs, accs = _walk_left(i, (cs, accs), lambda j, s: tile(j, s[0], s[1], None))
        o_ref[...] = jnp.where(first, accs[0], accs[1]).astype(o_ref.dtype)

    return pl.pallas_call(
        body, name=_name(base), grid=(npair, s_dim // bq), in_specs=[blk, k_all, v_all], out_specs=blk,
        out_shape=SDS((s_dim, d_dim), BF16), compiler_params=_params(("parallel", "arbitrary")),
    )(qkv, qkv, qkv)


def _sb_bwd(qkv, do, base):
    s_dim, d_dim = do.shape
    bq = ATT_BLOCK
    scale = HEAD_DIM ** -0.5
    npair, blk, k_all, v_all = _pair_specs(s_dim, d_dim, bq)

    def body(q_ref, k_ref, v_ref, do_ref, dq_ref, dk_ref, dv_ref, kept):
        i = pl.program_id(1)
        qs, first = _pair_split(q_ref[...])
        dos, _ = _pair_split(do_ref[...])
        heads = (0, 1)
        row = lax.broadcasted_iota(I32, (bq, bq), 0)
        col = lax.broadcasted_iota(I32, (bq, bq), 1)
        later = (row > col).astype(BF16)
        from_here = (row >= col).astype(BF16)
        causal = col < row

        @pl.when(i == 0)
        def _():
            dk_ref[...] = jnp.zeros_like(dk_ref)
            dv_ref[...] = jnp.zeros_like(dv_ref)

        def weights(h, kb, vb, c, mask):
            lk, lb = _sb_scores(qs[h], kb, scale, mask)
            w = jnp.exp(lb + _split_dot(lk, later) + c)
            if mask is not None:
                w = jnp.where(mask, w, 0.0)
            return lk, lb, w, _dot(dos[h], vb, "nt") * w

        def slot(h, n):
            return (h * ATT_KEEP + n) * len(_KEPT)

        def total(j, cs, tots, mask):
            ks = pl.ds(pl.multiple_of(j * bq, bq), bq)
            kb, vb = k_ref[ks, :], v_ref[ks, :]
            new_c, new_tot = [], []
            for h in heads:
                tile = weights(h, kb, vb, cs[h], mask)

                @pl.when(i - j < ATT_KEEP)
                def _(tile=tile, h=h):
                    for a, val in enumerate(tile):
                        kept[slot(h, i - j) + a] = val

                new_c.append(cs[h] + jnp.sum(tile[0], axis=1, keepdims=True))
                new_tot.append(tots[h] + jnp.sum(tile[3], axis=1, keepdims=True))
            return tuple(new_c), tuple(new_tot)

        zero = jnp.zeros((bq, 1), F32)
        cs, tots = total(i, (zero, zero), (zero, zero), causal)
        cs, tots = _walk_left(i, (cs, tots), lambda j, s: total(j, s[0], s[1], None))

        def grads(j, cs, rs, dqs, mask):
            ks = pl.ds(pl.multiple_of(j * bq, bq), bq)
            kb, vb = k_ref[ks, :], v_ref[ks, :]
            new_c, new_r, new_dq = [], [], []
            dk_t = dv_t = None
            for h in heads:
                lk, lb, w, de = lax.cond(
                    i - j < ATT_KEEP,
                    lambda h=h: tuple(kept[slot(h, i - j) + a] for a in range(len(_KEPT))),
                    lambda h=h: weights(h, kb, vb, cs[h], mask))
                before = tots[h] - (rs[h] + _split_dot(de, from_here))
                dz = de * jnp.exp(lk) - jnp.exp(lb) * before
                if mask is not None:
                    dz = jnp.where(mask, dz, 0.0)
                dzb = (dz * scale).astype(BF16)
                dk_h, dv_h = _dot(dzb, qs[h], "tn"), _dot(w.astype(BF16), dos[h], "tn")
                dk_t, dv_t = (dk_h, dv_h) if dk_t is None else (dk_t + dk_h, dv_t + dv_h)
                new_dq.append(dqs[h] + _dot(dzb, kb))
                new_c.append(cs[h] + jnp.sum(lk, axis=1, keepdims=True))
                new_r.append(rs[h] + jnp.sum(de, axis=1, keepdims=True))
            dk_ref[ks, :] += dk_t
            dv_ref[ks, :] += dv_t
            return tuple(new_c), tuple(new_r), tuple(new_dq)

        zero_dq = jnp.zeros((bq, HEAD_PAIR), F32)
        state = grads(i, (zero, zero), (zero, zero), (zero_dq, zero_dq), causal)
        _, _, dqs = _walk_left(i, state, lambda j, s: grads(j, s[0], s[1], s[2], None))
        dq_ref[...] = jnp.where(first, dqs[0], dqs[1])

    cols = pl.BlockSpec((s_dim, HEAD_PAIR), lambda p, i: (0, p))
    out = SDS((s_dim, d_dim), F32)
    return pl.pallas_call(
        body, name=_name(base), grid=(npair, s_dim // bq), in_specs=[blk, k_all, v_all, blk],
        out_specs=[blk, cols, cols], out_shape=[out, out, out],
        scratch_shapes=[pltpu.VMEM((2 * ATT_KEEP * len(_KEPT), bq, bq), F32)],
        compiler_params=_params(("parallel", "arbitrary")),
    )(qkv, qkv, qkv, do)


def _cmul(ar, ai, br, bi):
    return ar * br - ai * bi, ar * bi + ai * br


def _s5_prep_fn(lr, li, ldt, br, bi):
    dt = jnp.exp(ldt)
    mag = jnp.exp(lr * dt)
    ar, ai = mag * jnp.cos(li * dt), mag * jnp.sin(li * dt)
    den = lr * lr + li * li
    er = ((ar - 1) * lr + ai * li) / den
    ei = (ai * lr - (ar - 1) * li) / den
    return ar, ai, er * br - ei * bi, er * bi + ei * br


def _s5_prep(lr, li, ldt, br, bi, base):
    col, mat = SDS(lr.shape, F32), SDS(br.shape, F32)
    return _rowwise(lambda *a: (_s5_prep_fn(*a), ()), [lr, li, ldt, br, bi], [], [col, col, mat, mat], [], base, tr=512)


def _s5_prep_bwd(lr, li, ldt, br, bi, cots, base):
    col, mat = SDS(lr.shape, F32), SDS(br.shape, F32)

    def bwd(lr, li, ldt, br, bi, c0, c1, c2, c3):
        _, vjp = jax.vjp(_s5_prep_fn, lr, li, ldt, br, bi)
        return vjp((c0, c1, c2, c3)), ()

    return _rowwise(bwd, [lr, li, ldt, br, bi, *cots], [], [col, col, col, mat, mat], [], base, tr=512)


def _scan_tables(ar_ref, ai_ref, sl, conj):
    shape = (SUBLANES, S5_BLOCK_ST)
    a_r = jnp.broadcast_to(ar_ref[:, sl], shape)
    a_i = jnp.broadcast_to(ai_ref[:, sl], shape)
    if conj:
        a_i = -a_i
    a2 = _cmul(a_r, a_i, a_r, a_i)
    a4 = _cmul(*a2, *a2)
    row = lax.broadcasted_iota(I32, shape, 0)
    p_r, p_i, t_r, t_i = a_r, a_i, a_r, a_i
    for k in range(1, SUBLANES):
        p_r, p_i = _cmul(p_r, p_i, a_r, a_i)
        sel = (row == (SUBLANES - 1 - k)) if conj else (row == k)
        t_r, t_i = jnp.where(sel, p_r, t_r), jnp.where(sel, p_i, t_i)
    if conj:
        t_r, t_i = jnp.where(row == SUBLANES - 1, a_r, t_r), jnp.where(row == SUBLANES - 1, a_i, t_i)
    return row, (a_r, a_i), a2, a4, (t_r, t_i)


def _scan_group(xr, xi, row, a1, a2, a4, table, cr, ci, reverse):
    for s, (mr, mi) in ((1, a1), (2, a2), (4, a4)):
        if reverse:
            keep = row < SUBLANES - s
            sr, si = pltpu.roll(xr, SUBLANES - s, 0), pltpu.roll(xi, SUBLANES - s, 0)
        else:
            keep = row >= s
            sr, si = pltpu.roll(xr, s, 0), pltpu.roll(xi, s, 0)
        sr, si = jnp.where(keep, sr, 0.0), jnp.where(keep, si, 0.0)
        xr, xi = xr + mr * sr - mi * si, xi + mr * si + mi * sr
    tr, ti = table
    return xr + tr * cr - ti * ci, xi + tr * ci + ti * cr


def _s5_fwd(u, ar, ai, wr, wi, cr_w, ci_w, dsk, base, chunk=128):
    s_dim, d_dim = u.shape
    nb = d_dim // S5_BLOCK_CH
    n_st = nb * S5_BLOCK_ST
    chunk = _tile(s_dim, chunk, SUBLANES)
    nch = s_dim // chunk

    def body(u_ref, ar_ref, ai_ref, wr_ref, wi_ref, cr_ref, ci_ref, d_ref, y_ref, gy_ref, sr_ref, si_ref,
             xr_s, xi_s, car_r, car_i):
        @pl.when(pl.program_id(0) == 0)
        def _():
            car_r[...] = jnp.zeros_like(car_r)
            car_i[...] = jnp.zeros_like(car_i)

        sr_ref[...] = car_r[...]
        si_ref[...] = car_i[...]
        for b in range(nb):
            ub = u_ref[:, b * S5_BLOCK_CH:(b + 1) * S5_BLOCK_CH].astype(BF16)
            sl = slice(b * S5_BLOCK_ST, (b + 1) * S5_BLOCK_ST)
            xr_s[:, sl] = _dot(ub, wr_ref[b])
            xi_s[:, sl] = _dot(ub, wi_ref[b])
        for b in range(nb):
            sl = slice(b * S5_BLOCK_ST, (b + 1) * S5_BLOCK_ST)
            row, a1, a2, a4, table = _scan_tables(ar_ref, ai_ref, sl, False)

            def group(g, carry, sl=sl, row=row, a1=a1, a2=a2, a4=a4, table=table):
                rows = pl.ds(pl.multiple_of(g * SUBLANES, SUBLANES), SUBLANES)
                xr, xi = _scan_group(xr_s[rows, sl], xi_s[rows, sl], row, a1, a2, a4, table, carry[0], carry[1], False)
                xr_s[rows, sl] = xr
                xi_s[rows, sl] = xi
                last = (SUBLANES, S5_BLOCK_ST)
                return (jnp.broadcast_to(xr[SUBLANES - 1:, :], last), jnp.broadcast_to(xi[SUBLANES - 1:, :], last))

            c_r, c_i = lax.fori_loop(0, chunk // SUBLANES, group, (car_r[:, sl], car_i[:, sl]))
            car_r[:, sl] = c_r
            car_i[:, sl] = c_i
        for b in range(nb):
            sl = slice(b * S5_BLOCK_ST, (b + 1) * S5_BLOCK_ST)
            ch = slice(b * S5_BLOCK_CH, (b + 1) * S5_BLOCK_CH)
            y = _dot(xr_s[:, sl].astype(BF16), cr_ref[b]) - _dot(xi_s[:, sl].astype(BF16), ci_ref[b])
            y = y + d_ref[:, ch] * u_ref[:, ch]
            y_ref[:, ch] = y
            gy_ref[:, ch] = jax.nn.gelu(y).astype(BF16)

    rows = pl.BlockSpec((chunk, d_dim), lambda i: (i, 0))
    vec = pl.BlockSpec((1, n_st), lambda i: (0, 0))
    w_in = pl.BlockSpec((nb, S5_BLOCK_CH, S5_BLOCK_ST), lambda i: (0, 0, 0))
    w_out = pl.BlockSpec((nb, S5_BLOCK_ST, S5_BLOCK_CH), lambda i: (0, 0, 0))
    st = pl.BlockSpec((SUBLANES, n_st), lambda i: (i, 0))
    return pl.pallas_call(
        body, name=_name(base), grid=(nch,),
        in_specs=[rows, vec, vec, w_in, w_in, w_out, w_out, pl.BlockSpec((1, d_dim), lambda i: (0, 0))],
        out_specs=[rows, rows, st, st],
        out_shape=[SDS((s_dim, d_dim), F32), SDS((s_dim, d_dim), BF16),
                   SDS((nch * SUBLANES, n_st), F32), SDS((nch * SUBLANES, n_st), F32)],
        scratch_shapes=[pltpu.VMEM((chunk, n_st), F32), pltpu.VMEM((chunk, n_st), F32),
                        pltpu.VMEM((SUBLANES, n_st), F32), pltpu.VMEM((SUBLANES, n_st), F32)],
        compiler_params=_params(("arbitrary",)),
    )(u, ar, ai, wr, wi, cr_w, ci_w, dsk)


def _s5_bwd(u, dy, st_r, st_i, ar, ai, wr, wi, cr_w, ci_w, dsk, base, chunk):
    s_dim, d_dim = u.shape
    nb = d_dim // S5_BLOCK_CH
    n_st = nb * S5_BLOCK_ST
    nch = s_dim // chunk
    ng = chunk // SUBLANES

    def body(u_ref, dy_ref, sr_ref, si_ref, ar_ref, ai_ref, wr_ref, wi_ref, cr_ref, ci_ref, d_ref,
             du_ref, dwr_ref, dwi_ref, dcr_ref, dci_ref, dar_ref, dai_ref, dd_ref,
             xr_s, xi_s, lr_s, li_s, car_r, car_i):
        first = pl.program_id(0) == 0

        @pl.when(first)
        def _():
            car_r[...] = jnp.zeros_like(car_r)
            car_i[...] = jnp.zeros_like(car_i)
            for ref in (dwr_ref, dwi_ref, dcr_ref, dci_ref, dar_ref, dai_ref, dd_ref):
                ref[...] = jnp.zeros_like(ref)

        xr_s[pl.ds(0, SUBLANES), :] = sr_ref[...]
        xi_s[pl.ds(0, SUBLANES), :] = si_ref[...]
        body_rows = pl.ds(SUBLANES, chunk)
        dyb = dy_ref[...]
        dd_ref[...] += jnp.sum(dyb * u_ref[...], axis=0, keepdims=True)
        for b in range(nb):
            ch = slice(b * S5_BLOCK_CH, (b + 1) * S5_BLOCK_CH)
            sl = slice(b * S5_BLOCK_ST, (b + 1) * S5_BLOCK_ST)
            ub = u_ref[:, ch].astype(BF16)
            dyc = dyb[:, ch].astype(BF16)
            xr_s[body_rows, sl] = _dot(ub, wr_ref[b])
            xi_s[body_rows, sl] = _dot(ub, wi_ref[b])
            lr_s[:, sl] = _dot(dyc, cr_ref[b], "nt")
            li_s[:, sl] = -_dot(dyc, ci_ref[b], "nt")
        for b in range(nb):
            sl = slice(b * S5_BLOCK_ST, (b + 1) * S5_BLOCK_ST)
            row, a1, a2, a4, table = _scan_tables(ar_ref, ai_ref, sl, False)

            def fwd_group(g, carry, sl=sl, row=row, a1=a1, a2=a2, a4=a4, table=table):
                rows = pl.ds(pl.multiple_of((g + 1) * SUBLANES, SUBLANES), SUBLANES)
                xr, xi = _scan_group(xr_s[rows, sl], xi_s[rows, sl], row, a1, a2, a4, table, carry[0], carry[1], False)
                xr_s[rows, sl] = xr
                xi_s[rows, sl] = xi
                last = (SUBLANES, S5_BLOCK_ST)
                return (jnp.broadcast_to(xr[SUBLANES - 1:, :], last), jnp.broadcast_to(xi[SUBLANES - 1:, :], last))

            lax.fori_loop(0, ng, fwd_group, (sr_ref[:, sl], si_ref[:, sl]))
            row, a1, a2, a4, table = _scan_tables(ar_ref, ai_ref, sl, True)

            def bwd_group(n, carry, sl=sl, row=row, a1=a1, a2=a2, a4=a4, table=table):
                g = ng - 1 - n
                c_r, c_i, acc_r, acc_i = carry
                rows = pl.ds(pl.multiple_of(g * SUBLANES, SUBLANES), SUBLANES)
                lr, li = _scan_group(lr_s[rows, sl], li_s[rows, sl], row, a1, a2, a4, table, c_r, c_i, True)
                lr_s[rows, sl] = lr
                li_s[rows, sl] = li
                prev = pl.ds(pl.multiple_of(g * SUBLANES, SUBLANES), SUBLANES)
                here = pl.ds(pl.multiple_of((g + 1) * SUBLANES, SUBLANES), SUBLANES)
                px_r = jnp.where(row >= 1, pltpu.roll(xr_s[here, sl], 1, 0), pltpu.roll(xr_s[prev, sl], 1, 0))
                px_i = jnp.where(row >= 1, pltpu.roll(xi_s[here, sl], 1, 0), pltpu.roll(xi_s[prev, sl], 1, 0))
                acc_r = acc_r + lr * px_r + li * px_i
                acc_i = acc_i + li * px_r - lr * px_i
                first_row = (SUBLANES, S5_BLOCK_ST)
                return (jnp.broadcast_to(lr[:1, :], first_row), jnp.broadcast_to(li[:1, :], first_row), acc_r, acc_i)

            zero = jnp.zeros((SUBLANES, S5_BLOCK_ST), F32)
            c_r, c_i, acc_r, acc_i = lax.fori_loop(0, ng, bwd_group, (car_r[:, sl], car_i[:, sl], zero, zero))
            car_r[:, sl] = c_r
            car_i[:, sl] = c_i
            dar_ref[:, sl] += acc_r
            dai_ref[:, sl] += acc_i
        for b in range(nb):
            ch = slice(b * S5_BLOCK_CH, (b + 1) * S5_BLOCK_CH)
            sl = slice(b * S5_BLOCK_ST, (b + 1) * S5_BLOCK_ST)
            ub = u_ref[:, ch].astype(BF16)
            dyc = dyb[:, ch].astype(BF16)
            lrb, lib = lr_s[:, sl].astype(BF16), li_s[:, sl].astype(BF16)
            du_ref[:, ch] = (_dot(lrb, wr_ref[b], "nt") + _dot(lib, wi_ref[b], "nt") + d_ref[:, ch] * dyb[:, ch])
            dwr_ref[b] += _dot(ub, lrb, "tn")
            dwi_ref[b] += _dot(ub, lib, "tn")
            dcr_ref[b] += _dot(xr_s[body_rows, sl].astype(BF16), dyc, "tn")
            dci_ref[b] -= _dot(xi_s[body_rows, sl].astype(BF16), dyc, "tn")

    rev = lambda i: (nch - 1 - i, 0)
    rows = pl.BlockSpec((chunk, d_dim), rev)
    st = pl.BlockSpec((SUBLANES, n_st), rev)
    vec = pl.BlockSpec((1, n_st), lambda i: (0, 0))
    w_in = pl.BlockSpec((nb, S5_BLOCK_CH, S5_BLOCK_ST), lambda i: (0, 0, 0))
    w_out = pl.BlockSpec((nb, S5_BLOCK_ST, S5_BLOCK_CH), lambda i: (0, 0, 0))
    acc8 = pl.BlockSpec((SUBLANES, n_st), lambda i: (0, 0))
    dvec = pl.BlockSpec((1, d_dim), lambda i: (0, 0))
    return pl.pallas_call(
        body, name=_name(base), grid=(nch,),
        in_specs=[rows, rows, st, st, vec, vec, w_in, w_in, w_out, w_out, dvec],
        out_specs=[rows, w_in, w_in, w_out, w_out, acc8, acc8, dvec],
        out_shape=[SDS((s_dim, d_dim), F32),
                   SDS((nb, S5_BLOCK_CH, S5_BLOCK_ST), F32), SDS((nb, S5_BLOCK_CH, S5_BLOCK_ST), F32),
                   SDS((nb, S5_BLOCK_ST, S5_BLOCK_CH), F32), SDS((nb, S5_BLOCK_ST, S5_BLOCK_CH), F32),
                   SDS((SUBLANES, n_st), F32), SDS((SUBLANES, n_st), F32), SDS((1, d_dim), F32)],
        scratch_shapes=[pltpu.VMEM((chunk + SUBLANES, n_st), F32), pltpu.VMEM((chunk + SUBLANES, n_st), F32),
                        pltpu.VMEM((chunk, n_st), F32), pltpu.VMEM((chunk, n_st), F32),
                        pltpu.VMEM((SUBLANES, n_st), F32), pltpu.VMEM((SUBLANES, n_st), F32)],
        compiler_params=_params(("arbitrary",)),
    )(u, dy, st_r, st_i, ar, ai, wr, wi, cr_w, ci_w, dsk)


def _block_diag(w, transpose):
    g = w.shape[0]
    nb = g // S5_BLOCK_GROUPS
    eye = jnp.eye(S5_BLOCK_GROUPS, dtype=w.dtype)
    if transpose:
        w = w.reshape(nb, S5_BLOCK_GROUPS, S5_GROUP, S5_STATE)
        return jnp.einsum("bgcp,gh->bgphc", w, eye).reshape(nb, S5_BLOCK_ST, S5_BLOCK_CH)
    w = w.reshape(nb, S5_BLOCK_GROUPS, S5_STATE, S5_GROUP)
    return jnp.einsum("bgpc,gh->bgchp", w, eye).reshape(nb, S5_BLOCK_CH, S5_BLOCK_ST)


def _block_diag_extract(w, transpose):
    nb = w.shape[0]
    eye = jnp.eye(S5_BLOCK_GROUPS, dtype=w.dtype)
    if transpose:
        w = w.reshape(nb, S5_BLOCK_GROUPS, S5_STATE, S5_BLOCK_GROUPS, S5_GROUP)
        return jnp.einsum("bgphc,gh->bgcp", w, eye).reshape(nb * S5_BLOCK_GROUPS, S5_GROUP, S5_STATE)
    w = w.reshape(nb, S5_BLOCK_GROUPS, S5_GROUP, S5_BLOCK_GROUPS, S5_STATE)
    return jnp.einsum("bgchp,gh->bgpc", w, eye).reshape(nb * S5_BLOCK_GROUPS, S5_STATE, S5_GROUP)


def _dwconv_fwd(h, w, b, base, tr=512):
    s_dim, d_dim = h.shape
    tr = _tile(s_dim, tr, CONV_HALO)
    per = tr // CONV_HALO

    def body(h_ref, halo_ref, w_ref, b_ref, o_ref, buf):
        i = pl.program_id(0)
        buf[pl.ds(0, CONV_HALO), :] = jnp.where(i > 0, halo_ref[...], 0.0)
        buf[pl.ds(CONV_HALO, tr), :] = h_ref[...]
        acc = jnp.broadcast_to(b_ref[...], (tr, d_dim))
        for k in range(CONV_WIDTH):
            acc = acc + w_ref[pl.ds(k, 1), :] * buf[pl.ds(CONV_HALO - (CONV_WIDTH - 1) + k, tr), :]
        o_ref[...] = acc

    return pl.pallas_call(
        body, name=_name(base), grid=(s_dim // tr,),
        in_specs=[pl.BlockSpec((tr, d_dim), lambda i: (i, 0)),
                  pl.BlockSpec((CONV_HALO, d_dim), lambda i: (jnp.maximum(i * per - 1, 0), 0)),
                  pl.BlockSpec((CONV_WIDTH, d_dim), lambda i: (0, 0)), pl.BlockSpec((1, d_dim), lambda i: (0, 0))],
        out_specs=pl.BlockSpec((tr, d_dim), lambda i: (i, 0)), out_shape=SDS((s_dim, d_dim), F32),
        scratch_shapes=[pltpu.VMEM((tr + CONV_HALO, d_dim), F32)], compiler_params=_params(("arbitrary",)),
    )(h, h, w, b)


def _dwconv_bwd(h, dout, w, base, tr=512):
    s_dim, d_dim = h.shape
    tr = _tile(s_dim, tr, CONV_HALO)
    per = tr // CONV_HALO
    n = s_dim // tr
    last_halo = s_dim // CONV_HALO - 1

    def body(h_ref, hhalo_ref, d_ref, dhalo_ref, w_ref, dh_ref, dw_ref, db_ref, hbuf, dbuf):
        i = pl.program_id(0)
        hbuf[pl.ds(0, CONV_HALO), :] = jnp.where(i > 0, hhalo_ref[...], 0.0)
        hbuf[pl.ds(CONV_HALO, tr), :] = h_ref[...]
        dbuf[pl.ds(0, tr), :] = d_ref[...]
        dbuf[pl.ds(tr, CONV_HALO), :] = jnp.where(i < n - 1, dhalo_ref[...], 0.0)
        dout_t = d_ref[...]
        acc = jnp.zeros((tr, d_dim), F32)
        dws = []
        for k in range(CONV_WIDTH):
            acc = acc + w_ref[pl.ds(k, 1), :] * dbuf[pl.ds(CONV_WIDTH - 1 - k, tr), :]
            shifted = hbuf[pl.ds(CONV_HALO - (CONV_WIDTH - 1) + k, tr), :]
            dws.append(jnp.sum(dout_t * shifted, axis=0, keepdims=True))
        dh_ref[...] = acc
        dw_t = jnp.concatenate(dws, axis=0)
        db_t = jnp.sum(dout_t, axis=0, keepdims=True)

        @pl.when(i == 0)
        def _():
            dw_ref[...] = dw_t
            db_ref[...] = db_t

        @pl.when(i > 0)
        def _():
            dw_ref[...] += dw_t
            db_ref[...] += db_t

    tile = pl.BlockSpec((tr, d_dim), lambda i: (i, 0))
    before = pl.BlockSpec((CONV_HALO, d_dim), lambda i: (jnp.maximum(i * per - 1, 0), 0))
    after = pl.BlockSpec((CONV_HALO, d_dim), lambda i: (jnp.minimum((i + 1) * per, last_halo), 0))
    return pl.pallas_call(
        body, name=_name(base), grid=(n,),
        in_specs=[tile, before, tile, after, pl.BlockSpec((CONV_WIDTH, d_dim), lambda i: (0, 0))],
        out_specs=[tile, pl.BlockSpec((CONV_WIDTH, d_dim), lambda i: (0, 0)), pl.BlockSpec((1, d_dim), lambda i: (0, 0))],
        out_shape=[SDS((s_dim, d_dim), F32), SDS((CONV_WIDTH, d_dim), F32), SDS((1, d_dim), F32)],
        scratch_shapes=[pltpu.VMEM((tr + CONV_HALO, d_dim), F32), pltpu.VMEM((tr + CONV_HALO, d_dim), F32)],
        compiler_params=_params(("arbitrary",)),
    )(h, h, dout, dout, w)


def _local_step(x, mod, w, target):
    s_dim, d_dim = x.shape
    depth = mod.shape[0]
    row = lambda dt=F32, n=d_dim: SDS((s_dim, n), dt)
    vec = lambda n=d_dim: SDS((1, n), F32)
    mods = [[mod[l:l + 1, k * d_dim:(k + 1) * d_dim] for k in range(6)] for l in range(depth)]
    ng = lambda l, k: w["norm_g"][l, k:k + 1, :]
    saved = []
    h = x
    for l in range(depth):
        sh_m, sc_m, g_m, sh_f, sc_f, g_f = mods[l]
        kind, j = l % 3, l // 3
        s = {"h0": h}
        if kind == 1:
            (u,) = _rowwise(lambda h, g, a, b: ((_pre(h, g, a, b),), ()), [h], [ng(l, 0), sh_m, sc_m], [row()], [], "pre")
        else:
            (u,) = _rowwise(lambda h, g, a, b: ((_pre(h, g, a, b),), ()), [h], [ng(l, 0), sh_m, sc_m], [row(BF16)], [], "pre")
        s["u"] = u
        bias = None
        if kind == 0:
            qkv = _matmul(u, w["sb_w_qkv"][j], "nn", BF16, "qkv")
            o = _sb_fwd(qkv, "sb_fwd")
            m = _matmul(o, w["sb_w_o"][j], "nn", F32, "wo")
            s.update(qkv=qkv, o=o)
        elif kind == 1:
            p = w["s5"][j]
            y_pre, gy, st_r, st_i = _s5_fwd(u, p["ar"], p["ai"], p["wr"], p["wi"], p["cr"], p["ci"], w["s5_d"][j:j + 1],
                                            "s5_fwd")
            t = _matmul(gy, w["s5_w_glu"][j], "nn", F32, "s5_glu")
            b_glu = w["s5_b_glu"][j:j + 1]
            (m,) = _rowwise(lambda t, b: ((_glu(t, b),), ()), [t], [b_glu], [row()], [], "s5_gate")
            s.update(y_pre=y_pre, gy=gy, st_r=st_r, st_i=st_i, t=t)
        else:
            t = _matmul(u, w["cv_w_pw1"][j], "nn", F32, "pw1")
            (hg,) = _rowwise(lambda t, b: ((_glu(t, b),), ()), [t], [w["cv_b_pw1"][j:j + 1]], [row()], [], "cv_glu")
            hc = _dwconv_fwd(hg, w["cv_w_dw"][j], w["cv_b_dw"][j:j + 1], "dwconv")
            (hs,) = _rowwise(lambda h, g, b: ((_ln_silu(h, g, b),), ()), [hc],
                             [w["cv_ln_g"][j:j + 1], w["cv_ln_b"][j:j + 1]], [row(BF16)], [], "cv_ln")
            m = _matmul(hs, w["cv_w_pw2"][j], "nn", F32, "pw2")
            bias = w["cv_b_pw2"][j:j + 1]
            s.update(t=t, hg=hg, hc=hc, hs=hs)
        s["m"] = m
        if bias is None:
            h1, u2 = _rowwise(
                lambda h, m, gate, g1, g2, a, b: ((lambda h1: (h1, _pre(h1, g2, a, b)))(_post(h, m, gate, g1)), ()),
                [h, m], [g_m, ng(l, 1), ng(l, 2), sh_f, sc_f], [row(), row(BF16)], [], "post_pre")
        else:
            h1, u2 = _rowwise(
                lambda h, m, gate, g1, g2, a, b, bb: ((lambda h1: (h1, _pre(h1, g2, a, b)))(_post(h, m + bb, gate, g1)), ()),
                [h, m], [g_m, ng(l, 1), ng(l, 2), sh_f, sc_f, bias], [row(), row(BF16)], [], "post_pre")
        f_dim = w["ffn_w_gu"].shape[2] // 2
        gu = _matmul(u2, w["ffn_w_gu"][l], "nn", BF16, "ffn_gu")
        (act,) = _rowwise(lambda gu: ((_swish_gate(_f(gu)),), ()), [gu], [], [row(BF16, f_dim)], [], "ffn_act")
        f = _matmul(act, w["ffn_w_down"][l], "nn", F32, "ffn_down")
        (h,) = _rowwise(lambda h, m, gate, g: ((_post(h, m, gate, g),), ()), [h1, f], [g_f, ng(l, 3)], [row()], [], "post")
        s.update(h1=h1, u2=u2, gu=gu, act=act, f=f, bias=bias)
        saved.append(s)
    dh, loss_cols = _rowwise(
        lambda y, t: ((((y - t) * (1.0 / d_dim)),), (jnp.sum(jnp.square(y - t), axis=0, keepdims=True),)),
        [h, target], [], [row()], [vec()], "loss")
    grads = {}
    dmod = [[None] * 6 for _ in range(depth)]
    dnorm = [[None] * 4 for _ in range(depth)]

    def add(name, idx, val):
        grads.setdefault(name, {})[idx] = val

    for l in reversed(range(depth)):
        sh_m, sc_m, g_m, sh_f, sc_f, g_f = mods[l]
        kind, j = l % 3, l // 3
        s = saved[l]

        def post_bwd(h, m, dh, gate, g):
            _, vjp = jax.vjp(_post, h, m, gate, g)
            _, dm, dgate, dg = vjp(dh)
            return (dm,), (dgate, dg)

        df, dmod[l][5], dnorm[l][3] = _rowwise(post_bwd, [s["h1"], s["f"], dh], [g_f, ng(l, 3)],
                                               [row(BF16)], [vec(), vec()], "post_bwd")
        f_dim = s["act"].shape[1]
        add("ffn_w_down", l, _matmul(s["act"], df, "tn", F32, "d_down"))
        dact = _matmul(df, w["ffn_w_down"][l], "nt", BF16, "d_act")

        def act_bwd(gu, da):
            _, vjp = jax.vjp(_swish2, *_halves(_f(gu)))
            return (jnp.concatenate(vjp(_f(da)), axis=1),), ()

        (dgu,) = _rowwise(act_bwd, [s["gu"], dact], [], [row(BF16, 2 * f_dim)], [], "act_bwd")
        add("ffn_w_gu", l, _matmul(s["u2"], dgu, "tn", F32, "d_gu"))
        du2 = _matmul(dgu, w["ffn_w_gu"][l], "nt", F32, "d_u2")
        bias = s["bias"]

        def post_pre_bwd(h, m, dh1, du2, gate, g1, g2, a, b, *bb):
            def fwd(h, m, gate, g1, g2, a, b):
                mm = m + bb[0] if bb else m
                h1 = _post(h, mm, gate, g1)
                return h1, _pre(h1, g2, a, b)
            _, vjp = jax.vjp(fwd, h, m, gate, g1, g2, a, b)
            dh0, dm, dgate, dg1, dg2, da, db = vjp((dh1, du2))
            return (dh0, dm), (dgate, dg1, dg2, da, db, jnp.sum(dm, axis=0, keepdims=True))

        vecs = [g_m, ng(l, 1), ng(l, 2), sh_f, sc_f] + ([bias] if bias is not None else [])
        dh, dm, dmod[l][2], dnorm[l][1], dnorm[l][2], dmod[l][3], dmod[l][4], dm_cols = _rowwise(
            post_pre_bwd, [s["h0"], s["m"], dh, du2], vecs, [row(), row(BF16)], [vec()] * 6, "post_pre_bwd")
        if kind == 0:
            add("sb_w_o", j, _matmul(s["o"], dm, "tn", F32, "d_wo"))
            do = _matmul(dm, w["sb_w_o"][j], "nt", BF16, "d_o")
            dqkv = jnp.concatenate(_sb_bwd(s["qkv"], do, "sb_bwd"), axis=1).astype(BF16)
            add("sb_w_qkv", j, _matmul(s["u"], dqkv, "tn", F32, "d_wqkv"))
            du = _matmul(dqkv, w["sb_w_qkv"][j], "nt", F32, "d_u")
        elif kind == 1:
            p = w["s5"][j]
            b_glu = w["s5_b_glu"][j:j + 1]

            def gate_bwd(t, dm, b):
                dt, db = _glu_bwd(t, _f(dm), b)
                return (dt,), (db,)

            dt, db_glu = _rowwise(gate_bwd, [s["t"], dm], [b_glu], [row(BF16, 2 * d_dim)], [vec(2 * d_dim)], "s5_gate_bwd")
            add("s5_b_glu", j, db_glu)
            add("s5_w_glu", j, _matmul(s["gy"], dt, "tn", F32, "d_wglu"))
            dgy = _matmul(dt, w["s5_w_glu"][j], "nt", F32, "d_gy")

            def gelu_bwd(y, dg):
                _, vjp = jax.vjp(jax.nn.gelu, y)
                return (vjp(dg)[0],), ()

            (dy,) = _rowwise(gelu_bwd, [s["y_pre"], dgy], [], [row()], [], "gelu_bwd")
            chunk = s_dim // (s["st_r"].shape[0] // SUBLANES)
            du, dwr, dwi, dcr, dci, dar, dai, dd = _s5_bwd(
                s["u"], dy, s["st_r"], s["st_i"], p["ar"], p["ai"], p["wr"], p["wi"], p["cr"], p["ci"],
                w["s5_d"][j:j + 1], "s5_bwd", chunk)
            add("s5_d", j, dd)
            add("s5_raw", j, (dwr, dwi, dcr, dci, dar, dai))
        else:
            add("cv_w_pw2", j, _matmul(s["hs"], dm, "tn", F32, "d_pw2"))
            dhs = _matmul(dm, w["cv_w_pw2"][j], "nt", F32, "d_hs")
            ln_g, ln_b = w["cv_ln_g"][j:j + 1], w["cv_ln_b"][j:j + 1]

            def ln_bwd(hc, dhs, g, b):
                _, vjp = jax.vjp(_ln_silu, hc, g, b)
                dhc, dg, db = vjp(dhs)
                return (dhc,), (dg, db)

            dhc, dln_g, dln_b = _rowwise(ln_bwd, [s["hc"], dhs], [ln_g, ln_b], [row()], [vec()] * 2, "ln_bwd")
            add("cv_ln_g", j, dln_g)
            add("cv_ln_b", j, dln_b)
            add("cv_b_pw2", j, dm_cols)
            dhg, dw_dw, db_dw = _dwconv_bwd(s["hg"], dhc, w["cv_w_dw"][j], "dwconv_bwd")
            add("cv_w_dw", j, dw_dw)
            add("cv_b_dw", j, db_dw)
            b_pw1 = w["cv_b_pw1"][j:j + 1]

            def glu_bwd(t, dhg, b):
                dt, db = _glu_bwd(t, dhg, b)
                return (dt,), (db,)

            dt, db_pw1 = _rowwise(glu_bwd, [s["t"], dhg], [b_pw1], [row(BF16, 2 * d_dim)], [vec(2 * d_dim)], "cv_glu_bwd")
            add("cv_b_pw1", j, db_pw1)
            add("cv_w_pw1", j, _matmul(s["u"], dt, "tn", F32, "d_pw1"))
            du = _matmul(dt, w["cv_w_pw1"][j], "nt", F32, "d_u")

        def pre_bwd(h, dh, du, g, a, b):
            _, vjp = jax.vjp(_pre, h, g, a, b)
            dh0, dg, da, db = vjp(du)
            return (dh + dh0,), (dg, da, db)

        dh, dnorm[l][0], dmod[l][0], dmod[l][1] = _rowwise(pre_bwd, [s["h0"], dh, du], [ng(l, 0), sh_m, sc_m],
                                                            [row()], [vec()] * 3, "pre_bwd")
    grads["norm_g"] = jnp.stack([jnp.concatenate(r, axis=0) for r in dnorm])
    dmod_arr = jnp.concatenate([jnp.concatenate(r, axis=1) for r in dmod], axis=0)
    return loss_cols, dh, dmod_arr, grads


def _position():
    return lax.axis_index("x"), lax.axis_index("y"), lax.axis_index("c")


def _all_gather8(x, base):
    m, n = x.shape
    space = pltpu.VMEM

    def body(x_ref, out_ref, send_sems, recv_sems, local_sem):
        px, py, pc = _position()
        me, sibling = (px, py, pc), (px, py, 1 - pc)
        chips = [(1 - px, py), (px, 1 - py), (1 - px, 1 - py)]
        src_mine = x_ref

        def rows(bx, by, bc):
            return out_ref.at[pl.ds((4 * bx + 2 * by + bc) * m, m), :]

        def copy(k, block, to, src=None):
            return pltpu.make_async_remote_copy(
                src_ref=rows(*block) if src is None else src, dst_ref=rows(*block),
                send_sem=send_sems.at[k], recv_sem=recv_sems.at[k], device_id=to, device_id_type=MESH)

        mine = pltpu.make_async_copy(src_mine, rows(*me), local_sem)
        mine.start()
        first = [copy(0, me, sibling, src=src_mine)]
        first += [copy(1 + j, me, (*chip, pc), src=src_mine) for j, chip in enumerate(chips)]
        for cp in first:
            cp.start()
        passed = [copy(4 + j, (*chip, pc), sibling) for j, chip in enumerate(chips)]
        for j, chip in enumerate(chips):
            copy(1 + j, (*chip, pc), me).wait_recv()
            passed[j].start()
        copy(0, sibling, me).wait_recv()
        for j, chip in enumerate(chips):
            copy(4 + j, (*chip, 1 - pc), me).wait_recv()
        for cp in first + passed:
            cp.wait_send()
        mine.wait()

    return pl.pallas_call(
        body, name=_name(base), out_shape=SDS((8 * m, n), x.dtype),
        in_specs=[pl.BlockSpec(memory_space=space)], out_specs=pl.BlockSpec(memory_space=space),
        scratch_shapes=[pltpu.SemaphoreType.DMA((7,)), pltpu.SemaphoreType.DMA((7,)), pltpu.SemaphoreType.DMA],
    )(x)


_ANY = pl.BlockSpec(memory_space=pl.ANY)


def _half(ref, core, lead):
    k_half = ref.shape[lead] // 2
    return ref.at[(slice(None),) * lead + (pl.ds(core * k_half, k_half), slice(None))]


def _gather_weights(shards, base):
    nw = len(shards)

    def body(*refs):
        x_refs, out_refs = refs[:nw], refs[nw:2 * nw]
        send_sems, recv_sems, local_sems = refs[2 * nw:]
        px, py, pc = _position()
        me, sibling = (px, py, pc), (px, py, 1 - pc)
        chips = [(1 - px, py), (px, 1 - py), (1 - px, 1 - py)]

        def block(i, bx, by, bc):
            return _half(out_refs[i].at[2 * bx + by], bc, 1)

        def copy(i, k, blk, to, src=None):
            dst = block(i, *blk)
            return pltpu.make_async_remote_copy(
                src_ref=dst if src is None else src, dst_ref=dst, send_sem=send_sems.at[7 * i + k],
                recv_sem=recv_sems.at[7 * i + k], device_id=to, device_id_type=MESH)

        mine = [_half(x_refs[i], pc, 1) for i in range(nw)]
        local = [pltpu.make_async_copy(mine[i], block(i, *me), local_sems.at[i]) for i in range(nw)]
        for cp in local:
            cp.start()
        first = []
        for i in range(nw):
            first.append(copy(i, 0, me, sibling, src=mine[i]))
            first += [copy(i, 1 + j, me, (*chip, pc), src=mine[i]) for j, chip in enumerate(chips)]
        for cp in first:
            cp.start()
        passed = []
        for j, chip in enumerate(chips):
            for i in range(nw):
                copy(i, 1 + j, (*chip, pc), me).wait_recv()
                passed.append(copy(i, 4 + j, (*chip, pc), sibling))
                passed[-1].start()
        for i in range(nw):
            copy(i, 0, sibling, me).wait_recv()
        for j, chip in enumerate(chips):
            for i in range(nw):
                copy(i, 4 + j, (*chip, 1 - pc), me).wait_recv()
        for cp in first + passed:
            cp.wait_send()
        for cp in local:
            cp.wait()

    return pl.pallas_call(
        body, name=_name(base), out_shape=[SDS((4,) + s.shape, s.dtype) for s in shards],
        in_specs=[_ANY] * nw, out_specs=[_ANY] * nw,
        scratch_shapes=[pltpu.SemaphoreType.DMA((7 * nw,)), pltpu.SemaphoreType.DMA((7 * nw,)),
                        pltpu.SemaphoreType.DMA((nw,))],
    )(*shards)


def _reduce_to_sibling(gs, base):
    nw = len(gs)

    def body(*refs):
        g_refs, recv_refs, send_sems, recv_sems = refs[:nw], refs[nw:2 * nw], refs[2 * nw], refs[2 * nw + 1]
        px, py, pc = _position()
        copies = [pltpu.make_async_remote_copy(
            src_ref=_half(g_refs[i], 1 - pc, 2), dst_ref=recv_refs[i], send_sem=send_sems.at[i],
            recv_sem=recv_sems.at[i], device_id=(px, py, 1 - pc), device_id_type=MESH) for i in range(nw)]
        for cp in copies:
            cp.start()
        for cp in copies:
            cp.wait_recv()
        for cp in copies:
            cp.wait_send()

    outs = [SDS(g.shape[:2] + (g.shape[2] // 2, g.shape[3]), g.dtype) for g in gs]
    return pl.pallas_call(
        body, name=_name(base), out_shape=outs, in_specs=[_ANY] * nw, out_specs=[_ANY] * nw,
        scratch_shapes=[pltpu.SemaphoreType.DMA((nw,)), pltpu.SemaphoreType.DMA((nw,))],
    )(*gs)


def _exchange_chips(ts, base):
    nw = len(ts)

    def body(*refs):
        t_refs, recv_refs, send_sems, recv_sems = refs[:nw], refs[nw:2 * nw], refs[2 * nw], refs[2 * nw + 1]
        px, py, pc = _position()
        chips = [(1 - px, py), (px, 1 - py), (1 - px, 1 - py)]
        copies = [pltpu.make_async_remote_copy(
            src_ref=t_refs[i].at[2 * cx + cy], dst_ref=recv_refs[i].at[k], send_sem=send_sems.at[3 * i + k],
            recv_sem=recv_sems.at[3 * i + k], device_id=(cx, cy, pc), device_id_type=MESH)
            for k, (cx, cy) in enumerate(chips) for i in range(nw)]
        for cp in copies:
            cp.start()
        for cp in copies:
            cp.wait_recv()
        for cp in copies:
            cp.wait_send()

    outs = [SDS((3,) + t.shape[1:], t.dtype) for t in ts]
    return pl.pallas_call(
        body, name=_name(base), out_shape=outs, in_specs=[_ANY] * nw, out_specs=[_ANY] * nw,
        scratch_shapes=[pltpu.SemaphoreType.DMA((3 * nw,)), pltpu.SemaphoreType.DMA((3 * nw,))],
    )(*ts)


def _share_with_sibling(rs, base):
    nw = len(rs)

    def body(*refs):
        in_refs, out_refs, send_sems, recv_sems = refs[:nw], refs[nw:2 * nw], refs[2 * nw], refs[2 * nw + 1]
        px, py, pc = _position()
        send = [pltpu.make_async_remote_copy(
            src_ref=_half(in_refs[i], pc, 1), dst_ref=_half(out_refs[i], pc, 1), send_sem=send_sems.at[i],
            recv_sem=recv_sems.at[i], device_id=(px, py, 1 - pc), device_id_type=MESH) for i in range(nw)]
        for cp in send:
            cp.start()
        for i in range(nw):
            pltpu.make_async_remote_copy(
                src_ref=_half(in_refs[i], 1 - pc, 1), dst_ref=_half(out_refs[i], 1 - pc, 1), send_sem=send_sems.at[i],
                recv_sem=recv_sems.at[i], device_id=(px, py, 1 - pc), device_id_type=MESH).wait_recv()
        for cp in send:
            cp.wait_send()

    return pl.pallas_call(
        body, name=_name(base), out_shape=[SDS(r.shape, r.dtype) for r in rs], in_specs=[_ANY] * nw,
        out_specs=[_ANY] * nw, input_output_aliases={i: i for i in range(nw)},
        scratch_shapes=[pltpu.SemaphoreType.DMA((nw,)), pltpu.SemaphoreType.DMA((nw,))],
    )(*rs)


def _add_own_half(g, recv, core, base, tr=512):
    _, l_dim, k_half, n = recv.shape
    tr = _tile(k_half, tr, 2 * SUBLANES)
    per = k_half // tr

    def body(core_ref, g_ref, r_ref, o_ref):
        o_ref[...] = (g_ref[...] + r_ref[...]).astype(o_ref.dtype)

    blk = (1, 1, tr, n)
    spec = pltpu.PrefetchScalarGridSpec(
        num_scalar_prefetch=1, grid=(4, l_dim, per),
        in_specs=[pl.BlockSpec(blk, lambda j, l, i, core: (j, l, core[0] * per + i, 0)),
                  pl.BlockSpec(blk, lambda j, l, i, core: (j, l, i, 0))],
        out_specs=pl.BlockSpec(blk, lambda j, l, i, core: (j, l, i, 0)))
    return pl.pallas_call(body, name=_name(base), grid_spec=spec, out_shape=SDS(recv.shape, BF16),
                          compiler_params=_params(("parallel", "parallel", "parallel")))(core, g, recv)


def _add_chips(g, from_sibling, recv, chip_core, base, tr=512):
    _, l_dim, k_half, n = from_sibling.shape
    tr = _tile(k_half, tr, 2 * SUBLANES)
    per = k_half // tr

    def body(idx_ref, g_ref, s_ref, r_ref, o_ref):
        own = g_ref[0, 0] + s_ref[0, 0]
        o_ref[0] = ((own + _f(r_ref[0, 0])) + _f(r_ref[1, 0])) + _f(r_ref[2, 0])

    spec = pltpu.PrefetchScalarGridSpec(
        num_scalar_prefetch=1, grid=(l_dim, per),
        in_specs=[pl.BlockSpec((1, 1, tr, n), lambda l, i, idx: (idx[0], l, idx[1] * per + i, 0)),
                  pl.BlockSpec((1, 1, tr, n), lambda l, i, idx: (idx[0], l, i, 0)),
                  pl.BlockSpec((3, 1, tr, n), lambda l, i, idx: (0, l, i, 0))],
        out_specs=pl.BlockSpec((1, tr, n), lambda l, i, idx: (l, idx[1] * per + i, 0)))
    return pl.pallas_call(body, name=_name(base), grid_spec=spec, out_shape=SDS((l_dim, 2 * k_half, n), F32),
                          compiler_params=_params(("parallel", "parallel")))(chip_core, g, from_sibling, recv)


def _sum8(g, base):
    r, n = g.shape[0] // 8, g.shape[1]

    def body(g_ref, o_ref):
        acc = g_ref[pl.ds(0, r), :]
        for d in range(1, 8):
            acc = acc + g_ref[pl.ds(d * r, r), :]
        o_ref[...] = acc

    return pl.pallas_call(body, name=_name(base), out_shape=SDS((r, n), F32), compiler_params=_params())(g)


def _adamw_math(w, g, m, v):
    m = ADAM_B1 * m + (1.0 - ADAM_B1) * g
    v = ADAM_B2 * v + (1.0 - ADAM_B2) * jnp.square(g)
    m_hat = m / (1.0 - ADAM_B1 ** ADAM_STEP)
    v_hat = v / (1.0 - ADAM_B2 ** ADAM_STEP)
    delta = -ADAM_LR * (m_hat / (jnp.sqrt(v_hat) + ADAM_EPS) + ADAM_WD * w)
    return delta, m, v


def _adamw(w, g, m, v, base):
    out = SDS(w.shape, F32)
    return _rowwise(lambda w, g, m, v: (_adamw_math(w, g, m, v), ()), [w, g, m, v], [], [out, out, out], [], base)


def _mod_grad_adamw(sc_t, dmod, w, m, v, base, tr=256):
    l_dim, d_dim, n = w.shape
    tr = _tile(d_dim, tr, SUBLANES)

    def body(sc_ref, dm_ref, w_ref, m_ref, v_ref, g_ref, d_ref, nm_ref, nv_ref):
        g = lax.dot_general(sc_ref[...], dm_ref[0], _DIMS["nn"], preferred_element_type=F32,
                            precision=lax.Precision.HIGHEST)
        delta, nm, nv = _adamw_math(w_ref[0], g, m_ref[0], v_ref[0])
        g_ref[0], d_ref[0], nm_ref[0], nv_ref[0] = g, delta, nm, nv

    tile = pl.BlockSpec((1, tr, n), lambda l, i: (l, i, 0))
    out = SDS(w.shape, F32)
    return pl.pallas_call(
        body, name=_name(base), grid=(l_dim, d_dim // tr),
        in_specs=[pl.BlockSpec((tr, 8), lambda l, i: (i, 0)), pl.BlockSpec((1, 8, n), lambda l, i: (l, 0, 0)),
                  tile, tile, tile],
        out_specs=[tile] * 4, out_shape=[out] * 4, compiler_params=_params(("parallel", "parallel")),
    )(sc_t, dmod, w, m, v)


def _mod_forward(c_all, w_mod, b_mod, base):
    l_dim, d_dim, n = w_mod.shape

    def body(c_ref, w_ref, b_ref, o_ref):
        o_ref[0] = lax.dot_general(jax.nn.silu(c_ref[...]), w_ref[0], _DIMS["nn"], preferred_element_type=F32,
                                   precision=lax.Precision.HIGHEST) + b_ref[0]

    return pl.pallas_call(
        body, name=_name(base), grid=(l_dim,),
        in_specs=[pl.BlockSpec((8, d_dim), lambda l: (0, 0)), pl.BlockSpec((1, d_dim, n), lambda l: (l, 0, 0)),
                  pl.BlockSpec((1, 1, n), lambda l: (l, 0, 0))],
        out_specs=pl.BlockSpec((1, 8, n), lambda l: (l, 0, 0)), out_shape=SDS((l_dim, 8, n), F32),
        compiler_params=_params(("parallel",)),
    )(c_all, w_mod, b_mod)


def _pack(arrays, dtype, row_mult):
    flat = jnp.concatenate([a.reshape(-1).astype(dtype) for a in arrays])
    quantum = PACK_LANES * row_mult
    pad = (-flat.shape[0]) % quantum
    return jnp.pad(flat, (0, pad)).reshape(-1, PACK_LANES)


def _unpack(flat, shapes):
    out, off = [], 0
    for shp in shapes:
        n = math.prod(shp)
        out.append(flat[off:off + n].reshape(shp))
        off += n
    return out


_WEIGHTS = ["norm_g", "w_mod", "b_mod", "sb_w_qkv", "sb_w_o", "s5_lam_re", "s5_lam_im", "s5_log_dt", "s5_b_re",
            "s5_b_im", "s5_c_re", "s5_c_im", "s5_d", "s5_w_glu", "s5_b_glu", "cv_w_pw1", "cv_b_pw1", "cv_w_dw",
            "cv_b_dw", "cv_ln_g", "cv_ln_b", "cv_w_pw2", "cv_b_pw2", "ffn_w_gate", "ffn_w_up", "ffn_w_down"]
_BIG_COL = ["sb_w_qkv", "s5_w_glu", "cv_w_pw1", "ffn_w_gate", "ffn_w_up"]
_BIG_ROW = ["sb_w_o", "cv_w_pw2", "ffn_w_down"]
_BIG = _BIG_COL + _BIG_ROW
_SMALL_SHARDED = {"norm_g": 2, "cv_b_pw1": 1, "cv_w_dw": 2, "cv_b_dw": 1, "cv_ln_g": 1, "cv_ln_b": 1, "cv_b_pw2": 1}
_REPLICATED = ["b_mod", "s5_lam_re", "s5_lam_im", "s5_log_dt", "s5_b_re", "s5_b_im", "s5_c_re", "s5_c_im", "s5_d",
               "s5_b_glu"]


def _full_from_shards(stacked, name):
    if name in _BIG_COL or (name in _SMALL_SHARDED):
        axis = stacked.ndim - 1
    else:
        axis = stacked.ndim - 2
    moved = jnp.moveaxis(stacked, 0, axis - 1)
    shp = list(moved.shape)
    shp[axis - 1:axis + 1] = [shp[axis - 1] * shp[axis]]
    return moved.reshape(shp)


def _shards_from_full(full, name):
    axis = full.ndim - 1 if (name in _BIG_COL or name in _SMALL_SHARDED) else full.ndim - 2
    shp = list(full.shape)
    shp[axis:axis + 1] = [4, shp[axis] // 4]
    return jnp.moveaxis(full.reshape(shp), axis, 0)


def kernel(x, c, norm_g, w_mod, b_mod, sb_w_qkv, sb_w_o, s5_lam_re, s5_lam_im, s5_log_dt, s5_b_re, s5_b_im, s5_c_re, s5_c_im, s5_d, s5_w_glu, s5_b_glu, cv_w_pw1, cv_b_pw1, cv_w_dw, cv_b_dw, cv_ln_g, cv_ln_b, cv_w_pw2, cv_b_pw2, ffn_w_gate, ffn_w_up, ffn_w_down, loss_target, m_norm_g, m_w_mod, m_b_mod, m_sb_w_qkv, m_sb_w_o, m_s5_lam_re, m_s5_lam_im, m_s5_log_dt, m_s5_b_re, m_s5_b_im, m_s5_c_re, m_s5_c_im, m_s5_d, m_s5_w_glu, m_s5_b_glu, m_cv_w_pw1, m_cv_b_pw1, m_cv_w_dw, m_cv_b_dw, m_cv_ln_g, m_cv_ln_b, m_cv_w_pw2, m_cv_b_pw2, m_ffn_w_gate, m_ffn_w_up, m_ffn_w_down, v_norm_g, v_w_mod, v_b_mod, v_sb_w_qkv, v_sb_w_o, v_s5_lam_re, v_s5_lam_im, v_s5_log_dt, v_s5_b_re, v_s5_b_im, v_s5_c_re, v_s5_c_im, v_s5_d, v_s5_w_glu, v_s5_b_glu, v_cv_w_pw1, v_cv_b_pw1, v_cv_w_dw, v_cv_b_dw, v_cv_ln_g, v_cv_ln_b, v_cv_w_pw2, v_cv_b_pw2, v_ffn_w_gate, v_ffn_w_up, v_ffn_w_down):
    args = locals()
    wts = {n: args[n] for n in _WEIGHTS}
    mom = {n: args["m_" + n] for n in _WEIGHTS}
    var = {n: args["v_" + n] for n in _WEIGHTS}
    px, py, pc = lax.axis_index("x"), lax.axis_index("y"), lax.axis_index("c")
    chip = 2 * px + py
    dev = 2 * chip + pc
    s_dim, d_dim = x.shape[1], x.shape[2]
    depth = norm_g.shape[0]
    x2, tgt = x[0], loss_target[0]

    small_names = list(_SMALL_SHARDED)
    small_in = _pack([c] + [wts[n] for n in small_names], F32, SUBLANES)
    small_all = _all_gather8(small_in, "gather_small").reshape(8, -1)
    c_all = small_all[:, :d_dim]
    full = {}
    off = d_dim
    for n in small_names:
        size = math.prod(wts[n].shape)
        stacked = small_all[0::2, off:off + size].reshape((4,) + wts[n].shape)
        full[n] = _full_from_shards(stacked, n)
        off += size

    n_mod = w_mod.shape[2]
    b_mod_cols = lax.dynamic_slice_in_dim(b_mod, chip * n_mod, n_mod, axis=1)[:, None, :]
    mod_part = _mod_forward(c_all, w_mod, b_mod_cols, "mod_fwd")
    mod_all = _all_gather8(mod_part.reshape(depth * 8, n_mod), "gather_mod")
    mod_all = mod_all.reshape(4, 2, depth, 8, n_mod)[:, 0]
    mod_mine = lax.dynamic_index_in_dim(mod_all, dev, axis=2, keepdims=False)
    mod = mod_mine.transpose(1, 0, 2).reshape(depth, 4 * n_mod)

    stacked = _gather_weights([wts[n].astype(BF16) for n in _BIG], "gather_big")
    for n, st in zip(_BIG, stacked, strict=True):
        full[n] = _full_from_shards(st, n)
    full["ffn_w_gu"] = jnp.concatenate([full.pop("ffn_w_gate"), full.pop("ffn_w_up")], axis=2)
    for n in _REPLICATED:
        full[n] = wts[n]

    n_s5 = s5_lam_re.shape[0]
    gp = s5_lam_re.shape[1] * s5_lam_re.shape[2]
    col = lambda a: a.reshape(gp, 1)
    s5_raw_in = []
    full["s5"] = []
    for j in range(n_s5):
        ldt = jnp.broadcast_to(s5_log_dt[j][:, None], s5_lam_re[j].shape)
        raw = (col(s5_lam_re[j]), col(s5_lam_im[j]), col(ldt), s5_b_re[j].reshape(gp, S5_GROUP), s5_b_im[j].reshape(gp, S5_GROUP))
        ar, ai, bbr, bbi = _s5_prep(*raw, "s5_prep")
        s5_raw_in.append(raw)
        shape3 = s5_b_re[j].shape
        full["s5"].append(dict(
            ar=ar.reshape(1, gp), ai=ai.reshape(1, gp),
            wr=_block_diag(bbr.reshape(shape3), False).astype(BF16), wi=_block_diag(bbi.reshape(shape3), False).astype(BF16),
            cr=_block_diag(s5_c_re[j], True).astype(BF16), ci=_block_diag(s5_c_im[j], True).astype(BF16)))

    loss_cols, grad_x, dmod, grads = _local_step(x2, mod, full, tgt)
    loss = lax.psum(0.5 * jnp.sum(loss_cols) / d_dim, ("x", "y", "c"))

    small_grads = {}
    for name in ("s5_lam_re", "s5_lam_im", "s5_log_dt", "s5_b_re", "s5_b_im", "s5_c_re", "s5_c_im"):
        small_grads[name] = []
    for j in range(n_s5):
        dwr, dwi, dcr, dci, dar, dai = grads["s5_raw"][j]
        shape3 = s5_b_re[j].shape
        cots = (jnp.sum(dar, axis=0).reshape(gp, 1), jnp.sum(dai, axis=0).reshape(gp, 1),
                _block_diag_extract(dwr, False).reshape(gp, S5_GROUP), _block_diag_extract(dwi, False).reshape(gp, S5_GROUP))
        dlr, dli, dldt, dbr, dbi = _s5_prep_bwd(*s5_raw_in[j], cots, "s5_prep_bwd")
        small_grads["s5_lam_re"].append(dlr.reshape(s5_lam_re[j].shape))
        small_grads["s5_lam_im"].append(dli.reshape(s5_lam_re[j].shape))
        small_grads["s5_log_dt"].append(jnp.sum(dldt.reshape(s5_lam_re[j].shape), axis=1))
        small_grads["s5_b_re"].append(dbr.reshape(shape3))
        small_grads["s5_b_im"].append(dbi.reshape(shape3))
        small_grads["s5_c_re"].append(_block_diag_extract(dcr, True))
        small_grads["s5_c_im"].append(_block_diag_extract(dci, True))
    small_grads = {n: jnp.stack(v) for n, v in small_grads.items()}
    stack = lambda name, count: jnp.stack([grads[name][j] for j in range(count)])
    small_grads["s5_d"] = stack("s5_d", n_s5)[:, 0]
    small_grads["s5_b_glu"] = stack("s5_b_glu", n_s5)[:, 0]
    n_cv = cv_w_pw1.shape[0]
    for name in ("cv_b_pw1", "cv_b_dw", "cv_ln_g", "cv_ln_b", "cv_b_pw2"):
        small_grads[name] = stack(name, n_cv)[:, 0]
    small_grads["cv_w_dw"] = stack("cv_w_dw", n_cv)
    small_grads["norm_g"] = grads["norm_g"]

    small_order = ["norm_g", "s5_lam_re", "s5_lam_im", "s5_log_dt", "s5_b_re", "s5_b_im", "s5_c_re", "s5_c_im", "s5_d",
                   "s5_b_glu", "cv_b_pw1", "cv_w_dw", "cv_b_dw", "cv_ln_g", "cv_ln_b", "cv_b_pw2"]
    small_shapes = [small_grads[n].shape for n in small_order]
    sg_in = _pack([dmod] + [small_grads[n] for n in small_order], F32, SUBLANES)
    sg_all = _all_gather8(sg_in, "gather_small_grads")
    sg_sum = _sum8(sg_all, "sum_small_grads").reshape(-1)
    dmod_all = sg_all.reshape(8, -1)[:, :dmod.size].reshape(8, depth, 4, n_mod)
    g_full = dict(zip(small_order, _unpack(sg_sum[dmod.size:], small_shapes), strict=True))
    g_full["b_mod"] = sg_sum[:dmod.size].reshape(dmod.shape)

    gu = stack("ffn_w_gu", depth)
    f_dim = gu.shape[2] // 2
    big_grads = {"sb_w_qkv": stack("sb_w_qkv", sb_w_qkv.shape[0]), "sb_w_o": stack("sb_w_o", sb_w_o.shape[0]),
                 "s5_w_glu": stack("s5_w_glu", n_s5), "cv_w_pw1": stack("cv_w_pw1", n_cv), "cv_w_pw2": stack("cv_w_pw2", n_cv),
                 "ffn_w_gate": gu[:, :, :f_dim], "ffn_w_up": gu[:, :, f_dim:], "ffn_w_down": stack("ffn_w_down", depth)}
    chip_major = [_shards_from_full(big_grads[n], n) for n in _BIG]
    core_idx, chip_core = pc.reshape(1).astype(I32), jnp.stack([chip, pc]).astype(I32)
    from_sibling = _reduce_to_sibling(chip_major, "reduce_sibling")
    chip_sums = [_add_own_half(g, r, core_idx, "reduce_add_sibling") for g, r in zip(chip_major, from_sibling, strict=True)]
    from_chips = _exchange_chips(chip_sums, "reduce_chips")
    my_half = [_add_chips(g, s, r, chip_core, "reduce_add_chips")
               for g, s, r in zip(chip_major, from_sibling, from_chips, strict=True)]
    g_shard = dict(zip(_BIG, _share_with_sibling(my_half, "reduce_share"), strict=True))

    out_g, out_d, out_m, out_v = {}, {}, {}, {}
    for n in _BIG:
        shp = wts[n].shape
        two = lambda a: a.reshape(-1, shp[-1])
        out_g[n] = g_shard[n]
        d, nm, nv = _adamw(two(wts[n]), two(g_shard[n]), two(mom[n]), two(var[n]), "adamw")
        out_d[n], out_m[n], out_v[n] = d.reshape(shp), nm.reshape(shp), nv.reshape(shp)
    sc_t = jax.nn.silu(c_all).T
    dmod_cols = lax.dynamic_index_in_dim(dmod_all, chip, axis=2, keepdims=False).transpose(1, 0, 2)
    out_g["w_mod"], out_d["w_mod"], out_m["w_mod"], out_v["w_mod"] = _mod_grad_adamw(
        sc_t, dmod_cols, w_mod, m_w_mod, v_w_mod, "w_mod_adamw")
    small_all_names = [n for n in _WEIGHTS if n not in _BIG and n != "w_mod"]
    local_g = []
    for n in small_all_names:
        g = g_full[n]
        if n in _SMALL_SHARDED:
            width = wts[n].shape[-1]
            g = lax.dynamic_slice_in_dim(g, chip * width, width, axis=g.ndim - 1)
        local_g.append(g)
        out_g[n] = g
    pk = lambda arrs: _pack(arrs, F32, SUBLANES)
    d, nm, nv = _adamw(pk([wts[n] for n in small_all_names]), pk(local_g), pk([mom[n] for n in small_all_names]),
                       pk([var[n] for n in small_all_names]), "adamw_small")
    shapes = [wts[n].shape for n in small_all_names]
    for n, a, b, cc in zip(small_all_names, _unpack(d.reshape(-1), shapes), _unpack(nm.reshape(-1), shapes),
                           _unpack(nv.reshape(-1), shapes), strict=True):
        out_d[n], out_m[n], out_v[n] = a, b, cc
    return (loss, grad_x[None], *[out_g[n] for n in _WEIGHTS], *[out_d[n] for n in _WEIGHTS],
            *[out_m[n] for n in _WEIGHTS], *[out_v[n] for n in _WEIGHTS])
```

```python
import functools
import math

import jax
import jax.numpy as jnp
from jax import lax
from jax.experimental import pallas as pl
from jax.experimental.pallas import tpu as pltpu

F32 = jnp.float32
BF16 = jnp.bfloat16
I32 = jnp.int32
SDS = jax.ShapeDtypeStruct
MESH = pl.DeviceIdType.MESH

HEAD_DIM = 64
HEAD_PAIR = 2 * HEAD_DIM
S5_GROUP = 16
S5_STATE = 64
S5_BLOCK_GROUPS = 8
S5_BLOCK_CH = S5_GROUP * S5_BLOCK_GROUPS
S5_BLOCK_ST = S5_STATE * S5_BLOCK_GROUPS
CONV_WIDTH = 31
CONV_HALO = 32
EPS = 1e-6
ADAM_LR = 0.001
ADAM_B1 = 0.9
ADAM_B2 = 0.999
ADAM_EPS = 1e-08
ADAM_WD = 0.01
ADAM_STEP = 10

LANES = 128
SUBLANES = 8
VMEM_LIMIT = 56 * 2 ** 20
PACK_LANES = 512
ATT_BLOCK = 128
EXP_FLOOR = -104.0
ATT_PAIRS = 2
ATT_KEEP = 4
_KEPT = ("log_keep", "log_beta", "weight", "d_exponent")

_COUNTER = [0]


def _name(base):
    _COUNTER[0] += 1
    return f"{base}_{_COUNTER[0]}"


def _tile(dim, target, mult=LANES):
    t = min(dim, target)
    t -= t % mult
    while t > mult and dim % t:
        t -= mult
    return t if t > 0 and dim % t == 0 else dim


def _params(sem=None):
    return pltpu.CompilerParams(dimension_semantics=sem, vmem_limit_bytes=VMEM_LIMIT)


_DIMS = {"nn": (((1,), (0,)), ((), ())), "nt": (((1,), (1,)), ((), ())), "tn": (((0,), (0,)), ((), ()))}


def _matmul(a, b, form, out_dtype, base, tm=1536, tn=1536, tk=1536):
    if form == "tn":
        k_dim, m_dim = a.shape
    else:
        m_dim, k_dim = a.shape
    n_dim = b.shape[0] if form == "nt" else b.shape[1]
    tm, tn, tk = _tile(m_dim, tm), _tile(n_dim, tn), _tile(k_dim, tk)
    nk = k_dim // tk
    dims = _DIMS[form]

    def body(a_ref, b_ref, o_ref, *acc):
        part = lax.dot_general(a_ref[...].astype(BF16), b_ref[...].astype(BF16), dims, preferred_element_type=F32)
        if nk == 1:
            o_ref[...] = part.astype(o_ref.dtype)
            return
        acc_ref, = acc
        k = pl.program_id(2)

        @pl.when(k == 0)
        def _():
            acc_ref[...] = part

        @pl.when(jnp.logical_and(k > 0, k < nk - 1))
        def _():
            acc_ref[...] += part

        @pl.when(k == nk - 1)
        def _():
            o_ref[...] = (acc_ref[...] + part).astype(o_ref.dtype)

    a_spec = (pl.BlockSpec((tk, tm), lambda i, j, k: (k, i)) if form == "tn"
              else pl.BlockSpec((tm, tk), lambda i, j, k: (i, k)))
    b_spec = (pl.BlockSpec((tn, tk), lambda i, j, k: (j, k)) if form == "nt"
              else pl.BlockSpec((tk, tn), lambda i, j, k: (k, j)))
    return pl.pallas_call(
        body, name=_name(base), grid=(m_dim // tm, n_dim // tn, nk),
        in_specs=[a_spec, b_spec], out_specs=pl.BlockSpec((tm, tn), lambda i, j, k: (i, j)),
        out_shape=SDS((m_dim, n_dim), out_dtype), scratch_shapes=[pltpu.VMEM((tm, tn), F32)] if nk > 1 else [],
        compiler_params=_params(("parallel", "parallel", "arbitrary")),
    )(a, b)


def _rowwise(fn, rows, vecs, out_rows, out_vecs, base, tr=256):
    s_dim = rows[0].shape[0]
    tr = _tile(s_dim, tr, SUBLANES)
    nr, nv, nor = len(rows), len(vecs), len(out_rows)

    def body(*refs):
        r_in, v_in = refs[:nr], refs[nr:nr + nv]
        r_out, v_out = refs[nr + nv:nr + nv + nor], refs[nr + nv + nor:]
        ro, vo = fn(*[r[...] for r in r_in], *[v[...] for v in v_in])
        for ref, val in zip(r_out, ro, strict=True):
            ref[...] = val.astype(ref.dtype)
        if v_out:
            i = pl.program_id(0)

            @pl.when(i == 0)
            def _():
                for ref, val in zip(v_out, vo, strict=True):
                    ref[...] = val.astype(ref.dtype)

            @pl.when(i > 0)
            def _():
                for ref, val in zip(v_out, vo, strict=True):
                    ref[...] += val.astype(ref.dtype)

    in_specs = [pl.BlockSpec((tr, r.shape[1]), lambda i: (i, 0)) for r in rows]
    in_specs += [pl.BlockSpec(v.shape, lambda i: (0, 0)) for v in vecs]
    out_specs = [pl.BlockSpec((tr, o.shape[1]), lambda i: (i, 0)) for o in out_rows]
    out_specs += [pl.BlockSpec(o.shape, lambda i: (0, 0)) for o in out_vecs]
    outs = pl.pallas_call(
        body, name=_name(base), grid=(s_dim // tr,), in_specs=in_specs, out_specs=out_specs,
        out_shape=list(out_rows) + list(out_vecs),
        compiler_params=_params(("arbitrary",)),
    )(*rows, *vecs)
    return outs


def _f(x):
    return x.astype(F32)


def _rms(x, g):
    return x * lax.rsqrt(jnp.mean(x * x, axis=-1, keepdims=True) + EPS) * g


def _pre(h, g, sh, sc):
    return _rms(h, g) * (1 + sc) + sh


def _post(h, m, gate, g):
    return h + gate * _rms(m, g)


def _halves(t):
    d = t.shape[1] // 2
    return t[:, :d], t[:, d:]


def _glu2(ta, tb, ba, bb):
    return (ta + ba) * jax.nn.sigmoid(tb + bb)


def _glu(t, b):
    return _glu2(*_halves(t), *_halves(b))


def _glu_bwd(t, dout, b):
    _, vjp = jax.vjp(_glu2, *_halves(t), *_halves(b))
    dta, dtb, dba, dbb = vjp(dout)
    return jnp.concatenate([dta, dtb], axis=1), jnp.concatenate([dba, dbb], axis=1)


def _swish2(g, u):
    return jax.nn.silu(g) * u


def _swish_gate(gu):
    return _swish2(*_halves(gu))


def _ln_silu(h, g, b):
    mu = jnp.mean(h, axis=-1, keepdims=True)
    var = jnp.mean(jnp.square(h - mu), axis=-1, keepdims=True)
    return jax.nn.silu((h - mu) * lax.rsqrt(var + EPS) * g + b)


def _dot(a, b, form="nn"):
    return lax.dot_general(a, b, _DIMS[form], preferred_element_type=F32)


def _split_dot(x, tri):
    hi = x.astype(BF16)
    lo = (x - hi.astype(F32)).astype(BF16)
    return _dot(hi, tri) + _dot(lo, tri)


def _sb_scores(qb, kb, scale, mask):
    z = _dot(qb, kb, "nt") * scale
    sp = jnp.maximum(z, 0.0) + jnp.log(1.0 + jnp.exp(-jnp.abs(z)))
    lk = -sp
    if mask is not None:
        lk = jnp.where(mask, lk, 0.0)
    return lk, z - sp


def _walk_left(i, state, step):
    def cond(ns):
        live = functools.reduce(jnp.maximum, [jnp.max(c) for c in ns[1][0]])
        return jnp.logical_and(ns[0] < i, live > EXP_FLOOR)

    def body(ns):
        return ns[0] + 1, step(i - 1 - ns[0], ns[1])

    return lax.while_loop(cond, body, (jnp.int32(0), state))[1]


def _pair_split(x):
    first = lax.broadcasted_iota(I32, x.shape, 1) < HEAD_DIM
    zero = jnp.zeros_like(x)
    return (jnp.where(first, x, zero), jnp.where(first, zero, x)), first


def _group_specs(s_dim, d_dim, bq):
    pairs = ATT_PAIRS if d_dim % (ATT_PAIRS * HEAD_PAIR) == 0 else 1
    width = pairs * HEAD_PAIR
    ngroup = d_dim // width
    blk = pl.BlockSpec((bq, width), lambda p, i: (i, p))
    k_all = pl.BlockSpec((s_dim, width), lambda p, i: (0, ngroup + p))
    v_all = pl.BlockSpec((s_dim, width), lambda p, i: (0, 2 * ngroup + p))
    return pairs, ngroup, blk, k_all, v_all


def _pair_lanes(g):
    return slice(g * HEAD_PAIR, (g + 1) * HEAD_PAIR)


def _sb_fwd(qkv, base):
    s_dim, d_dim = qkv.shape[0], qkv.shape[1] // 3
    bq = ATT_BLOCK
    scale = HEAD_DIM ** -0.5
    pairs, ngroup, blk, k_all, v_all = _group_specs(s_dim, d_dim, bq)

    def body(q_ref, k_ref, v_ref, o_ref):
        i = pl.program_id(1)
        split = [_pair_split(q_ref[:, _pair_lanes(g)]) for g in range(pairs)]
        qs = [q for pair, _ in split for q in pair]
        first = split[0][1]
        row = lax.broadcasted_iota(I32, (bq, bq), 0)
        col = lax.broadcasted_iota(I32, (bq, bq), 1)
        later = (row > col).astype(BF16)
        causal = col < row

        def tile(j, cs, accs, mask):
            ks = pl.ds(pl.multiple_of(j * bq, bq), bq)
            new_c, new_acc = [], []
            for h, (qh, c, acc) in enumerate(zip(qs, cs, accs, strict=True)):
                lanes = _pair_lanes(h // 2)
                lk, lb = _sb_scores(qh, k_ref[ks, lanes], scale, mask)
                w = jnp.exp(lb + _split_dot(lk, later) + c)
                if mask is not None:
                    w = jnp.where(mask, w, 0.0)
                new_acc.append(acc + _dot(w.astype(BF16), v_ref[ks, lanes]))
                new_c.append(c + jnp.sum(lk, axis=1, keepdims=True))
            return tuple(new_c), tuple(new_acc)

        zero_c, zero_acc = jnp.zeros((bq, 1), F32), jnp.zeros((bq, HEAD_PAIR), F32)
        cs, accs = tile(i, (zero_c,) * len(qs), (zero_acc,) * len(qs), causal)
        cs, accs = _walk_left(i, (cs, accs), lambda j, s: tile(j, s[0], s[1], None))
        for g in range(pairs):
            o_ref[:, _pair_lanes(g)] = jnp.where(first, accs[2 * g], accs[2 * g + 1]).astype(o_ref.dtype)

    return pl.pallas_call(
        body, name=_name(base), grid=(ngroup, s_dim // bq), in_specs=[blk, k_all, v_all], out_specs=blk,
        out_shape=SDS((s_dim, d_dim), BF16), compiler_params=_params(("parallel", "arbitrary")),
    )(qkv, qkv, qkv)


def _sb_bwd(qkv, do, base):
    s_dim, d_dim = do.shape
    bq = ATT_BLOCK
    scale = HEAD_DIM ** -0.5
    pairs, ngroup, blk, k_all, v_all = _group_specs(s_dim, d_dim, bq)
    width = pairs * HEAD_PAIR
    nq = s_dim // bq

    def body(q_ref, k_ref, v_ref, do_ref, dq_ref, dk_hbm, dv_hbm, dk_ref, dv_ref, kept, out_sems):
        p, i = pl.program_id(0), pl.program_id(1)
        q_split = [_pair_split(q_ref[:, _pair_lanes(g)]) for g in range(pairs)]
        qs = [q for pair, _ in q_split for q in pair]
        dos = [d for g in range(pairs) for d in _pair_split(do_ref[:, _pair_lanes(g)])[0]]
        first = q_split[0][1]
        heads = tuple(range(2 * pairs))
        row = lax.broadcasted_iota(I32, (bq, bq), 0)
        col = lax.broadcasted_iota(I32, (bq, bq), 1)
        later = (row > col).astype(BF16)
        from_here = (row >= col).astype(BF16)
        causal = col < row

        @pl.when(i == 0)
        def _():
            dk_ref[...] = jnp.zeros_like(dk_ref)
            dv_ref[...] = jnp.zeros_like(dv_ref)

        def weights(h, kb, vb, c, mask):
            lk, lb = _sb_scores(qs[h], kb, scale, mask)
            w = jnp.exp(lb + _split_dot(lk, later) + c)
            if mask is not None:
                w = jnp.where(mask, w, 0.0)
            return lk, lb, w, _dot(dos[h], vb, "nt") * w

        def slot(h, n):
            return (h * ATT_KEEP + n) * len(_KEPT)

        def total(j, cs, tots, mask):
            ks = pl.ds(pl.multiple_of(j * bq, bq), bq)
            new_c, new_tot = [], []
            for h in heads:
                lanes = _pair_lanes(h // 2)
                tile = weights(h, k_ref[ks, lanes], v_ref[ks, lanes], cs[h], mask)

                @pl.when(i - j < ATT_KEEP)
                def _(tile=tile, h=h):
                    for a, val in enumerate(tile):
                        kept[slot(h, i - j) + a] = val

                new_c.append(cs[h] + jnp.sum(tile[0], axis=1, keepdims=True))
                new_tot.append(tots[h] + jnp.sum(tile[3], axis=1, keepdims=True))
            return tuple(new_c), tuple(new_tot)

        zero = jnp.zeros((bq, 1), F32)
        zeros = (zero,) * len(heads)
        cs, tots = total(i, zeros, zeros, causal)
        cs, tots = _walk_left(i, (cs, tots), lambda j, s: total(j, s[0], s[1], None))

        def grads(j, cs, rs, dqs, mask):
            ks = pl.ds(pl.multiple_of(j * bq, bq), bq)
            new_c, new_r, new_dq = [], [], []
            dk_t, dv_t = [None] * pairs, [None] * pairs
            for h in heads:
                g, lanes = h // 2, _pair_lanes(h // 2)
                kb = k_ref[ks, lanes]
                lk, lb, w, de = lax.cond(
                    i - j < ATT_KEEP,
                    lambda h=h: tuple(kept[slot(h, i - j) + a] for a in range(len(_KEPT))),
                    lambda h=h, kb=kb, lanes=lanes: weights(h, kb, v_ref[ks, lanes], cs[h], mask))
                before = tots[h] - (rs[h] + _split_dot(de, from_here))
                dz = de * jnp.exp(lk) - jnp.exp(lb) * before
                if mask is not None:
                    dz = jnp.where(mask, dz, 0.0)
                dzb = (dz * scale).astype(BF16)
                dk_h, dv_h = _dot(dzb, qs[h], "tn"), _dot(w.astype(BF16), dos[h], "tn")
                dk_t[g], dv_t[g] = (dk_h, dv_h) if dk_t[g] is None else (dk_t[g] + dk_h, dv_t[g] + dv_h)
                new_dq.append(dqs[h] + _dot(dzb, kb))
                new_c.append(cs[h] + jnp.sum(lk, axis=1, keepdims=True))
                new_r.append(rs[h] + jnp.sum(de, axis=1, keepdims=True))
            for g in range(pairs):
                dk_ref[ks, _pair_lanes(g)] += dk_t[g]
                dv_ref[ks, _pair_lanes(g)] += dv_t[g]
            return tuple(new_c), tuple(new_r), tuple(new_dq)

        zero_dq = jnp.zeros((bq, HEAD_PAIR), F32)
        state = grads(i, zeros, zeros, (zero_dq,) * len(heads), causal)
        _, _, dqs = _walk_left(i, state, lambda j, s: grads(j, s[0], s[1], s[2], None))
        for g in range(pairs):
            dq_ref[:, _pair_lanes(g)] = jnp.where(first, dqs[2 * g], dqs[2 * g + 1])

        @pl.when(i == nq - 1)
        def _():
            cols = pl.ds(pl.multiple_of(p * width, width), width)
            out = [pltpu.make_async_copy(dk_ref, dk_hbm.at[:, cols], out_sems.at[0]),
                   pltpu.make_async_copy(dv_ref, dv_hbm.at[:, cols], out_sems.at[1])]
            for cp in out:
                cp.start()
            for cp in out:
                cp.wait()

    out = SDS((s_dim, d_dim), F32)
    acc = pltpu.VMEM((s_dim, width), F32)
    return pl.pallas_call(
        body, name=_name(base), grid=(ngroup, nq), in_specs=[blk, k_all, v_all, blk],
        out_specs=[blk, _ANY, _ANY], out_shape=[out, out, out],
        scratch_shapes=[acc, acc, pltpu.VMEM((2 * pairs * ATT_KEEP * len(_KEPT), bq, bq), F32),
                        pltpu.SemaphoreType.DMA((2,))],
        compiler_params=_params(("parallel", "arbitrary")),
    )(qkv, qkv, qkv, do)


def _cmul(ar, ai, br, bi):
    return ar * br - ai * bi, ar * bi + ai * br


def _s5_prep_fn(lr, li, ldt, br, bi):
    dt = jnp.exp(ldt)
    mag = jnp.exp(lr * dt)
    ar, ai = mag * jnp.cos(li * dt), mag * jnp.sin(li * dt)
    den = lr * lr + li * li
    er = ((ar - 1) * lr + ai * li) / den
    ei = (ai * lr - (ar - 1) * li) / den
    return ar, ai, er * br - ei * bi, er * bi + ei * br


def _s5_prep(lr, li, ldt, br, bi, base):
    col, mat = SDS(lr.shape, F32), SDS(br.shape, F32)
    return _rowwise(lambda *a: (_s5_prep_fn(*a), ()), [lr, li, ldt, br, bi], [], [col, col, mat, mat], [], base, tr=512)


def _s5_prep_bwd(lr, li, ldt, br, bi, cots, base):
    col, mat = SDS(lr.shape, F32), SDS(br.shape, F32)

    def bwd(lr, li, ldt, br, bi, c0, c1, c2, c3):
        _, vjp = jax.vjp(_s5_prep_fn, lr, li, ldt, br, bi)
        return vjp((c0, c1, c2, c3)), ()

    return _rowwise(bwd, [lr, li, ldt, br, bi, *cots], [], [col, col, col, mat, mat], [], base, tr=512)


def _scan_tables(ar_ref, ai_ref, sl, conj):
    shape = (SUBLANES, S5_BLOCK_ST)
    a_r = jnp.broadcast_to(ar_ref[:, sl], shape)
    a_i = jnp.broadcast_to(ai_ref[:, sl], shape)
    if conj:
        a_i = -a_i
    a2 = _cmul(a_r, a_i, a_r, a_i)
    a4 = _cmul(*a2, *a2)
    row = lax.broadcasted_iota(I32, shape, 0)
    p_r, p_i, t_r, t_i = a_r, a_i, a_r, a_i
    for k in range(1, SUBLANES):
        p_r, p_i = _cmul(p_r, p_i, a_r, a_i)
        sel = (row == (SUBLANES - 1 - k)) if conj else (row == k)
        t_r, t_i = jnp.where(sel, p_r, t_r), jnp.where(sel, p_i, t_i)
    if conj:
        t_r, t_i = jnp.where(row == SUBLANES - 1, a_r, t_r), jnp.where(row == SUBLANES - 1, a_i, t_i)
    return row, (a_r, a_i), a2, a4, (t_r, t_i)


def _scan_group(xr, xi, row, a1, a2, a4, table, cr, ci, reverse):
    for s, (mr, mi) in ((1, a1), (2, a2), (4, a4)):
        if reverse:
            keep = row < SUBLANES - s
            sr, si = pltpu.roll(xr, SUBLANES - s, 0), pltpu.roll(xi, SUBLANES - s, 0)
        else:
            keep = row >= s
            sr, si = pltpu.roll(xr, s, 0), pltpu.roll(xi, s, 0)
        sr, si = jnp.where(keep, sr, 0.0), jnp.where(keep, si, 0.0)
        xr, xi = xr + mr * sr - mi * si, xi + mr * si + mi * sr
    tr, ti = table
    return xr + tr * cr - ti * ci, xi + tr * ci + ti * cr


def _s5_fwd(u, ar, ai, wr, wi, cr_w, ci_w, dsk, base, chunk=128):
    s_dim, d_dim = u.shape
    nb = d_dim // S5_BLOCK_CH
    n_st = nb * S5_BLOCK_ST
    chunk = _tile(s_dim, chunk, SUBLANES)
    nch = s_dim // chunk

    def body(u_ref, ar_ref, ai_ref, wr_ref, wi_ref, cr_ref, ci_ref, d_ref, y_ref, gy_ref, sr_ref, si_ref,
             xr_s, xi_s, car_r, car_i):
        @pl.when(pl.program_id(0) == 0)
        def _():
            car_r[...] = jnp.zeros_like(car_r)
            car_i[...] = jnp.zeros_like(car_i)

        sr_ref[...] = car_r[...]
        si_ref[...] = car_i[...]
        for b in range(nb):
            ub = u_ref[:, b * S5_BLOCK_CH:(b + 1) * S5_BLOCK_CH].astype(BF16)
            sl = slice(b * S5_BLOCK_ST, (b + 1) * S5_BLOCK_ST)
            xr_s[:, sl] = _dot(ub, wr_ref[b])
            xi_s[:, sl] = _dot(ub, wi_ref[b])
        for b in range(nb):
            sl = slice(b * S5_BLOCK_ST, (b + 1) * S5_BLOCK_ST)
            row, a1, a2, a4, table = _scan_tables(ar_ref, ai_ref, sl, False)

            def group(g, carry, sl=sl, row=row, a1=a1, a2=a2, a4=a4, table=table):
                rows = pl.ds(pl.multiple_of(g * SUBLANES, SUBLANES), SUBLANES)
                xr, xi = _scan_group(xr_s[rows, sl], xi_s[rows, sl], row, a1, a2, a4, table, carry[0], carry[1], False)
                xr_s[rows, sl] = xr
                xi_s[rows, sl] = xi
                last = (SUBLANES, S5_BLOCK_ST)
                return (jnp.broadcast_to(xr[SUBLANES - 1:, :], last), jnp.broadcast_to(xi[SUBLANES - 1:, :], last))

            c_r, c_i = lax.fori_loop(0, chunk // SUBLANES, group, (car_r[:, sl], car_i[:, sl]))
            car_r[:, sl] = c_r
            car_i[:, sl] = c_i
        for b in range(nb):
            sl = slice(b * S5_BLOCK_ST, (b + 1) * S5_BLOCK_ST)
            ch = slice(b * S5_BLOCK_CH, (b + 1) * S5_BLOCK_CH)
            y = _dot(xr_s[:, sl].astype(BF16), cr_ref[b]) - _dot(xi_s[:, sl].astype(BF16), ci_ref[b])
            y = y + d_ref[:, ch] * u_ref[:, ch]
            y_ref[:, ch] = y
            gy_ref[:, ch] = jax.nn.gelu(y).astype(BF16)

    rows = pl.BlockSpec((chunk, d_dim), lambda i: (i, 0))
    vec = pl.BlockSpec((1, n_st), lambda i: (0, 0))
    w_in = pl.BlockSpec((nb, S5_BLOCK_CH, S5_BLOCK_ST), lambda i: (0, 0, 0))
    w_out = pl.BlockSpec((nb, S5_BLOCK_ST, S5_BLOCK_CH), lambda i: (0, 0, 0))
    st = pl.BlockSpec((SUBLANES, n_st), lambda i: (i, 0))
    return pl.pallas_call(
        body, name=_name(base), grid=(nch,),
        in_specs=[rows, vec, vec, w_in, w_in, w_out, w_out, pl.BlockSpec((1, d_dim), lambda i: (0, 0))],
        out_specs=[rows, rows, st, st],
        out_shape=[SDS((s_dim, d_dim), F32), SDS((s_dim, d_dim), BF16),
                   SDS((nch * SUBLANES, n_st), F32), SDS((nch * SUBLANES, n_st), F32)],
        scratch_shapes=[pltpu.VMEM((chunk, n_st), F32), pltpu.VMEM((chunk, n_st), F32),
                        pltpu.VMEM((SUBLANES, n_st), F32), pltpu.VMEM((SUBLANES, n_st), F32)],
        compiler_params=_params(("arbitrary",)),
    )(u, ar, ai, wr, wi, cr_w, ci_w, dsk)


def _s5_bwd(u, dy, st_r, st_i, ar, ai, wr, wi, cr_w, ci_w, dsk, base, chunk):
    s_dim, d_dim = u.shape
    nb = d_dim // S5_BLOCK_CH
    n_st = nb * S5_BLOCK_ST
    nch = s_dim // chunk
    ng = chunk // SUBLANES

    def body(u_ref, dy_ref, sr_ref, si_ref, ar_ref, ai_ref, wr_ref, wi_ref, cr_ref, ci_ref, d_ref,
             du_ref, dwr_ref, dwi_ref, dcr_ref, dci_ref, dar_ref, dai_ref, dd_ref,
             xr_s, xi_s, lr_s, li_s, car_r, car_i):
        first = pl.program_id(0) == 0

        @pl.when(first)
        def _():
            car_r[...] = jnp.zeros_like(car_r)
            car_i[...] = jnp.zeros_like(car_i)
            for ref in (dwr_ref, dwi_ref, dcr_ref, dci_ref, dar_ref, dai_ref, dd_ref):
                ref[...] = jnp.zeros_like(ref)

        xr_s[pl.ds(0, SUBLANES), :] = sr_ref[...]
        xi_s[pl.ds(0, SUBLANES), :] = si_ref[...]
        body_rows = pl.ds(SUBLANES, chunk)
        dyb = dy_ref[...]
        dd_ref[...] += jnp.sum(dyb * u_ref[...], axis=0, keepdims=True)
        for b in range(nb):
            ch = slice(b * S5_BLOCK_CH, (b + 1) * S5_BLOCK_CH)
            sl = slice(b * S5_BLOCK_ST, (b + 1) * S5_BLOCK_ST)
            ub = u_ref[:, ch].astype(BF16)
            dyc = dyb[:, ch].astype(BF16)
            xr_s[body_rows, sl] = _dot(ub, wr_ref[b])
            xi_s[body_rows, sl] = _dot(ub, wi_ref[b])
            lr_s[:, sl] = _dot(dyc, cr_ref[b], "nt")
            li_s[:, sl] = -_dot(dyc, ci_ref[b], "nt")
        for b in range(nb):
            sl = slice(b * S5_BLOCK_ST, (b + 1) * S5_BLOCK_ST)
            row, a1, a2, a4, table = _scan_tables(ar_ref, ai_ref, sl, False)

            def fwd_group(g, carry, sl=sl, row=row, a1=a1, a2=a2, a4=a4, table=table):
                rows = pl.ds(pl.multiple_of((g + 1) * SUBLANES, SUBLANES), SUBLANES)
                xr, xi = _scan_group(xr_s[rows, sl], xi_s[rows, sl], row, a1, a2, a4, table, carry[0], carry[1], False)
                xr_s[rows, sl] = xr
                xi_s[rows, sl] = xi
                last = (SUBLANES, S5_BLOCK_ST)
                return (jnp.broadcast_to(xr[SUBLANES - 1:, :], last), jnp.broadcast_to(xi[SUBLANES - 1:, :], last))

            lax.fori_loop(0, ng, fwd_group, (sr_ref[:, sl], si_ref[:, sl]))
            row, a1, a2, a4, table = _scan_tables(ar_ref, ai_ref, sl, True)

            def bwd_group(n, carry, sl=sl, row=row, a1=a1, a2=a2, a4=a4, table=table):
                g = ng - 1 - n
                c_r, c_i, acc_r, acc_i = carry
                rows = pl.ds(pl.multiple_of(g * SUBLANES, SUBLANES), SUBLANES)
                lr, li = _scan_group(lr_s[rows, sl], li_s[rows, sl], row, a1, a2, a4, table, c_r, c_i, True)
                lr_s[rows, sl] = lr
                li_s[rows, sl] = li
                prev = pl.ds(pl.multiple_of(g * SUBLANES, SUBLANES), SUBLANES)
                here = pl.ds(pl.multiple_of((g + 1) * SUBLANES, SUBLANES), SUBLANES)
                px_r = jnp.where(row >= 1, pltpu.roll(xr_s[here, sl], 1, 0), pltpu.roll(xr_s[prev, sl], 1, 0))
                px_i = jnp.where(row >= 1, pltpu.roll(xi_s[here, sl], 1, 0), pltpu.roll(xi_s[prev, sl], 1, 0))
                acc_r = acc_r + lr * px_r + li * px_i
                acc_i = acc_i + li * px_r - lr * px_i
                first_row = (SUBLANES, S5_BLOCK_ST)
                return (jnp.broadcast_to(lr[:1, :], first_row), jnp.broadcast_to(li[:1, :], first_row), acc_r, acc_i)

            zero = jnp.zeros((SUBLANES, S5_BLOCK_ST), F32)
            c_r, c_i, acc_r, acc_i = lax.fori_loop(0, ng, bwd_group, (car_r[:, sl], car_i[:, sl], zero, zero))
            car_r[:, sl] = c_r
            car_i[:, sl] = c_i
            dar_ref[:, sl] += acc_r
            dai_ref[:, sl] += acc_i
        for b in range(nb):
            ch = slice(b * S5_BLOCK_CH, (b + 1) * S5_BLOCK_CH)
            sl = slice(b * S5_BLOCK_ST, (b + 1) * S5_BLOCK_ST)
            ub = u_ref[:, ch].astype(BF16)
            dyc = dyb[:, ch].astype(BF16)
            lrb, lib = lr_s[:, sl].astype(BF16), li_s[:, sl].astype(BF16)
            du_ref[:, ch] = (_dot(lrb, wr_ref[b], "nt") + _dot(lib, wi_ref[b], "nt") + d_ref[:, ch] * dyb[:, ch])
            dwr_ref[b] += _dot(ub, lrb, "tn")
            dwi_ref[b] += _dot(ub, lib, "tn")
            dcr_ref[b] += _dot(xr_s[body_rows, sl].astype(BF16), dyc, "tn")
            dci_ref[b] -= _dot(xi_s[body_rows, sl].astype(BF16), dyc, "tn")

    rev = lambda i: (nch - 1 - i, 0)
    rows = pl.BlockSpec((chunk, d_dim), rev)
    st = pl.BlockSpec((SUBLANES, n_st), rev)
    vec = pl.BlockSpec((1, n_st), lambda i: (0, 0))
    w_in = pl.BlockSpec((nb, S5_BLOCK_CH, S5_BLOCK_ST), lambda i: (0, 0, 0))
    w_out = pl.BlockSpec((nb, S5_BLOCK_ST, S5_BLOCK_CH), lambda i: (0, 0, 0))
    acc8 = pl.BlockSpec((SUBLANES, n_st), lambda i: (0, 0))
    dvec = pl.BlockSpec((1, d_dim), lambda i: (0, 0))
    return pl.pallas_call(
        body, name=_name(base), grid=(nch,),
        in_specs=[rows, rows, st, st, vec, vec, w_in, w_in, w_out, w_out, dvec],
        out_specs=[rows, w_in, w_in, w_out, w_out, acc8, acc8, dvec],
        out_shape=[SDS((s_dim, d_dim), F32),
                   SDS((nb, S5_BLOCK_CH, S5_BLOCK_ST), F32), SDS((nb, S5_BLOCK_CH, S5_BLOCK_ST), F32),
                   SDS((nb, S5_BLOCK_ST, S5_BLOCK_CH), F32), SDS((nb, S5_BLOCK_ST, S5_BLOCK_CH), F32),
                   SDS((SUBLANES, n_st), F32), SDS((SUBLANES, n_st), F32), SDS((1, d_dim), F32)],
        scratch_shapes=[pltpu.VMEM((chunk + SUBLANES, n_st), F32), pltpu.VMEM((chunk + SUBLANES, n_st), F32),
                        pltpu.VMEM((chunk, n_st), F32), pltpu.VMEM((chunk, n_st), F32),
                        pltpu.VMEM((SUBLANES, n_st), F32), pltpu.VMEM((SUBLANES, n_st), F32)],
        compiler_params=_params(("arbitrary",)),
    )(u, dy, st_r, st_i, ar, ai, wr, wi, cr_w, ci_w, dsk)


def _block_diag(w, transpose):
    g = w.shape[0]
    nb = g // S5_BLOCK_GROUPS
    eye = jnp.eye(S5_BLOCK_GROUPS, dtype=w.dtype)
    if transpose:
        w = w.reshape(nb, S5_BLOCK_GROUPS, S5_GROUP, S5_STATE)
        return jnp.einsum("bgcp,gh->bgphc", w, eye).reshape(nb, S5_BLOCK_ST, S5_BLOCK_CH)
    w = w.reshape(nb, S5_BLOCK_GROUPS, S5_STATE, S5_GROUP)
    return jnp.einsum("bgpc,gh->bgchp", w, eye).reshape(nb, S5_BLOCK_CH, S5_BLOCK_ST)


def _block_diag_extract(w, transpose):
    nb = w.shape[0]
    eye = jnp.eye(S5_BLOCK_GROUPS, dtype=w.dtype)
    if transpose:
        w = w.reshape(nb, S5_BLOCK_GROUPS, S5_STATE, S5_BLOCK_GROUPS, S5_GROUP)
        return jnp.einsum("bgphc,gh->bgcp", w, eye).reshape(nb * S5_BLOCK_GROUPS, S5_GROUP, S5_STATE)
    w = w.reshape(nb, S5_BLOCK_GROUPS, S5_GROUP, S5_BLOCK_GROUPS, S5_STATE)
    return jnp.einsum("bgchp,gh->bgpc", w, eye).reshape(nb * S5_BLOCK_GROUPS, S5_STATE, S5_GROUP)


def _dwconv_fwd(h, w, b, base, tr=512):
    s_dim, d_dim = h.shape
    tr = _tile(s_dim, tr, CONV_HALO)
    per = tr // CONV_HALO

    def body(h_ref, halo_ref, w_ref, b_ref, o_ref, buf):
        i = pl.program_id(0)
        buf[pl.ds(0, CONV_HALO), :] = jnp.where(i > 0, halo_ref[...], 0.0)
        buf[pl.ds(CONV_HALO, tr), :] = h_ref[...]
        acc = jnp.broadcast_to(b_ref[...], (tr, d_dim))
        for k in range(CONV_WIDTH):
            acc = acc + w_ref[pl.ds(k, 1), :] * buf[pl.ds(CONV_HALO - (CONV_WIDTH - 1) + k, tr), :]
        o_ref[...] = acc

    return pl.pallas_call(
        body, name=_name(base), grid=(s_dim // tr,),
        in_specs=[pl.BlockSpec((tr, d_dim), lambda i: (i, 0)),
                  pl.BlockSpec((CONV_HALO, d_dim), lambda i: (jnp.maximum(i * per - 1, 0), 0)),
                  pl.BlockSpec((CONV_WIDTH, d_dim), lambda i: (0, 0)), pl.BlockSpec((1, d_dim), lambda i: (0, 0))],
        out_specs=pl.BlockSpec((tr, d_dim), lambda i: (i, 0)), out_shape=SDS((s_dim, d_dim), F32),
        scratch_shapes=[pltpu.VMEM((tr + CONV_HALO, d_dim), F32)], compiler_params=_params(("arbitrary",)),
    )(h, h, w, b)


def _dwconv_bwd(h, dout, w, base, tr=512):
    s_dim, d_dim = h.shape
    tr = _tile(s_dim, tr, CONV_HALO)
    per = tr // CONV_HALO
    n = s_dim // tr
    last_halo = s_dim // CONV_HALO - 1

    def body(h_ref, hhalo_ref, d_ref, dhalo_ref, w_ref, dh_ref, dw_ref, db_ref, hbuf, dbuf):
        i = pl.program_id(0)
        hbuf[pl.ds(0, CONV_HALO), :] = jnp.where(i > 0, hhalo_ref[...], 0.0)
        hbuf[pl.ds(CONV_HALO, tr), :] = h_ref[...]
        dbuf[pl.ds(0, tr), :] = d_ref[...]
        dbuf[pl.ds(tr, CONV_HALO), :] = jnp.where(i < n - 1, dhalo_ref[...], 0.0)
        dout_t = d_ref[...]
        acc = jnp.zeros((tr, d_dim), F32)
        dws = []
        for k in range(CONV_WIDTH):
            acc = acc + w_ref[pl.ds(k, 1), :] * dbuf[pl.ds(CONV_WIDTH - 1 - k, tr), :]
            shifted = hbuf[pl.ds(CONV_HALO - (CONV_WIDTH - 1) + k, tr), :]
            dws.append(jnp.sum(dout_t * shifted, axis=0, keepdims=True))
        dh_ref[...] = acc
        dw_t = jnp.concatenate(dws, axis=0)
        db_t = jnp.sum(dout_t, axis=0, keepdims=True)

        @pl.when(i == 0)
        def _():
            dw_ref[...] = dw_t
            db_ref[...] = db_t

        @pl.when(i > 0)
        def _():
            dw_ref[...] += dw_t
            db_ref[...] += db_t

    tile = pl.BlockSpec((tr, d_dim), lambda i: (i, 0))
    before = pl.BlockSpec((CONV_HALO, d_dim), lambda i: (jnp.maximum(i * per - 1, 0), 0))
    after = pl.BlockSpec((CONV_HALO, d_dim), lambda i: (jnp.minimum((i + 1) * per, last_halo), 0))
    return pl.pallas_call(
        body, name=_name(base), grid=(n,),
        in_specs=[tile, before, tile, after, pl.BlockSpec((CONV_WIDTH, d_dim), lambda i: (0, 0))],
        out_specs=[tile, pl.BlockSpec((CONV_WIDTH, d_dim), lambda i: (0, 0)), pl.BlockSpec((1, d_dim), lambda i: (0, 0))],
        out_shape=[SDS((s_dim, d_dim), F32), SDS((CONV_WIDTH, d_dim), F32), SDS((1, d_dim), F32)],
        scratch_shapes=[pltpu.VMEM((tr + CONV_HALO, d_dim), F32), pltpu.VMEM((tr + CONV_HALO, d_dim), F32)],
        compiler_params=_params(("arbitrary",)),
    )(h, h, dout, dout, w)


def _local_step(x, mod, w, target):
    s_dim, d_dim = x.shape
    depth = mod.shape[0]
    row = lambda dt=F32, n=d_dim: SDS((s_dim, n), dt)
    vec = lambda n=d_dim: SDS((1, n), F32)
    mods = [[mod[l:l + 1, k * d_dim:(k + 1) * d_dim] for k in range(6)] for l in range(depth)]
    ng = lambda l, k: w["norm_g"][l, k:k + 1, :]
    saved = []
    h = x
    for l in range(depth):
        sh_m, sc_m, g_m, sh_f, sc_f, g_f = mods[l]
        kind, j = l % 3, l // 3
        s = {"h0": h}
        if kind == 1:
            (u,) = _rowwise(lambda h, g, a, b: ((_pre(h, g, a, b),), ()), [h], [ng(l, 0), sh_m, sc_m], [row()], [], "pre")
        else:
            (u,) = _rowwise(lambda h, g, a, b: ((_pre(h, g, a, b),), ()), [h], [ng(l, 0), sh_m, sc_m], [row(BF16)], [], "pre")
        s["u"] = u
        bias = None
        if kind == 0:
            qkv = _matmul(u, w["sb_w_qkv"][j], "nn", BF16, "qkv")
            o = _sb_fwd(qkv, "sb_fwd")
            m = _matmul(o, w["sb_w_o"][j], "nn", F32, "wo")
            s.update(qkv=qkv, o=o)
        elif kind == 1:
            p = w["s5"][j]
            y_pre, gy, st_r, st_i = _s5_fwd(u, p["ar"], p["ai"], p["wr"], p["wi"], p["cr"], p["ci"], w["s5_d"][j:j + 1],
                                            "s5_fwd")
            t = _matmul(gy, w["s5_w_glu"][j], "nn", F32, "s5_glu")
            b_glu = w["s5_b_glu"][j:j + 1]
            (m,) = _rowwise(lambda t, b: ((_glu(t, b),), ()), [t], [b_glu], [row()], [], "s5_gate")
            s.update(y_pre=y_pre, gy=gy, st_r=st_r, st_i=st_i, t=t)
        else:
            t = _matmul(u, w["cv_w_pw1"][j], "nn", F32, "pw1")
            (hg,) = _rowwise(lambda t, b: ((_glu(t, b),), ()), [t], [w["cv_b_pw1"][j:j + 1]], [row()], [], "cv_glu")
            hc = _dwconv_fwd(hg, w["cv_w_dw"][j], w["cv_b_dw"][j:j + 1], "dwconv")
            (hs,) = _rowwise(lambda h, g, b: ((_ln_silu(h, g, b),), ()), [hc],
                             [w["cv_ln_g"][j:j + 1], w["cv_ln_b"][j:j + 1]], [row(BF16)], [], "cv_ln")
            m = _matmul(hs, w["cv_w_pw2"][j], "nn", F32, "pw2")
            bias = w["cv_b_pw2"][j:j + 1]
            s.update(t=t, hg=hg, hc=hc, hs=hs)
        s["m"] = m
        if bias is None:
            h1, u2 = _rowwise(
                lambda h, m, gate, g1, g2, a, b: ((lambda h1: (h1, _pre(h1, g2, a, b)))(_post(h, m, gate, g1)), ()),
                [h, m], [g_m, ng(l, 1), ng(l, 2), sh_f, sc_f], [row(), row(BF16)], [], "post_pre")
        else:
            h1, u2 = _rowwise(
                lambda h, m, gate, g1, g2, a, b, bb: ((lambda h1: (h1, _pre(h1, g2, a, b)))(_post(h, m + bb, gate, g1)), ()),
                [h, m], [g_m, ng(l, 1), ng(l, 2), sh_f, sc_f, bias], [row(), row(BF16)], [], "post_pre")
        f_dim = w["ffn_w_gu"].shape[2] // 2
        gu = _matmul(u2, w["ffn_w_gu"][l], "nn", BF16, "ffn_gu")
        (act,) = _rowwise(lambda gu: ((_swish_gate(_f(gu)),), ()), [gu], [], [row(BF16, f_dim)], [], "ffn_act")
        f = _matmul(act, w["ffn_w_down"][l], "nn", F32, "ffn_down")
        (h,) = _rowwise(lambda h, m, gate, g: ((_post(h, m, gate, g),), ()), [h1, f], [g_f, ng(l, 3)], [row()], [], "post")
        s.update(h1=h1, u2=u2, gu=gu, act=act, f=f, bias=bias)
        saved.append(s)
    dh, loss_cols = _rowwise(
        lambda y, t: ((((y - t) * (1.0 / d_dim)),), (jnp.sum(jnp.square(y - t), axis=0, keepdims=True),)),
        [h, target], [], [row()], [vec()], "loss")
    grads = {}
    dmod = [[None] * 6 for _ in range(depth)]
    dnorm = [[None] * 4 for _ in range(depth)]

    def add(name, idx, val):
        grads.setdefault(name, {})[idx] = val

    for l in reversed(range(depth)):
        sh_m, sc_m, g_m, sh_f, sc_f, g_f = mods[l]
        kind, j = l % 3, l // 3
        s = saved[l]

        def post_bwd(h, m, dh, gate, g):
            _, vjp = jax.vjp(_post, h, m, gate, g)
            _, dm, dgate, dg = vjp(dh)
            return (dm,), (dgate, dg)

        df, dmod[l][5], dnorm[l][3] = _rowwise(post_bwd, [s["h1"], s["f"], dh], [g_f, ng(l, 3)],
                                               [row(BF16)], [vec(), vec()], "post_bwd")
        f_dim = s["act"].shape[1]
        add("ffn_w_down", l, _matmul(s["act"], df, "tn", F32, "d_down"))
        dact = _matmul(df, w["ffn_w_down"][l], "nt", BF16, "d_act")

        def act_bwd(gu, da):
            _, vjp = jax.vjp(_swish2, *_halves(_f(gu)))
            return (jnp.concatenate(vjp(_f(da)), axis=1),), ()

        (dgu,) = _rowwise(act_bwd, [s["gu"], dact], [], [row(BF16, 2 * f_dim)], [], "act_bwd")
        add("ffn_w_gu", l, _matmul(s["u2"], dgu, "tn", F32, "d_gu"))
        du2 = _matmul(dgu, w["ffn_w_gu"][l], "nt", F32, "d_u2")
        bias = s["bias"]

        def post_pre_bwd(h, m, dh1, du2, gate, g1, g2, a, b, *bb):
            def fwd(h, m, gate, g1, g2, a, b):
                mm = m + bb[0] if bb else m
                h1 = _post(h, mm, gate, g1)
                return h1, _pre(h1, g2, a, b)
            _, vjp = jax.vjp(fwd, h, m, gate, g1, g2, a, b)
            dh0, dm, dgate, dg1, dg2, da, db = vjp((dh1, du2))
            return (dh0, dm), (dgate, dg1, dg2, da, db, jnp.sum(dm, axis=0, keepdims=True))

        vecs = [g_m, ng(l, 1), ng(l, 2), sh_f, sc_f] + ([bias] if bias is not None else [])
        dh, dm, dmod[l][2], dnorm[l][1], dnorm[l][2], dmod[l][3], dmod[l][4], dm_cols = _rowwise(
            post_pre_bwd, [s["h0"], s["m"], dh, du2], vecs, [row(), row(BF16)], [vec()] * 6, "post_pre_bwd")
        if kind == 0:
            add("sb_w_o", j, _matmul(s["o"], dm, "tn", F32, "d_wo"))
            do = _matmul(dm, w["sb_w_o"][j], "nt", BF16, "d_o")
            dqkv = jnp.concatenate(_sb_bwd(s["qkv"], do, "sb_bwd"), axis=1).astype(BF16)
            add("sb_w_qkv", j, _matmul(s["u"], dqkv, "tn", F32, "d_wqkv"))
            du = _matmul(dqkv, w["sb_w_qkv"][j], "nt", F32, "d_u")
        elif kind == 1:
            p = w["s5"][j]
            b_glu = w["s5_b_glu"][j:j + 1]

            def gate_bwd(t, dm, b):
                dt, db = _glu_bwd(t, _f(dm), b)
                return (dt,), (db,)

            dt, db_glu = _rowwise(gate_bwd, [s["t"], dm], [b_glu], [row(BF16, 2 * d_dim)], [vec(2 * d_dim)], "s5_gate_bwd")
            add("s5_b_glu", j, db_glu)
            add("s5_w_glu", j, _matmul(s["gy"], dt, "tn", F32, "d_wglu"))
            dgy = _matmul(dt, w["s5_w_glu"][j], "nt", F32, "d_gy")

            def gelu_bwd(y, dg):
                _, vjp = jax.vjp(jax.nn.gelu, y)
                return (vjp(dg)[0],), ()

            (dy,) = _rowwise(gelu_bwd, [s["y_pre"], dgy], [], [row()], [], "gelu_bwd")
            chunk = s_dim // (s["st_r"].shape[0] // SUBLANES)
            du, dwr, dwi, dcr, dci, dar, dai, dd = _s5_bwd(
                s["u"], dy, s["st_r"], s["st_i"], p["ar"], p["ai"], p["wr"], p["wi"], p["cr"], p["ci"],
                w["s5_d"][j:j + 1], "s5_bwd", chunk)
            add("s5_d", j, dd)
            add("s5_raw", j, (dwr, dwi, dcr, dci, dar, dai))
        else:
            add("cv_w_pw2", j, _matmul(s["hs"], dm, "tn", F32, "d_pw2"))
            dhs = _matmul(dm, w["cv_w_pw2"][j], "nt", F32, "d_hs")
            ln_g, ln_b = w["cv_ln_g"][j:j + 1], w["cv_ln_b"][j:j + 1]

            def ln_bwd(hc, dhs, g, b):
                _, vjp = jax.vjp(_ln_silu, hc, g, b)
                dhc, dg, db = vjp(dhs)
                return (dhc,), (dg, db)

            dhc, dln_g, dln_b = _rowwise(ln_bwd, [s["hc"], dhs], [ln_g, ln_b], [row()], [vec()] * 2, "ln_bwd")
            add("cv_ln_g", j, dln_g)
            add("cv_ln_b", j, dln_b)
            add("cv_b_pw2", j, dm_cols)
            dhg, dw_dw, db_dw = _dwconv_bwd(s["hg"], dhc, w["cv_w_dw"][j], "dwconv_bwd")
            add("cv_w_dw", j, dw_dw)
            add("cv_b_dw", j, db_dw)
            b_pw1 = w["cv_b_pw1"][j:j + 1]

            def glu_bwd(t, dhg, b):
                dt, db = _glu_bwd(t, dhg, b)
                return (dt,), (db,)

            dt, db_pw1 = _rowwise(glu_bwd, [s["t"], dhg], [b_pw1], [row(BF16, 2 * d_dim)], [vec(2 * d_dim)], "cv_glu_bwd")
            add("cv_b_pw1", j, db_pw1)
            add("cv_w_pw1", j, _matmul(s["u"], dt, "tn", F32, "d_pw1"))
            du = _matmul(dt, w["cv_w_pw1"][j], "nt", F32, "d_u")

        def pre_bwd(h, dh, du, g, a, b):
            _, vjp = jax.vjp(_pre, h, g, a, b)
            dh0, dg, da, db = vjp(du)
            return (dh + dh0,), (dg, da, db)

        dh, dnorm[l][0], dmod[l][0], dmod[l][1] = _rowwise(pre_bwd, [s["h0"], dh, du], [ng(l, 0), sh_m, sc_m],
                                                            [row()], [vec()] * 3, "pre_bwd")
    grads["norm_g"] = jnp.stack([jnp.concatenate(r, axis=0) for r in dnorm])
    dmod_arr = jnp.concatenate([jnp.concatenate(r, axis=1) for r in dmod], axis=0)
    return loss_cols, dh, dmod_arr, grads


def _position():
    return lax.axis_index("x"), lax.axis_index("y"), lax.axis_index("c")


def _all_gather8(x, base):
    m, n = x.shape
    space = pltpu.VMEM

    def body(x_ref, out_ref, send_sems, recv_sems, local_sem):
        px, py, pc = _position()
        me, sibling = (px, py, pc), (px, py, 1 - pc)
        chips = [(1 - px, py), (px, 1 - py), (1 - px, 1 - py)]
        src_mine = x_ref

        def rows(bx, by, bc):
            return out_ref.at[pl.ds((4 * bx + 2 * by + bc) * m, m), :]

        def copy(k, block, to, src=None):
            return pltpu.make_async_remote_copy(
                src_ref=rows(*block) if src is None else src, dst_ref=rows(*block),
                send_sem=send_sems.at[k], recv_sem=recv_sems.at[k], device_id=to, device_id_type=MESH)

        mine = pltpu.make_async_copy(src_mine, rows(*me), local_sem)
        mine.start()
        first = [copy(0, me, sibling, src=src_mine)]
        first += [copy(1 + j, me, (*chip, pc), src=src_mine) for j, chip in enumerate(chips)]
        for cp in first:
            cp.start()
        passed = [copy(4 + j, (*chip, pc), sibling) for j, chip in enumerate(chips)]
        for j, chip in enumerate(chips):
            copy(1 + j, (*chip, pc), me).wait_recv()
            passed[j].start()
        copy(0, sibling, me).wait_recv()
        for j, chip in enumerate(chips):
            copy(4 + j, (*chip, 1 - pc), me).wait_recv()
        for cp in first + passed:
            cp.wait_send()
        mine.wait()

    return pl.pallas_call(
        body, name=_name(base), out_shape=SDS((8 * m, n), x.dtype),
        in_specs=[pl.BlockSpec(memory_space=space)], out_specs=pl.BlockSpec(memory_space=space),
        scratch_shapes=[pltpu.SemaphoreType.DMA((7,)), pltpu.SemaphoreType.DMA((7,)), pltpu.SemaphoreType.DMA],
    )(x)


_ANY = pl.BlockSpec(memory_space=pl.ANY)


def _half(ref, core, lead):
    k_half = ref.shape[lead] // 2
    return ref.at[(slice(None),) * lead + (pl.ds(core * k_half, k_half), slice(None))]


def _gather_weights(shards, base):
    nw = len(shards)

    def body(*refs):
        x_refs, out_refs = refs[:nw], refs[nw:2 * nw]
        send_sems, recv_sems, local_sems = refs[2 * nw:]
        px, py, pc = _position()
        me, sibling = (px, py, pc), (px, py, 1 - pc)
        chips = [(1 - px, py), (px, 1 - py), (1 - px, 1 - py)]

        def block(i, bx, by, bc):
            return _half(out_refs[i].at[2 * bx + by], bc, 1)

        def copy(i, k, blk, to, src=None):
            dst = block(i, *blk)
            return pltpu.make_async_remote_copy(
                src_ref=dst if src is None else src, dst_ref=dst, send_sem=send_sems.at[7 * i + k],
                recv_sem=recv_sems.at[7 * i + k], device_id=to, device_id_type=MESH)

        mine = [_half(x_refs[i], pc, 1) for i in range(nw)]
        local = [pltpu.make_async_copy(mine[i], block(i, *me), local_sems.at[i]) for i in range(nw)]
        for cp in local:
            cp.start()
        first = []
        for i in range(nw):
            first.append(copy(i, 0, me, sibling, src=mine[i]))
            first += [copy(i, 1 + j, me, (*chip, pc), src=mine[i]) for j, chip in enumerate(chips)]
        for cp in first:
            cp.start()
        passed = []
        for j, chip in enumerate(chips):
            for i in range(nw):
                copy(i, 1 + j, (*chip, pc), me).wait_recv()
                passed.append(copy(i, 4 + j, (*chip, pc), sibling))
                passed[-1].start()
        for i in range(nw):
            copy(i, 0, sibling, me).wait_recv()
        for j, chip in enumerate(chips):
            for i in range(nw):
                copy(i, 4 + j, (*chip, 1 - pc), me).wait_recv()
        for cp in first + passed:
            cp.wait_send()
        for cp in local:
            cp.wait()

    return pl.pallas_call(
        body, name=_name(base), out_shape=[SDS((4,) + s.shape, s.dtype) for s in shards],
        in_specs=[_ANY] * nw, out_specs=[_ANY] * nw,
        scratch_shapes=[pltpu.SemaphoreType.DMA((7 * nw,)), pltpu.SemaphoreType.DMA((7 * nw,)),
                        pltpu.SemaphoreType.DMA((nw,))],
    )(*shards)


def _reduce_to_sibling(gs, base):
    nw = len(gs)

    def body(*refs):
        g_refs, recv_refs, send_sems, recv_sems = refs[:nw], refs[nw:2 * nw], refs[2 * nw], refs[2 * nw + 1]
        px, py, pc = _position()
        copies = [pltpu.make_async_remote_copy(
            src_ref=_half(g_refs[i], 1 - pc, 2), dst_ref=recv_refs[i], send_sem=send_sems.at[i],
            recv_sem=recv_sems.at[i], device_id=(px, py, 1 - pc), device_id_type=MESH) for i in range(nw)]
        for cp in copies:
            cp.start()
        for cp in copies:
            cp.wait_recv()
        for cp in copies:
            cp.wait_send()

    outs = [SDS(g.shape[:2] + (g.shape[2] // 2, g.shape[3]), g.dtype) for g in gs]
    return pl.pallas_call(
        body, name=_name(base), out_shape=outs, in_specs=[_ANY] * nw, out_specs=[_ANY] * nw,
        scratch_shapes=[pltpu.SemaphoreType.DMA((nw,)), pltpu.SemaphoreType.DMA((nw,))],
    )(*gs)


def _exchange_chips(ts, base):
    nw = len(ts)

    def body(*refs):
        t_refs, recv_refs, send_sems, recv_sems = refs[:nw], refs[nw:2 * nw], refs[2 * nw], refs[2 * nw + 1]
        px, py, pc = _position()
        chips = [(1 - px, py), (px, 1 - py), (1 - px, 1 - py)]
        copies = [pltpu.make_async_remote_copy(
            src_ref=t_refs[i].at[2 * cx + cy], dst_ref=recv_refs[i].at[k], send_sem=send_sems.at[3 * i + k],
            recv_sem=recv_sems.at[3 * i + k], device_id=(cx, cy, pc), device_id_type=MESH)
            for k, (cx, cy) in enumerate(chips) for i in range(nw)]
        for cp in copies:
            cp.start()
        for cp in copies:
            cp.wait_recv()
        for cp in copies:
            cp.wait_send()

    outs = [SDS((3,) + t.shape[1:], t.dtype) for t in ts]
    return pl.pallas_call(
        body, name=_name(base), out_shape=outs, in_specs=[_ANY] * nw, out_specs=[_ANY] * nw,
        scratch_shapes=[pltpu.SemaphoreType.DMA((3 * nw,)), pltpu.SemaphoreType.DMA((3 * nw,))],
    )(*ts)


def _share_with_sibling(rs, base):
    nw = len(rs)

    def body(*refs):
        in_refs, out_refs, send_sems, recv_sems = refs[:nw], refs[nw:2 * nw], refs[2 * nw], refs[2 * nw + 1]
        px, py, pc = _position()
        send = [pltpu.make_async_remote_copy(
            src_ref=_half(in_refs[i], pc, 1), dst_ref=_half(out_refs[i], pc, 1), send_sem=send_sems.at[i],
            recv_sem=recv_sems.at[i], device_id=(px, py, 1 - pc), device_id_type=MESH) for i in range(nw)]
        for cp in send:
            cp.start()
        for i in range(nw):
            pltpu.make_async_remote_copy(
                src_ref=_half(in_refs[i], 1 - pc, 1), dst_ref=_half(out_refs[i], 1 - pc, 1), send_sem=send_sems.at[i],
                recv_sem=recv_sems.at[i], device_id=(px, py, 1 - pc), device_id_type=MESH).wait_recv()
        for cp in send:
            cp.wait_send()

    return pl.pallas_call(
        body, name=_name(base), out_shape=[SDS(r.shape, r.dtype) for r in rs], in_specs=[_ANY] * nw,
        out_specs=[_ANY] * nw, input_output_aliases={i: i for i in range(nw)},
        scratch_shapes=[pltpu.SemaphoreType.DMA((nw,)), pltpu.SemaphoreType.DMA((nw,))],
    )(*rs)


def _add_own_half(g, recv, core, base, tr=512):
    _, l_dim, k_half, n = recv.shape
    tr = _tile(k_half, tr, 2 * SUBLANES)
    per = k_half // tr

    def body(core_ref, g_ref, r_ref, o_ref):
        o_ref[...] = (g_ref[...] + r_ref[...]).astype(o_ref.dtype)

    blk = (1, 1, tr, n)
    spec = pltpu.PrefetchScalarGridSpec(
        num_scalar_prefetch=1, grid=(4, l_dim, per),
        in_specs=[pl.BlockSpec(blk, lambda j, l, i, core: (j, l, core[0] * per + i, 0)),
                  pl.BlockSpec(blk, lambda j, l, i, core: (j, l, i, 0))],
        out_specs=pl.BlockSpec(blk, lambda j, l, i, core: (j, l, i, 0)))
    return pl.pallas_call(body, name=_name(base), grid_spec=spec, out_shape=SDS(recv.shape, BF16),
                          compiler_params=_params(("parallel", "parallel", "parallel")))(core, g, recv)


def _add_chips(g, from_sibling, recv, chip_core, base, tr=512):
    _, l_dim, k_half, n = from_sibling.shape
    tr = _tile(k_half, tr, 2 * SUBLANES)
    per = k_half // tr

    def body(idx_ref, g_ref, s_ref, r_ref, o_ref):
        own = g_ref[0, 0] + s_ref[0, 0]
        o_ref[0] = ((own + _f(r_ref[0, 0])) + _f(r_ref[1, 0])) + _f(r_ref[2, 0])

    spec = pltpu.PrefetchScalarGridSpec(
        num_scalar_prefetch=1, grid=(l_dim, per),
        in_specs=[pl.BlockSpec((1, 1, tr, n), lambda l, i, idx: (idx[0], l, idx[1] * per + i, 0)),
                  pl.BlockSpec((1, 1, tr, n), lambda l, i, idx: (idx[0], l, i, 0)),
                  pl.BlockSpec((3, 1, tr, n), lambda l, i, idx: (0, l, i, 0))],
        out_specs=pl.BlockSpec((1, tr, n), lambda l, i, idx: (l, idx[1] * per + i, 0)))
    return pl.pallas_call(body, name=_name(base), grid_spec=spec, out_shape=SDS((l_dim, 2 * k_half, n), F32),
                          compiler_params=_params(("parallel", "parallel")))(chip_core, g, from_sibling, recv)


def _sum8(g, base):
    r, n = g.shape[0] // 8, g.shape[1]

    def body(g_ref, o_ref):
        acc = g_ref[pl.ds(0, r), :]
        for d in range(1, 8):
            acc = acc + g_ref[pl.ds(d * r, r), :]
        o_ref[...] = acc

    return pl.pallas_call(body, name=_name(base), out_shape=SDS((r, n), F32), compiler_params=_params())(g)


def _adamw_math(w, g, m, v):
    m = ADAM_B1 * m + (1.0 - ADAM_B1) * g
    v = ADAM_B2 * v + (1.0 - ADAM_B2) * jnp.square(g)
    m_hat = m / (1.0 - ADAM_B1 ** ADAM_STEP)
    v_hat = v / (1.0 - ADAM_B2 ** ADAM_STEP)
    delta = -ADAM_LR * (m_hat / (jnp.sqrt(v_hat) + ADAM_EPS) + ADAM_WD * w)
    return delta, m, v


def _adamw(w, g, m, v, base):
    out = SDS(w.shape, F32)
    return _rowwise(lambda w, g, m, v: (_adamw_math(w, g, m, v), ()), [w, g, m, v], [], [out, out, out], [], base)


def _mod_grad_adamw(sc_t, dmod, w, m, v, base, tr=256):
    l_dim, d_dim, n = w.shape
    tr = _tile(d_dim, tr, SUBLANES)

    def body(sc_ref, dm_ref, w_ref, m_ref, v_ref, g_ref, d_ref, nm_ref, nv_ref):
        g = lax.dot_general(sc_ref[...], dm_ref[0], _DIMS["nn"], preferred_element_type=F32,
                            precision=lax.Precision.HIGHEST)
        delta, nm, nv = _adamw_math(w_ref[0], g, m_ref[0], v_ref[0])
        g_ref[0], d_ref[0], nm_ref[0], nv_ref[0] = g, delta, nm, nv

    tile = pl.BlockSpec((1, tr, n), lambda l, i: (l, i, 0))
    out = SDS(w.shape, F32)
    return pl.pallas_call(
        body, name=_name(base), grid=(l_dim, d_dim // tr),
        in_specs=[pl.BlockSpec((tr, 8), lambda l, i: (i, 0)), pl.BlockSpec((1, 8, n), lambda l, i: (l, 0, 0)),
                  tile, tile, tile],
        out_specs=[tile] * 4, out_shape=[out] * 4, compiler_params=_params(("parallel", "parallel")),
    )(sc_t, dmod, w, m, v)


def _mod_forward(c_all, w_mod, b_mod, base):
    l_dim, d_dim, n = w_mod.shape

    def body(c_ref, w_ref, b_ref, o_ref):
        o_ref[0] = lax.dot_general(jax.nn.silu(c_ref[...]), w_ref[0], _DIMS["nn"], preferred_element_type=F32,
                                   precision=lax.Precision.HIGHEST) + b_ref[0]

    return pl.pallas_call(
        body, name=_name(base), grid=(l_dim,),
        in_specs=[pl.BlockSpec((8, d_dim), lambda l: (0, 0)), pl.BlockSpec((1, d_dim, n), lambda l: (l, 0, 0)),
                  pl.BlockSpec((1, 1, n), lambda l: (l, 0, 0))],
        out_specs=pl.BlockSpec((1, 8, n), lambda l: (l, 0, 0)), out_shape=SDS((l_dim, 8, n), F32),
        compiler_params=_params(("parallel",)),
    )(c_all, w_mod, b_mod)


def _pack(arrays, dtype, row_mult):
    flat = jnp.concatenate([a.reshape(-1).astype(dtype) for a in arrays])
    quantum = PACK_LANES * row_mult
    pad = (-flat.shape[0]) % quantum
    return jnp.pad(flat, (0, pad)).reshape(-1, PACK_LANES)


def _unpack(flat, shapes):
    out, off = [], 0
    for shp in shapes:
        n = math.prod(shp)
        out.append(flat[off:off + n].reshape(shp))
        off += n
    return out


_WEIGHTS = ["norm_g", "w_mod", "b_mod", "sb_w_qkv", "sb_w_o", "s5_lam_re", "s5_lam_im", "s5_log_dt", "s5_b_re",
            "s5_b_im", "s5_c_re", "s5_c_im", "s5_d", "s5_w_glu", "s5_b_glu", "cv_w_pw1", "cv_b_pw1", "cv_w_dw",
            "cv_b_dw", "cv_ln_g", "cv_ln_b", "cv_w_pw2", "cv_b_pw2", "ffn_w_gate", "ffn_w_up", "ffn_w_down"]
_BIG_COL = ["sb_w_qkv", "s5_w_glu", "cv_w_pw1", "ffn_w_gate", "ffn_w_up"]
_BIG_ROW = ["sb_w_o", "cv_w_pw2", "ffn_w_down"]
_BIG = _BIG_COL + _BIG_ROW
_SMALL_SHARDED = {"norm_g": 2, "cv_b_pw1": 1, "cv_w_dw": 2, "cv_b_dw": 1, "cv_ln_g": 1, "cv_ln_b": 1, "cv_b_pw2": 1}
_REPLICATED = ["b_mod", "s5_lam_re", "s5_lam_im", "s5_log_dt", "s5_b_re", "s5_b_im", "s5_c_re", "s5_c_im", "s5_d",
               "s5_b_glu"]


def _full_from_shards(stacked, name):
    if name in _BIG_COL or (name in _SMALL_SHARDED):
        axis = stacked.ndim - 1
    else:
        axis = stacked.ndim - 2
    moved = jnp.moveaxis(stacked, 0, axis - 1)
    shp = list(moved.shape)
    shp[axis - 1:axis + 1] = [shp[axis - 1] * shp[axis]]
    return moved.reshape(shp)


def _shards_from_full(full, name):
    axis = full.ndim - 1 if (name in _BIG_COL or name in _SMALL_SHARDED) else full.ndim - 2
    shp = list(full.shape)
    shp[axis:axis + 1] = [4, shp[axis] // 4]
    return jnp.moveaxis(full.reshape(shp), axis, 0)


def kernel(x, c, norm_g, w_mod, b_mod, sb_w_qkv, sb_w_o, s5_lam_re, s5_lam_im, s5_log_dt, s5_b_re, s5_b_im, s5_c_re, s5_c_im, s5_d, s5_w_glu, s5_b_glu, cv_w_pw1, cv_b_pw1, cv_w_dw, cv_b_dw, cv_ln_g, cv_ln_b, cv_w_pw2, cv_b_pw2, ffn_w_gate, ffn_w_up, ffn_w_down, loss_target, m_norm_g, m_w_mod, m_b_mod, m_sb_w_qkv, m_sb_w_o, m_s5_lam_re, m_s5_lam_im, m_s5_log_dt, m_s5_b_re, m_s5_b_im, m_s5_c_re, m_s5_c_im, m_s5_d, m_s5_w_glu, m_s5_b_glu, m_cv_w_pw1, m_cv_b_pw1, m_cv_w_dw, m_cv_b_dw, m_cv_ln_g, m_cv_ln_b, m_cv_w_pw2, m_cv_b_pw2, m_ffn_w_gate, m_ffn_w_up, m_ffn_w_down, v_norm_g, v_w_mod, v_b_mod, v_sb_w_qkv, v_sb_w_o, v_s5_lam_re, v_s5_lam_im, v_s5_log_dt, v_s5_b_re, v_s5_b_im, v_s5_c_re, v_s5_c_im, v_s5_d, v_s5_w_glu, v_s5_b_glu, v_cv_w_pw1, v_cv_b_pw1, v_cv_w_dw, v_cv_b_dw, v_cv_ln_g, v_cv_ln_b, v_cv_w_pw2, v_cv_b_pw2, v_ffn_w_gate, v_ffn_w_up, v_ffn_w_down):
    args = locals()
    wts = {n: args[n] for n in _WEIGHTS}
    mom = {n: args["m_" + n] for n in _WEIGHTS}
    var = {n: args["v_" + n] for n in _WEIGHTS}
    px, py, pc = lax.axis_index("x"), lax.axis_index("y"), lax.axis_index("c")
    chip = 2 * px + py
    dev = 2 * chip + pc
    s_dim, d_dim = x.shape[1], x.shape[2]
    depth = norm_g.shape[0]
    x2, tgt = x[0], loss_target[0]

    small_names = list(_SMALL_SHARDED)
    small_in = _pack([c] + [wts[n] for n in small_names], F32, SUBLANES)
    small_all = _all_gather8(small_in, "gather_small").reshape(8, -1)
    c_all = small_all[:, :d_dim]
    full = {}
    off = d_dim
    for n in small_names:
        size = math.prod(wts[n].shape)
        stacked = small_all[0::2, off:off + size].reshape((4,) + wts[n].shape)
        full[n] = _full_from_shards(stacked, n)
        off += size

    n_mod = w_mod.shape[2]
    b_mod_cols = lax.dynamic_slice_in_dim(b_mod, chip * n_mod, n_mod, axis=1)[:, None, :]
    mod_part = _mod_forward(c_all, w_mod, b_mod_cols, "mod_fwd")
    mod_all = _all_gather8(mod_part.reshape(depth * 8, n_mod), "gather_mod")
    mod_all = mod_all.reshape(4, 2, depth, 8, n_mod)[:, 0]
    mod_mine = lax.dynamic_index_in_dim(mod_all, dev, axis=2, keepdims=False)
    mod = mod_mine.transpose(1, 0, 2).reshape(depth, 4 * n_mod)

    stacked = _gather_weights([wts[n].astype(BF16) for n in _BIG], "gather_big")
    for n, st in zip(_BIG, stacked, strict=True):
        full[n] = _full_from_shards(st, n)
    full["ffn_w_gu"] = jnp.concatenate([full.pop("ffn_w_gate"), full.pop("ffn_w_up")], axis=2)
    for n in _REPLICATED:
        full[n] = wts[n]

    n_s5 = s5_lam_re.shape[0]
    gp = s5_lam_re.shape[1] * s5_lam_re.shape[2]
    col = lambda a: a.reshape(gp, 1)
    s5_raw_in = []
    full["s5"] = []
    for j in range(n_s5):
        ldt = jnp.broadcast_to(s5_log_dt[j][:, None], s5_lam_re[j].shape)
        raw = (col(s5_lam_re[j]), col(s5_lam_im[j]), col(ldt), s5_b_re[j].reshape(gp, S5_GROUP), s5_b_im[j].reshape(gp, S5_GROUP))
        ar, ai, bbr, bbi = _s5_prep(*raw, "s5_prep")
        s5_raw_in.append(raw)
        shape3 = s5_b_re[j].shape
        full["s5"].append(dict(
            ar=ar.reshape(1, gp), ai=ai.reshape(1, gp),
            wr=_block_diag(bbr.reshape(shape3), False).astype(BF16), wi=_block_diag(bbi.reshape(shape3), False).astype(BF16),
            cr=_block_diag(s5_c_re[j], True).astype(BF16), ci=_block_diag(s5_c_im[j], True).astype(BF16)))

    loss_cols, grad_x, dmod, grads = _local_step(x2, mod, full, tgt)
    loss = lax.psum(0.5 * jnp.sum(loss_cols) / d_dim, ("x", "y", "c"))

    small_grads = {}
    for name in ("s5_lam_re", "s5_lam_im", "s5_log_dt", "s5_b_re", "s5_b_im", "s5_c_re", "s5_c_im"):
        small_grads[name] = []
    for j in range(n_s5):
        dwr, dwi, dcr, dci, dar, dai = grads["s5_raw"][j]
        shape3 = s5_b_re[j].shape
        cots = (jnp.sum(dar, axis=0).reshape(gp, 1), jnp.sum(dai, axis=0).reshape(gp, 1),
                _block_diag_extract(dwr, False).reshape(gp, S5_GROUP), _block_diag_extract(dwi, False).reshape(gp, S5_GROUP))
        dlr, dli, dldt, dbr, dbi = _s5_prep_bwd(*s5_raw_in[j], cots, "s5_prep_bwd")
        small_grads["s5_lam_re"].append(dlr.reshape(s5_lam_re[j].shape))
        small_grads["s5_lam_im"].append(dli.reshape(s5_lam_re[j].shape))
        small_grads["s5_log_dt"].append(jnp.sum(dldt.reshape(s5_lam_re[j].shape), axis=1))
        small_grads["s5_b_re"].append(dbr.reshape(shape3))
        small_grads["s5_b_im"].append(dbi.reshape(shape3))
        small_grads["s5_c_re"].append(_block_diag_extract(dcr, True))
        small_grads["s5_c_im"].append(_block_diag_extract(dci, True))
    small_grads = {n: jnp.stack(v) for n, v in small_grads.items()}
    stack = lambda name, count: jnp.stack([grads[name][j] for j in range(count)])
    small_grads["s5_d"] = stack("s5_d", n_s5)[:, 0]
    small_grads["s5_b_glu"] = stack("s5_b_glu", n_s5)[:, 0]
    n_cv = cv_w_pw1.shape[0]
    for name in ("cv_b_pw1", "cv_b_dw", "cv_ln_g", "cv_ln_b", "cv_b_pw2"):
        small_grads[name] = stack(name, n_cv)[:, 0]
    small_grads["cv_w_dw"] = stack("cv_w_dw", n_cv)
    small_grads["norm_g"] = grads["norm_g"]

    small_order = ["norm_g", "s5_lam_re", "s5_lam_im", "s5_log_dt", "s5_b_re", "s5_b_im", "s5_c_re", "s5_c_im", "s5_d",
                   "s5_b_glu", "cv_b_pw1", "cv_w_dw", "cv_b_dw", "cv_ln_g", "cv_ln_b", "cv_b_pw2"]
    small_shapes = [small_grads[n].shape for n in small_order]
    sg_in = _pack([dmod] + [small_grads[n] for n in small_order], F32, SUBLANES)
    sg_all = _all_gather8(sg_in, "gather_small_grads")
    sg_sum = _sum8(sg_all, "sum_small_grads").reshape(-1)
    dmod_all = sg_all.reshape(8, -1)[:, :dmod.size].reshape(8, depth, 4, n_mod)
    g_full = dict(zip(small_order, _unpack(sg_sum[dmod.size:], small_shapes), strict=True))
    g_full["b_mod"] = sg_sum[:dmod.size].reshape(dmod.shape)

    gu = stack("ffn_w_gu", depth)
    f_dim = gu.shape[2] // 2
    big_grads = {"sb_w_qkv": stack("sb_w_qkv", sb_w_qkv.shape[0]), "sb_w_o": stack("sb_w_o", sb_w_o.shape[0]),
                 "s5_w_glu": stack("s5_w_glu", n_s5), "cv_w_pw1": stack("cv_w_pw1", n_cv), "cv_w_pw2": stack("cv_w_pw2", n_cv),
                 "ffn_w_gate": gu[:, :, :f_dim], "ffn_w_up": gu[:, :, f_dim:], "ffn_w_down": stack("ffn_w_down", depth)}
    chip_major = [_shards_from_full(big_grads[n], n) for n in _BIG]
    core_idx, chip_core = pc.reshape(1).astype(I32), jnp.stack([chip, pc]).astype(I32)
    from_sibling = _reduce_to_sibling(chip_major, "reduce_sibling")
    chip_sums = [_add_own_half(g, r, core_idx, "reduce_add_sibling") for g, r in zip(chip_major, from_sibling, strict=True)]
    from_chips = _exchange_chips(chip_sums, "reduce_chips")
    my_half = [_add_chips(g, s, r, chip_core, "reduce_add_chips")
               for g, s, r in zip(chip_major, from_sibling, from_chips, strict=True)]
    g_shard = dict(zip(_BIG, _share_with_sibling(my_half, "reduce_share"), strict=True))

    out_g, out_d, out_m, out_v = {}, {}, {}, {}
    for n in _BIG:
        shp = wts[n].shape
        two = lambda a: a.reshape(-1, shp[-1])
        out_g[n] = g_shard[n]
        d, nm, nv = _adamw(two(wts[n]), two(g_shard[n]), two(mom[n]), two(var[n]), "adamw")
        out_d[n], out_m[n], out_v[n] = d.reshape(shp), nm.reshape(shp), nv.reshape(shp)
    sc_t = jax.nn.silu(c_all).T
    dmod_cols = lax.dynamic_index_in_dim(dmod_all, chip, axis=2, keepdims=False).transpose(1, 0, 2)
    out_g["w_mod"], out_d["w_mod"], out_m["w_mod"], out_v["w_mod"] = _mod_grad_adamw(
        sc_t, dmod_cols, w_mod, m_w_mod, v_w_mod, "w_mod_adamw")
    small_all_names = [n for n in _WEIGHTS if n not in _BIG and n != "w_mod"]
    local_g = []
    for n in small_all_names:
        g = g_full[n]
        if n in _SMALL_SHARDED:
            width = wts[n].shape[-1]
            g = lax.dynamic_slice_in_dim(g, chip * width, width, axis=g.ndim - 1)
        local_g.append(g)
        out_g[n] = g
    pk = lambda arrs: _pack(arrs, F32, SUBLANES)
    d, nm, nv = _adamw(pk([wts[n] for n in small_all_names]), pk(local_g), pk([mom[n] for n in small_all_names]),
                       pk([var[n] for n in small_all_names]), "adamw_small")
    shapes = [wts[n].shape for n in small_all_names]
    for n, a, b, cc in zip(small_all_names, _unpack(d.reshape(-1), shapes), _unpack(nm.reshape(-1), shapes),
                           _unpack(nv.reshape(-1), shapes), strict=True):
        out_d[n], out_m[n], out_v[n] = a, b, cc
    return (loss, grad_x[None], *[out_g[n] for n in _WEIGHTS], *[out_d[n] for n in _WEIGHTS],
            *[out_m[n] for n in _WEIGHTS], *[out_v[n] for n in _WEIGHTS])
```

```python
import functools
import math

import jax
import jax.numpy as jnp
from jax import lax
from jax.experimental import pallas as pl
from jax.experimental.pallas import tpu as pltpu

F32 = jnp.float32
BF16 = jnp.bfloat16
I32 = jnp.int32
SDS = jax.ShapeDtypeStruct
MESH = pl.DeviceIdType.MESH

HEAD_DIM = 64
HEAD_PAIR = 2 * HEAD_DIM
S5_GROUP = 16
S5_STATE = 64
S5_BLOCK_GROUPS = 8
S5_BLOCK_CH = S5_GROUP * S5_BLOCK_GROUPS
S5_BLOCK_ST = S5_STATE * S5_BLOCK_GROUPS
CONV_WIDTH = 31
CONV_HALO = 32
EPS = 1e-6
ADAM_LR = 0.001
ADAM_B1 = 0.9
ADAM_B2 = 0.999
ADAM_EPS = 1e-08
ADAM_WD = 0.01
ADAM_STEP = 10

LANES = 128
SUBLANES = 8
VMEM_LIMIT = 56 * 2 ** 20
PACK_LANES = 512
ATT_BLOCK = 128
EXP_FLOOR = -104.0
ATT_PAIRS_FWD = 4
ATT_PAIRS_BWD = 2
ATT_KEEP = 4
_KEPT = ("log_keep", "log_beta", "weight", "d_exponent")

_COUNTER = [0]


def _name(base):
    _COUNTER[0] += 1
    return f"{base}_{_COUNTER[0]}"


def _tile(dim, target, mult=LANES):
    t = min(dim, target)
    t -= t % mult
    while t > mult and dim % t:
        t -= mult
    return t if t > 0 and dim % t == 0 else dim


def _params(sem=None):
    return pltpu.CompilerParams(dimension_semantics=sem, vmem_limit_bytes=VMEM_LIMIT)


_DIMS = {"nn": (((1,), (0,)), ((), ())), "nt": (((1,), (1,)), ((), ())), "tn": (((0,), (0,)), ((), ()))}


def _matmul(a, b, form, out_dtype, base, tm=1536, tn=1536, tk=1536):
    if form == "tn":
        k_dim, m_dim = a.shape
    else:
        m_dim, k_dim = a.shape
    n_dim = b.shape[0] if form == "nt" else b.shape[1]
    tm, tn, tk = _tile(m_dim, tm), _tile(n_dim, tn), _tile(k_dim, tk)
    nk = k_dim // tk
    dims = _DIMS[form]

    def body(a_ref, b_ref, o_ref, *acc):
        part = lax.dot_general(a_ref[...].astype(BF16), b_ref[...].astype(BF16), dims, preferred_element_type=F32)
        if nk == 1:
            o_ref[...] = part.astype(o_ref.dtype)
            return
        acc_ref, = acc
        k = pl.program_id(2)

        @pl.when(k == 0)
        def _():
            acc_ref[...] = part

        @pl.when(jnp.logical_and(k > 0, k < nk - 1))
        def _():
            acc_ref[...] += part

        @pl.when(k == nk - 1)
        def _():
            o_ref[...] = (acc_ref[...] + part).astype(o_ref.dtype)

    a_spec = (pl.BlockSpec((tk, tm), lambda i, j, k: (k, i)) if form == "tn"
              else pl.BlockSpec((tm, tk), lambda i, j, k: (i, k)))
    b_spec = (pl.BlockSpec((tn, tk), lambda i, j, k: (j, k)) if form == "nt"
              else pl.BlockSpec((tk, tn), lambda i, j, k: (k, j)))
    return pl.pallas_call(
        body, name=_name(base), grid=(m_dim // tm, n_dim // tn, nk),
        in_specs=[a_spec, b_spec], out_specs=pl.BlockSpec((tm, tn), lambda i, j, k: (i, j)),
        out_shape=SDS((m_dim, n_dim), out_dtype), scratch_shapes=[pltpu.VMEM((tm, tn), F32)] if nk > 1 else [],
        compiler_params=_params(("parallel", "parallel", "arbitrary")),
    )(a, b)


def _rowwise(fn, rows, vecs, out_rows, out_vecs, base, tr=256):
    s_dim = rows[0].shape[0]
    tr = _tile(s_dim, tr, SUBLANES)
    nr, nv, nor = len(rows), len(vecs), len(out_rows)

    def body(*refs):
        r_in, v_in = refs[:nr], refs[nr:nr + nv]
        r_out, v_out = refs[nr + nv:nr + nv + nor], refs[nr + nv + nor:]
        ro, vo = fn(*[r[...] for r in r_in], *[v[...] for v in v_in])
        for ref, val in zip(r_out, ro, strict=True):
            ref[...] = val.astype(ref.dtype)
        if v_out:
            i = pl.program_id(0)

            @pl.when(i == 0)
            def _():
                for ref, val in zip(v_out, vo, strict=True):
                    ref[...] = val.astype(ref.dtype)

            @pl.when(i > 0)
            def _():
                for ref, val in zip(v_out, vo, strict=True):
                    ref[...] += val.astype(ref.dtype)

    in_specs = [pl.BlockSpec((tr, r.shape[1]), lambda i: (i, 0)) for r in rows]
    in_specs += [pl.BlockSpec(v.shape, lambda i: (0, 0)) for v in vecs]
    out_specs = [pl.BlockSpec((tr, o.shape[1]), lambda i: (i, 0)) for o in out_rows]
    out_specs += [pl.BlockSpec(o.shape, lambda i: (0, 0)) for o in out_vecs]
    outs = pl.pallas_call(
        body, name=_name(base), grid=(s_dim // tr,), in_specs=in_specs, out_specs=out_specs,
        out_shape=list(out_rows) + list(out_vecs),
        compiler_params=_params(("arbitrary",)),
    )(*rows, *vecs)
    return outs


def _f(x):
    return x.astype(F32)


def _rms(x, g):
    return x * lax.rsqrt(jnp.mean(x * x, axis=-1, keepdims=True) + EPS) * g


def _pre(h, g, sh, sc):
    return _rms(h, g) * (1 + sc) + sh


def _post(h, m, gate, g):
    return h + gate * _rms(m, g)


def _halves(t):
    d = t.shape[1] // 2
    return t[:, :d], t[:, d:]


def _glu2(ta, tb, ba, bb):
    return (ta + ba) * jax.nn.sigmoid(tb + bb)


def _glu(t, b):
    return _glu2(*_halves(t), *_halves(b))


def _glu_bwd(t, dout, b):
    _, vjp = jax.vjp(_glu2, *_halves(t), *_halves(b))
    dta, dtb, dba, dbb = vjp(dout)
    return jnp.concatenate([dta, dtb], axis=1), jnp.concatenate([dba, dbb], axis=1)


def _swish2(g, u):
    return jax.nn.silu(g) * u


def _swish_gate(gu):
    return _swish2(*_halves(gu))


def _ln_silu(h, g, b):
    mu = jnp.mean(h, axis=-1, keepdims=True)
    var = jnp.mean(jnp.square(h - mu), axis=-1, keepdims=True)
    return jax.nn.silu((h - mu) * lax.rsqrt(var + EPS) * g + b)


def _dot(a, b, form="nn"):
    return lax.dot_general(a, b, _DIMS[form], preferred_element_type=F32)


def _split_dot(x, tri):
    hi = x.astype(BF16)
    lo = (x - hi.astype(F32)).astype(BF16)
    return _dot(hi, tri) + _dot(lo, tri)


def _sb_scores(qb, kb, scale, mask):
    z = _dot(qb, kb, "nt") * scale
    sp = jnp.maximum(z, 0.0) + jnp.log(1.0 + jnp.exp(-jnp.abs(z)))
    lk = -sp
    if mask is not None:
        lk = jnp.where(mask, lk, 0.0)
    return lk, z - sp


def _walk_left(i, state, step):
    def cond(ns):
        live = functools.reduce(jnp.maximum, [jnp.max(c) for c in ns[1][0]])
        return jnp.logical_and(ns[0] < i, live > EXP_FLOOR)

    def body(ns):
        return ns[0] + 1, step(i - 1 - ns[0], ns[1])

    return lax.while_loop(cond, body, (jnp.int32(0), state))[1]


def _pair_split(x):
    first = lax.broadcasted_iota(I32, x.shape, 1) < HEAD_DIM
    zero = jnp.zeros_like(x)
    return (jnp.where(first, x, zero), jnp.where(first, zero, x)), first


def _group_specs(s_dim, d_dim, bq, pairs):
    while d_dim % (pairs * HEAD_PAIR):
        pairs //= 2
    width = pairs * HEAD_PAIR
    ngroup = d_dim // width
    blk = pl.BlockSpec((bq, width), lambda p, i: (i, p))
    k_all = pl.BlockSpec((s_dim, width), lambda p, i: (0, ngroup + p))
    v_all = pl.BlockSpec((s_dim, width), lambda p, i: (0, 2 * ngroup + p))
    return pairs, ngroup, blk, k_all, v_all


def _pair_lanes(g):
    return slice(g * HEAD_PAIR, (g + 1) * HEAD_PAIR)


def _sb_fwd(qkv, base):
    s_dim, d_dim = qkv.shape[0], qkv.shape[1] // 3
    bq = ATT_BLOCK
    scale = HEAD_DIM ** -0.5
    pairs, ngroup, blk, k_all, v_all = _group_specs(s_dim, d_dim, bq, ATT_PAIRS_FWD)

    def body(q_ref, k_ref, v_ref, o_ref):
        i = pl.program_id(1)
        split = [_pair_split(q_ref[:, _pair_lanes(g)]) for g in range(pairs)]
        qs = [q for pair, _ in split for q in pair]
        first = split[0][1]
        row = lax.broadcasted_iota(I32, (bq, bq), 0)
        col = lax.broadcasted_iota(I32, (bq, bq), 1)
        later = (row > col).astype(BF16)
        causal = col < row

        def tile(j, cs, accs, mask):
            ks = pl.ds(pl.multiple_of(j * bq, bq), bq)
            new_c, new_acc = [], []
            for h, (qh, c, acc) in enumerate(zip(qs, cs, accs, strict=True)):
                lanes = _pair_lanes(h // 2)
                lk, lb = _sb_scores(qh, k_ref[ks, lanes], scale, mask)
                w = jnp.exp(lb + _split_dot(lk, later) + c)
                if mask is not None:
                    w = jnp.where(mask, w, 0.0)
                new_acc.append(acc + _dot(w.astype(BF16), v_ref[ks, lanes]))
                new_c.append(c + jnp.sum(lk, axis=1, keepdims=True))
            return tuple(new_c), tuple(new_acc)

        zero_c, zero_acc = jnp.zeros((bq, 1), F32), jnp.zeros((bq, HEAD_PAIR), F32)
        cs, accs = tile(i, (zero_c,) * len(qs), (zero_acc,) * len(qs), causal)
        cs, accs = _walk_left(i, (cs, accs), lambda j, s: tile(j, s[0], s[1], None))
        for g in range(pairs):
            o_ref[:, _pair_lanes(g)] = jnp.where(first, accs[2 * g], accs[2 * g + 1]).astype(o_ref.dtype)

    return pl.pallas_call(
        body, name=_name(base), grid=(ngroup, s_dim // bq), in_specs=[blk, k_all, v_all], out_specs=blk,
        out_shape=SDS((s_dim, d_dim), BF16), compiler_params=_params(("parallel", "arbitrary")),
    )(qkv, qkv, qkv)


def _sb_bwd(qkv, do, base):
    s_dim, d_dim = do.shape
    bq = ATT_BLOCK
    scale = HEAD_DIM ** -0.5
    pairs, ngroup, blk, k_all, v_all = _group_specs(s_dim, d_dim, bq, ATT_PAIRS_BWD)
    width = pairs * HEAD_PAIR
    nq = s_dim // bq

    def body(q_ref, k_ref, v_ref, do_ref, dq_ref, dk_hbm, dv_hbm, dk_ref, dv_ref, kept, out_sems):
        p, i = pl.program_id(0), pl.program_id(1)
        q_split = [_pair_split(q_ref[:, _pair_lanes(g)]) for g in range(pairs)]
        qs = [q for pair, _ in q_split for q in pair]
        dos = [d for g in range(pairs) for d in _pair_split(do_ref[:, _pair_lanes(g)])[0]]
        first = q_split[0][1]
        heads = tuple(range(2 * pairs))
        row = lax.broadcasted_iota(I32, (bq, bq), 0)
        col = lax.broadcasted_iota(I32, (bq, bq), 1)
        later = (row > col).astype(BF16)
        from_here = (row >= col).astype(BF16)
        causal = col < row

        @pl.when(i == 0)
        def _():
            dk_ref[...] = jnp.zeros_like(dk_ref)
            dv_ref[...] = jnp.zeros_like(dv_ref)

        def weights(h, kb, vb, c, mask):
            lk, lb = _sb_scores(qs[h], kb, scale, mask)
            w = jnp.exp(lb + _split_dot(lk, later) + c)
            if mask is not None:
                w = jnp.where(mask, w, 0.0)
            return lk, lb, w, _dot(dos[h], vb, "nt") * w

        def slot(h, n):
            return (h * ATT_KEEP + n) * len(_KEPT)

        def total(j, cs, tots, mask):
            ks = pl.ds(pl.multiple_of(j * bq, bq), bq)
            new_c, new_tot = [], []
            for h in heads:
                lanes = _pair_lanes(h // 2)
                tile = weights(h, k_ref[ks, lanes], v_ref[ks, lanes], cs[h], mask)

                @pl.when(i - j < ATT_KEEP)
                def _(tile=tile, h=h):
                    for a, val in enumerate(tile):
                        kept[slot(h, i - j) + a] = val

                new_c.append(cs[h] + jnp.sum(tile[0], axis=1, keepdims=True))
                new_tot.append(tots[h] + jnp.sum(tile[3], axis=1, keepdims=True))
            return tuple(new_c), tuple(new_tot)

        zero = jnp.zeros((bq, 1), F32)
        zeros = (zero,) * len(heads)
        cs, tots = total(i, zeros, zeros, causal)
        cs, tots = _walk_left(i, (cs, tots), lambda j, s: total(j, s[0], s[1], None))

        def grads(j, cs, rs, dqs, mask):
            ks = pl.ds(pl.multiple_of(j * bq, bq), bq)
            new_c, new_r, new_dq = [], [], []
            dk_t, dv_t = [None] * pairs, [None] * pairs
            for h in heads:
                g, lanes = h // 2, _pair_lanes(h // 2)
                kb = k_ref[ks, lanes]
                lk, lb, w, de = lax.cond(
                    i - j < ATT_KEEP,
                    lambda h=h: tuple(kept[slot(h, i - j) + a] for a in range(len(_KEPT))),
                    lambda h=h, kb=kb, lanes=lanes: weights(h, kb, v_ref[ks, lanes], cs[h], mask))
                before = tots[h] - (rs[h] + _split_dot(de, from_here))
                dz = de * jnp.exp(lk) - jnp.exp(lb) * before
                if mask is not None:
                    dz = jnp.where(mask, dz, 0.0)
                dzb = (dz * scale).astype(BF16)
                dk_h, dv_h = _dot(dzb, qs[h], "tn"), _dot(w.astype(BF16), dos[h], "tn")
                dk_t[g], dv_t[g] = (dk_h, dv_h) if dk_t[g] is None else (dk_t[g] + dk_h, dv_t[g] + dv_h)
                new_dq.append(dqs[h] + _dot(dzb, kb))
                new_c.append(cs[h] + jnp.sum(lk, axis=1, keepdims=True))
                new_r.append(rs[h] + jnp.sum(de, axis=1, keepdims=True))
            for g in range(pairs):
                dk_ref[ks, _pair_lanes(g)] += dk_t[g]
                dv_ref[ks, _pair_lanes(g)] += dv_t[g]
            return tuple(new_c), tuple(new_r), tuple(new_dq)

        zero_dq = jnp.zeros((bq, HEAD_PAIR), F32)
        state = grads(i, zeros, zeros, (zero_dq,) * len(heads), causal)
        _, _, dqs = _walk_left(i, state, lambda j, s: grads(j, s[0], s[1], s[2], None))
        for g in range(pairs):
            dq_ref[:, _pair_lanes(g)] = jnp.where(first, dqs[2 * g], dqs[2 * g + 1])

        @pl.when(i == nq - 1)
        def _():
            cols = pl.ds(pl.multiple_of(p * width, width), width)
            out = [pltpu.make_async_copy(dk_ref, dk_hbm.at[:, cols], out_sems.at[0]),
                   pltpu.make_async_copy(dv_ref, dv_hbm.at[:, cols], out_sems.at[1])]
            for cp in out:
                cp.start()
            for cp in out:
                cp.wait()

    out = SDS((s_dim, d_dim), F32)
    acc = pltpu.VMEM((s_dim, width), F32)
    return pl.pallas_call(
        body, name=_name(base), grid=(ngroup, nq), in_specs=[blk, k_all, v_all, blk],
        out_specs=[blk, _ANY, _ANY], out_shape=[out, out, out],
        scratch_shapes=[acc, acc, pltpu.VMEM((2 * pairs * ATT_KEEP * len(_KEPT), bq, bq), F32),
                        pltpu.SemaphoreType.DMA((2,))],
        compiler_params=_params(("parallel", "arbitrary")),
    )(qkv, qkv, qkv, do)


def _cmul(ar, ai, br, bi):
    return ar * br - ai * bi, ar * bi + ai * br


def _s5_prep_fn(lr, li, ldt, br, bi):
    dt = jnp.exp(ldt)
    mag = jnp.exp(lr * dt)
    ar, ai = mag * jnp.cos(li * dt), mag * jnp.sin(li * dt)
    den = lr * lr + li * li
    er = ((ar - 1) * lr + ai * li) / den
    ei = (ai * lr - (ar - 1) * li) / den
    return ar, ai, er * br - ei * bi, er * bi + ei * br


def _s5_prep(lr, li, ldt, br, bi, base):
    col, mat = SDS(lr.shape, F32), SDS(br.shape, F32)
    return _rowwise(lambda *a: (_s5_prep_fn(*a), ()), [lr, li, ldt, br, bi], [], [col, col, mat, mat], [], base, tr=512)


def _s5_prep_bwd(lr, li, ldt, br, bi, cots, base):
    col, mat = SDS(lr.shape, F32), SDS(br.shape, F32)

    def bwd(lr, li, ldt, br, bi, c0, c1, c2, c3):
        _, vjp = jax.vjp(_s5_prep_fn, lr, li, ldt, br, bi)
        return vjp((c0, c1, c2, c3)), ()

    return _rowwise(bwd, [lr, li, ldt, br, bi, *cots], [], [col, col, col, mat, mat], [], base, tr=512)


def _scan_tables(ar_ref, ai_ref, sl, conj):
    shape = (SUBLANES, S5_BLOCK_ST)
    a_r = jnp.broadcast_to(ar_ref[:, sl], shape)
    a_i = jnp.broadcast_to(ai_ref[:, sl], shape)
    if conj:
        a_i = -a_i
    a2 = _cmul(a_r, a_i, a_r, a_i)
    a4 = _cmul(*a2, *a2)
    row = lax.broadcasted_iota(I32, shape, 0)
    p_r, p_i, t_r, t_i = a_r, a_i, a_r, a_i
    for k in range(1, SUBLANES):
        p_r, p_i = _cmul(p_r, p_i, a_r, a_i)
        sel = (row == (SUBLANES - 1 - k)) if conj else (row == k)
        t_r, t_i = jnp.where(sel, p_r, t_r), jnp.where(sel, p_i, t_i)
    if conj:
        t_r, t_i = jnp.where(row == SUBLANES - 1, a_r, t_r), jnp.where(row == SUBLANES - 1, a_i, t_i)
    return row, (a_r, a_i), a2, a4, (t_r, t_i)


def _scan_group(xr, xi, row, a1, a2, a4, table, cr, ci, reverse):
    for s, (mr, mi) in ((1, a1), (2, a2), (4, a4)):
        if reverse:
            keep = row < SUBLANES - s
            sr, si = pltpu.roll(xr, SUBLANES - s, 0), pltpu.roll(xi, SUBLANES - s, 0)
        else:
            keep = row >= s
            sr, si = pltpu.roll(xr, s, 0), pltpu.roll(xi, s, 0)
        sr, si = jnp.where(keep, sr, 0.0), jnp.where(keep, si, 0.0)
        xr, xi = xr + mr * sr - mi * si, xi + mr * si + mi * sr
    tr, ti = table
    return xr + tr * cr - ti * ci, xi + tr * ci + ti * cr


def _s5_fwd(u, ar, ai, wr, wi, cr_w, ci_w, dsk, base, chunk=256):
    s_dim, d_dim = u.shape
    nb = d_dim // S5_BLOCK_CH
    n_st = nb * S5_BLOCK_ST
    chunk = _tile(s_dim, chunk, SUBLANES)
    nch = s_dim // chunk

    def body(u_ref, ar_ref, ai_ref, wr_ref, wi_ref, cr_ref, ci_ref, d_ref, y_ref, gy_ref, sr_ref, si_ref,
             xr_s, xi_s, car_r, car_i):
        @pl.when(pl.program_id(0) == 0)
        def _():
            car_r[...] = jnp.zeros_like(car_r)
            car_i[...] = jnp.zeros_like(car_i)

        sr_ref[...] = car_r[...]
        si_ref[...] = car_i[...]
        for b in range(nb):
            ub = u_ref[:, b * S5_BLOCK_CH:(b + 1) * S5_BLOCK_CH].astype(BF16)
            sl = slice(b * S5_BLOCK_ST, (b + 1) * S5_BLOCK_ST)
            xr_s[:, sl] = _dot(ub, wr_ref[b])
            xi_s[:, sl] = _dot(ub, wi_ref[b])
        for b in range(nb):
            sl = slice(b * S5_BLOCK_ST, (b + 1) * S5_BLOCK_ST)
            row, a1, a2, a4, table = _scan_tables(ar_ref, ai_ref, sl, False)

            def group(g, carry, sl=sl, row=row, a1=a1, a2=a2, a4=a4, table=table):
                rows = pl.ds(pl.multiple_of(g * SUBLANES, SUBLANES), SUBLANES)
                xr, xi = _scan_group(xr_s[rows, sl], xi_s[rows, sl], row, a1, a2, a4, table, carry[0], carry[1], False)
                xr_s[rows, sl] = xr
                xi_s[rows, sl] = xi
                last = (SUBLANES, S5_BLOCK_ST)
                return (jnp.broadcast_to(xr[SUBLANES - 1:, :], last), jnp.broadcast_to(xi[SUBLANES - 1:, :], last))

            c_r, c_i = lax.fori_loop(0, chunk // SUBLANES, group, (car_r[:, sl], car_i[:, sl]))
            car_r[:, sl] = c_r
            car_i[:, sl] = c_i
        for b in range(nb):
            sl = slice(b * S5_BLOCK_ST, (b + 1) * S5_BLOCK_ST)
            ch = slice(b * S5_BLOCK_CH, (b + 1) * S5_BLOCK_CH)
            y = _dot(xr_s[:, sl].astype(BF16), cr_ref[b]) - _dot(xi_s[:, sl].astype(BF16), ci_ref[b])
            y = y + d_ref[:, ch] * u_ref[:, ch]
            y_ref[:, ch] = y
            gy_ref[:, ch] = jax.nn.gelu(y).astype(BF16)

    rows = pl.BlockSpec((chunk, d_dim), lambda i: (i, 0))
    vec = pl.BlockSpec((1, n_st), lambda i: (0, 0))
    w_in = pl.BlockSpec((nb, S5_BLOCK_CH, S5_BLOCK_ST), lambda i: (0, 0, 0))
    w_out = pl.BlockSpec((nb, S5_BLOCK_ST, S5_BLOCK_CH), lambda i: (0, 0, 0))
    st = pl.BlockSpec((SUBLANES, n_st), lambda i: (i, 0))
    return pl.pallas_call(
        body, name=_name(base), grid=(nch,),
        in_specs=[rows, vec, vec, w_in, w_in, w_out, w_out, pl.BlockSpec((1, d_dim), lambda i: (0, 0))],
        out_specs=[rows, rows, st, st],
        out_shape=[SDS((s_dim, d_dim), F32), SDS((s_dim, d_dim), BF16),
                   SDS((nch * SUBLANES, n_st), F32), SDS((nch * SUBLANES, n_st), F32)],
        scratch_shapes=[pltpu.VMEM((chunk, n_st), F32), pltpu.VMEM((chunk, n_st), F32),
                        pltpu.VMEM((SUBLANES, n_st), F32), pltpu.VMEM((SUBLANES, n_st), F32)],
        compiler_params=_params(("arbitrary",)),
    )(u, ar, ai, wr, wi, cr_w, ci_w, dsk)


def _s5_bwd(u, dy, st_r, st_i, ar, ai, wr, wi, cr_w, ci_w, dsk, base, chunk):
    s_dim, d_dim = u.shape
    nb = d_dim // S5_BLOCK_CH
    n_st = nb * S5_BLOCK_ST
    nch = s_dim // chunk
    ng = chunk // SUBLANES

    def body(u_ref, dy_ref, sr_ref, si_ref, ar_ref, ai_ref, wr_ref, wi_ref, cr_ref, ci_ref, d_ref,
             du_ref, dwr_ref, dwi_ref, dcr_ref, dci_ref, dar_ref, dai_ref, dd_ref,
             xr_s, xi_s, lr_s, li_s, car_r, car_i):
        first = pl.program_id(0) == 0

        @pl.when(first)
        def _():
            car_r[...] = jnp.zeros_like(car_r)
            car_i[...] = jnp.zeros_like(car_i)
            for ref in (dwr_ref, dwi_ref, dcr_ref, dci_ref, dar_ref, dai_ref, dd_ref):
                ref[...] = jnp.zeros_like(ref)

        xr_s[pl.ds(0, SUBLANES), :] = sr_ref[...]
        xi_s[pl.ds(0, SUBLANES), :] = si_ref[...]
        body_rows = pl.ds(SUBLANES, chunk)
        dyb = dy_ref[...]
        dd_ref[...] += jnp.sum(dyb * u_ref[...], axis=0, keepdims=True)
        for b in range(nb):
            ch = slice(b * S5_BLOCK_CH, (b + 1) * S5_BLOCK_CH)
            sl = slice(b * S5_BLOCK_ST, (b + 1) * S5_BLOCK_ST)
            ub = u_ref[:, ch].astype(BF16)
            dyc = dyb[:, ch].astype(BF16)
            xr_s[body_rows, sl] = _dot(ub, wr_ref[b])
            xi_s[body_rows, sl] = _dot(ub, wi_ref[b])
            lr_s[:, sl] = _dot(dyc, cr_ref[b], "nt")
            li_s[:, sl] = -_dot(dyc, ci_ref[b], "nt")
        for b in range(nb):
            sl = slice(b * S5_BLOCK_ST, (b + 1) * S5_BLOCK_ST)
            row, a1, a2, a4, table = _scan_tables(ar_ref, ai_ref, sl, False)

            def fwd_group(g, carry, sl=sl, row=row, a1=a1, a2=a2, a4=a4, table=table):
                rows = pl.ds(pl.multiple_of((g + 1) * SUBLANES, SUBLANES), SUBLANES)
                xr, xi = _scan_group(xr_s[rows, sl], xi_s[rows, sl], row, a1, a2, a4, table, carry[0], carry[1], False)
                xr_s[rows, sl] = xr
                xi_s[rows, sl] = xi
                last = (SUBLANES, S5_BLOCK_ST)
                return (jnp.broadcast_to(xr[SUBLANES - 1:, :], last), jnp.broadcast_to(xi[SUBLANES - 1:, :], last))

            lax.fori_loop(0, ng, fwd_group, (sr_ref[:, sl], si_ref[:, sl]))
            row, a1, a2, a4, table = _scan_tables(ar_ref, ai_ref, sl, True)

            def bwd_group(n, carry, sl=sl, row=row, a1=a1, a2=a2, a4=a4, table=table):
                g = ng - 1 - n
                c_r, c_i, acc_r, acc_i = carry
                rows = pl.ds(pl.multiple_of(g * SUBLANES, SUBLANES), SUBLANES)
                lr, li = _scan_group(lr_s[rows, sl], li_s[rows, sl], row, a1, a2, a4, table, c_r, c_i, True)
                lr_s[rows, sl] = lr
                li_s[rows, sl] = li
                prev = pl.ds(pl.multiple_of(g * SUBLANES, SUBLANES), SUBLANES)
                here = pl.ds(pl.multiple_of((g + 1) * SUBLANES, SUBLANES), SUBLANES)
                px_r = jnp.where(row >= 1, pltpu.roll(xr_s[here, sl], 1, 0), pltpu.roll(xr_s[prev, sl], 1, 0))
                px_i = jnp.where(row >= 1, pltpu.roll(xi_s[here, sl], 1, 0), pltpu.roll(xi_s[prev, sl], 1, 0))
                acc_r = acc_r + lr * px_r + li * px_i
                acc_i = acc_i + li * px_r - lr * px_i
                first_row = (SUBLANES, S5_BLOCK_ST)
                return (jnp.broadcast_to(lr[:1, :], first_row), jnp.broadcast_to(li[:1, :], first_row), acc_r, acc_i)

            zero = jnp.zeros((SUBLANES, S5_BLOCK_ST), F32)
            c_r, c_i, acc_r, acc_i = lax.fori_loop(0, ng, bwd_group, (car_r[:, sl], car_i[:, sl], zero, zero))
            car_r[:, sl] = c_r
            car_i[:, sl] = c_i
            dar_ref[:, sl] += acc_r
            dai_ref[:, sl] += acc_i
        for b in range(nb):
            ch = slice(b * S5_BLOCK_CH, (b + 1) * S5_BLOCK_CH)
            sl = slice(b * S5_BLOCK_ST, (b + 1) * S5_BLOCK_ST)
            ub = u_ref[:, ch].astype(BF16)
            dyc = dyb[:, ch].astype(BF16)
            lrb, lib = lr_s[:, sl].astype(BF16), li_s[:, sl].astype(BF16)
            du_ref[:, ch] = (_dot(lrb, wr_ref[b], "nt") + _dot(lib, wi_ref[b], "nt") + d_ref[:, ch] * dyb[:, ch])
            dwr_ref[b] += _dot(ub, lrb, "tn")
            dwi_ref[b] += _dot(ub, lib, "tn")
            dcr_ref[b] += _dot(xr_s[body_rows, sl].astype(BF16), dyc, "tn")
            dci_ref[b] -= _dot(xi_s[body_rows, sl].astype(BF16), dyc, "tn")

    rev = lambda i: (nch - 1 - i, 0)
    rows = pl.BlockSpec((chunk, d_dim), rev)
    st = pl.BlockSpec((SUBLANES, n_st), rev)
    vec = pl.BlockSpec((1, n_st), lambda i: (0, 0))
    w_in = pl.BlockSpec((nb, S5_BLOCK_CH, S5_BLOCK_ST), lambda i: (0, 0, 0))
    w_out = pl.BlockSpec((nb, S5_BLOCK_ST, S5_BLOCK_CH), lambda i: (0, 0, 0))
    acc8 = pl.BlockSpec((SUBLANES, n_st), lambda i: (0, 0))
    dvec = pl.BlockSpec((1, d_dim), lambda i: (0, 0))
    return pl.pallas_call(
        body, name=_name(base), grid=(nch,),
        in_specs=[rows, rows, st, st, vec, vec, w_in, w_in, w_out, w_out, dvec],
        out_specs=[rows, w_in, w_in, w_out, w_out, acc8, acc8, dvec],
        out_shape=[SDS((s_dim, d_dim), F32),
                   SDS((nb, S5_BLOCK_CH, S5_BLOCK_ST), F32), SDS((nb, S5_BLOCK_CH, S5_BLOCK_ST), F32),
                   SDS((nb, S5_BLOCK_ST, S5_BLOCK_CH), F32), SDS((nb, S5_BLOCK_ST, S5_BLOCK_CH), F32),
                   SDS((SUBLANES, n_st), F32), SDS((SUBLANES, n_st), F32), SDS((1, d_dim), F32)],
        scratch_shapes=[pltpu.VMEM((chunk + SUBLANES, n_st), F32), pltpu.VMEM((chunk + SUBLANES, n_st), F32),
                        pltpu.VMEM((chunk, n_st), F32), pltpu.VMEM((chunk, n_st), F32),
                        pltpu.VMEM((SUBLANES, n_st), F32), pltpu.VMEM((SUBLANES, n_st), F32)],
        compiler_params=_params(("arbitrary",)),
    )(u, dy, st_r, st_i, ar, ai, wr, wi, cr_w, ci_w, dsk)


def _block_diag(w, transpose):
    g = w.shape[0]
    nb = g // S5_BLOCK_GROUPS
    eye = jnp.eye(S5_BLOCK_GROUPS, dtype=w.dtype)
    if transpose:
        w = w.reshape(nb, S5_BLOCK_GROUPS, S5_GROUP, S5_STATE)
        return jnp.einsum("bgcp,gh->bgphc", w, eye).reshape(nb, S5_BLOCK_ST, S5_BLOCK_CH)
    w = w.reshape(nb, S5_BLOCK_GROUPS, S5_STATE, S5_GROUP)
    return jnp.einsum("bgpc,gh->bgchp", w, eye).reshape(nb, S5_BLOCK_CH, S5_BLOCK_ST)


def _block_diag_extract(w, transpose):
    nb = w.shape[0]
    eye = jnp.eye(S5_BLOCK_GROUPS, dtype=w.dtype)
    if transpose:
        w = w.reshape(nb, S5_BLOCK_GROUPS, S5_STATE, S5_BLOCK_GROUPS, S5_GROUP)
        return jnp.einsum("bgphc,gh->bgcp", w, eye).reshape(nb * S5_BLOCK_GROUPS, S5_GROUP, S5_STATE)
    w = w.reshape(nb, S5_BLOCK_GROUPS, S5_GROUP, S5_BLOCK_GROUPS, S5_STATE)
    return jnp.einsum("bgchp,gh->bgpc", w, eye).reshape(nb * S5_BLOCK_GROUPS, S5_STATE, S5_GROUP)


def _dwconv_fwd(h, w, b, base, tr=512):
    s_dim, d_dim = h.shape
    tr = _tile(s_dim, tr, CONV_HALO)
    per = tr // CONV_HALO

    def body(h_ref, halo_ref, w_ref, b_ref, o_ref, buf):
        i = pl.program_id(0)
        buf[pl.ds(0, CONV_HALO), :] = jnp.where(i > 0, halo_ref[...], 0.0)
        buf[pl.ds(CONV_HALO, tr), :] = h_ref[...]
        acc = jnp.broadcast_to(b_ref[...], (tr, d_dim))
        for k in range(CONV_WIDTH):
            acc = acc + w_ref[pl.ds(k, 1), :] * buf[pl.ds(CONV_HALO - (CONV_WIDTH - 1) + k, tr), :]
        o_ref[...] = acc

    return pl.pallas_call(
        body, name=_name(base), grid=(s_dim // tr,),
        in_specs=[pl.BlockSpec((tr, d_dim), lambda i: (i, 0)),
                  pl.BlockSpec((CONV_HALO, d_dim), lambda i: (jnp.maximum(i * per - 1, 0), 0)),
                  pl.BlockSpec((CONV_WIDTH, d_dim), lambda i: (0, 0)), pl.BlockSpec((1, d_dim), lambda i: (0, 0))],
        out_specs=pl.BlockSpec((tr, d_dim), lambda i: (i, 0)), out_shape=SDS((s_dim, d_dim), F32),
        scratch_shapes=[pltpu.VMEM((tr + CONV_HALO, d_dim), F32)], compiler_params=_params(("arbitrary",)),
    )(h, h, w, b)


def _dwconv_bwd(h, dout, w, base, tr=512):
    s_dim, d_dim = h.shape
    tr = _tile(s_dim, tr, CONV_HALO)
    per = tr // CONV_HALO
    n = s_dim // tr
    last_halo = s_dim // CONV_HALO - 1

    def body(h_ref, hhalo_ref, d_ref, dhalo_ref, w_ref, dh_ref, dw_ref, db_ref, hbuf, dbuf):
        i = pl.program_id(0)
        hbuf[pl.ds(0, CONV_HALO), :] = jnp.where(i > 0, hhalo_ref[...], 0.0)
        hbuf[pl.ds(CONV_HALO, tr), :] = h_ref[...]
        dbuf[pl.ds(0, tr), :] = d_ref[...]
        dbuf[pl.ds(tr, CONV_HALO), :] = jnp.where(i < n - 1, dhalo_ref[...], 0.0)
        dout_t = d_ref[...]
        acc = jnp.zeros((tr, d_dim), F32)
        dws = []
        for k in range(CONV_WIDTH):
            acc = acc + w_ref[pl.ds(k, 1), :] * dbuf[pl.ds(CONV_WIDTH - 1 - k, tr), :]
            shifted = hbuf[pl.ds(CONV_HALO - (CONV_WIDTH - 1) + k, tr), :]
            dws.append(jnp.sum(dout_t * shifted, axis=0, keepdims=True))
        dh_ref[...] = acc
        dw_t = jnp.concatenate(dws, axis=0)
        db_t = jnp.sum(dout_t, axis=0, keepdims=True)

        @pl.when(i == 0)
        def _():
            dw_ref[...] = dw_t
            db_ref[...] = db_t

        @pl.when(i > 0)
        def _():
            dw_ref[...] += dw_t
            db_ref[...] += db_t

    tile = pl.BlockSpec((tr, d_dim), lambda i: (i, 0))
    before = pl.BlockSpec((CONV_HALO, d_dim), lambda i: (jnp.maximum(i * per - 1, 0), 0))
    after = pl.BlockSpec((CONV_HALO, d_dim), lambda i: (jnp.minimum((i + 1) * per, last_halo), 0))
    return pl.pallas_call(
        body, name=_name(base), grid=(n,),
        in_specs=[tile, before, tile, after, pl.BlockSpec((CONV_WIDTH, d_dim), lambda i: (0, 0))],
        out_specs=[tile, pl.BlockSpec((CONV_WIDTH, d_dim), lambda i: (0, 0)), pl.BlockSpec((1, d_dim), lambda i: (0, 0))],
        out_shape=[SDS((s_dim, d_dim), F32), SDS((CONV_WIDTH, d_dim), F32), SDS((1, d_dim), F32)],
        scratch_shapes=[pltpu.VMEM((tr + CONV_HALO, d_dim), F32), pltpu.VMEM((tr + CONV_HALO, d_dim), F32)],
        compiler_params=_params(("arbitrary",)),
    )(h, h, dout, dout, w)


def _local_step(x, mod, w, target):
    s_dim, d_dim = x.shape
    depth = mod.shape[0]
    row = lambda dt=F32, n=d_dim: SDS((s_dim, n), dt)
    vec = lambda n=d_dim: SDS((1, n), F32)
    mods = [[mod[l:l + 1, k * d_dim:(k + 1) * d_dim] for k in range(6)] for l in range(depth)]
    ng = lambda l, k: w["norm_g"][l, k:k + 1, :]
    saved = []
    h = x
    for l in range(depth):
        sh_m, sc_m, g_m, sh_f, sc_f, g_f = mods[l]
        kind, j = l % 3, l // 3
        s = {"h0": h}
        if kind == 1:
            (u,) = _rowwise(lambda h, g, a, b: ((_pre(h, g, a, b),), ()), [h], [ng(l, 0), sh_m, sc_m], [row()], [], "pre")
        else:
            (u,) = _rowwise(lambda h, g, a, b: ((_pre(h, g, a, b),), ()), [h], [ng(l, 0), sh_m, sc_m], [row(BF16)], [], "pre")
        s["u"] = u
        bias = None
        if kind == 0:
            qkv = _matmul(u, w["sb_w_qkv"][j], "nn", BF16, "qkv")
            o = _sb_fwd(qkv, "sb_fwd")
            m = _matmul(o, w["sb_w_o"][j], "nn", F32, "wo")
            s.update(qkv=qkv, o=o)
        elif kind == 1:
            p = w["s5"][j]
            y_pre, gy, st_r, st_i = _s5_fwd(u, p["ar"], p["ai"], p["wr"], p["wi"], p["cr"], p["ci"], w["s5_d"][j:j + 1],
                                            "s5_fwd")
            t = _matmul(gy, w["s5_w_glu"][j], "nn", F32, "s5_glu")
            b_glu = w["s5_b_glu"][j:j + 1]
            (m,) = _rowwise(lambda t, b: ((_glu(t, b),), ()), [t], [b_glu], [row()], [], "s5_gate")
            s.update(y_pre=y_pre, gy=gy, st_r=st_r, st_i=st_i, t=t)
        else:
            t = _matmul(u, w["cv_w_pw1"][j], "nn", F32, "pw1")
            (hg,) = _rowwise(lambda t, b: ((_glu(t, b),), ()), [t], [w["cv_b_pw1"][j:j + 1]], [row()], [], "cv_glu")
            hc = _dwconv_fwd(hg, w["cv_w_dw"][j], w["cv_b_dw"][j:j + 1], "dwconv")
            (hs,) = _rowwise(lambda h, g, b: ((_ln_silu(h, g, b),), ()), [hc],
                             [w["cv_ln_g"][j:j + 1], w["cv_ln_b"][j:j + 1]], [row(BF16)], [], "cv_ln")
            m = _matmul(hs, w["cv_w_pw2"][j], "nn", F32, "pw2")
            bias = w["cv_b_pw2"][j:j + 1]
            s.update(t=t, hg=hg, hc=hc, hs=hs)
        s["m"] = m
        if bias is None:
            h1, u2 = _rowwise(
                lambda h, m, gate, g1, g2, a, b: ((lambda h1: (h1, _pre(h1, g2, a, b)))(_post(h, m, gate, g1)), ()),
                [h, m], [g_m, ng(l, 1), ng(l, 2), sh_f, sc_f], [row(), row(BF16)], [], "post_pre")
        else:
            h1, u2 = _rowwise(
                lambda h, m, gate, g1, g2, a, b, bb: ((lambda h1: (h1, _pre(h1, g2, a, b)))(_post(h, m + bb, gate, g1)), ()),
                [h, m], [g_m, ng(l, 1), ng(l, 2), sh_f, sc_f, bias], [row(), row(BF16)], [], "post_pre")
        f_dim = w["ffn_w_gu"].shape[2] // 2
        gu = _matmul(u2, w["ffn_w_gu"][l], "nn", BF16, "ffn_gu")
        (act,) = _rowwise(lambda gu: ((_swish_gate(_f(gu)),), ()), [gu], [], [row(BF16, f_dim)], [], "ffn_act")
        f = _matmul(act, w["ffn_w_down"][l], "nn", F32, "ffn_down")
        (h,) = _rowwise(lambda h, m, gate, g: ((_post(h, m, gate, g),), ()), [h1, f], [g_f, ng(l, 3)], [row()], [], "post")
        s.update(h1=h1, u2=u2, gu=gu, act=act, f=f, bias=bias)
        saved.append(s)
    dh, loss_cols = _rowwise(
        lambda y, t: ((((y - t) * (1.0 / d_dim)),), (jnp.sum(jnp.square(y - t), axis=0, keepdims=True),)),
        [h, target], [], [row()], [vec()], "loss")
    grads = {}
    dmod = [[None] * 6 for _ in range(depth)]
    dnorm = [[None] * 4 for _ in range(depth)]

    def add(name, idx, val):
        grads.setdefault(name, {})[idx] = val

    for l in reversed(range(depth)):
        sh_m, sc_m, g_m, sh_f, sc_f, g_f = mods[l]
        kind, j = l % 3, l // 3
        s = saved[l]

        def post_bwd(h, m, dh, gate, g):
            _, vjp = jax.vjp(_post, h, m, gate, g)
            _, dm, dgate, dg = vjp(dh)
            return (dm,), (dgate, dg)

        df, dmod[l][5], dnorm[l][3] = _rowwise(post_bwd, [s["h1"], s["f"], dh], [g_f, ng(l, 3)],
                                               [row(BF16)], [vec(), vec()], "post_bwd")
        f_dim = s["act"].shape[1]
        add("ffn_w_down", l, _matmul(s["act"], df, "tn", F32, "d_down"))
        dact = _matmul(df, w["ffn_w_down"][l], "nt", BF16, "d_act")

        def act_bwd(gu, da):
            _, vjp = jax.vjp(_swish2, *_halves(_f(gu)))
            return (jnp.concatenate(vjp(_f(da)), axis=1),), ()

        (dgu,) = _rowwise(act_bwd, [s["gu"], dact], [], [row(BF16, 2 * f_dim)], [], "act_bwd")
        add("ffn_w_gu", l, _matmul(s["u2"], dgu, "tn", F32, "d_gu"))
        du2 = _matmul(dgu, w["ffn_w_gu"][l], "nt", F32, "d_u2")
        bias = s["bias"]

        def post_pre_bwd(h, m, dh1, du2, gate, g1, g2, a, b, *bb):
            def fwd(h, m, gate, g1, g2, a, b):
                mm = m + bb[0] if bb else m
                h1 = _post(h, mm, gate, g1)
                return h1, _pre(h1, g2, a, b)
            _, vjp = jax.vjp(fwd, h, m, gate, g1, g2, a, b)
            dh0, dm, dgate, dg1, dg2, da, db = vjp((dh1, du2))
            return (dh0, dm), (dgate, dg1, dg2, da, db, jnp.sum(dm, axis=0, keepdims=True))

        vecs = [g_m, ng(l, 1), ng(l, 2), sh_f, sc_f] + ([bias] if bias is not None else [])
        dh, dm, dmod[l][2], dnorm[l][1], dnorm[l][2], dmod[l][3], dmod[l][4], dm_cols = _rowwise(
            post_pre_bwd, [s["h0"], s["m"], dh, du2], vecs, [row(), row(BF16)], [vec()] * 6, "post_pre_bwd")
        if kind == 0:
            add("sb_w_o", j, _matmul(s["o"], dm, "tn", F32, "d_wo"))
            do = _matmul(dm, w["sb_w_o"][j], "nt", BF16, "d_o")
            dqkv = jnp.concatenate(_sb_bwd(s["qkv"], do, "sb_bwd"), axis=1).astype(BF16)
            add("sb_w_qkv", j, _matmul(s["u"], dqkv, "tn", F32, "d_wqkv"))
            du = _matmul(dqkv, w["sb_w_qkv"][j], "nt", F32, "d_u")
        elif kind == 1:
            p = w["s5"][j]
            b_glu = w["s5_b_glu"][j:j + 1]

            def gate_bwd(t, dm, b):
                dt, db = _glu_bwd(t, _f(dm), b)
                return (dt,), (db,)

            dt, db_glu = _rowwise(gate_bwd, [s["t"], dm], [b_glu], [row(BF16, 2 * d_dim)], [vec(2 * d_dim)], "s5_gate_bwd")
            add("s5_b_glu", j, db_glu)
            add("s5_w_glu", j, _matmul(s["gy"], dt, "tn", F32, "d_wglu"))
            dgy = _matmul(dt, w["s5_w_glu"][j], "nt", F32, "d_gy")

            def gelu_bwd(y, dg):
                _, vjp = jax.vjp(jax.nn.gelu, y)
                return (vjp(dg)[0],), ()

            (dy,) = _rowwise(gelu_bwd, [s["y_pre"], dgy], [], [row()], [], "gelu_bwd")
            chunk = s_dim // (s["st_r"].shape[0] // SUBLANES)
            du, dwr, dwi, dcr, dci, dar, dai, dd = _s5_bwd(
                s["u"], dy, s["st_r"], s["st_i"], p["ar"], p["ai"], p["wr"], p["wi"], p["cr"], p["ci"],
                w["s5_d"][j:j + 1], "s5_bwd", chunk)
            add("s5_d", j, dd)
            add("s5_raw", j, (dwr, dwi, dcr, dci, dar, dai))
        else:
            add("cv_w_pw2", j, _matmul(s["hs"], dm, "tn", F32, "d_pw2"))
            dhs = _matmul(dm, w["cv_w_pw2"][j], "nt", F32, "d_hs")
            ln_g, ln_b = w["cv_ln_g"][j:j + 1], w["cv_ln_b"][j:j + 1]

            def ln_bwd(hc, dhs, g, b):
                _, vjp = jax.vjp(_ln_silu, hc, g, b)
                dhc, dg, db = vjp(dhs)
                return (dhc,), (dg, db)

            dhc, dln_g, dln_b = _rowwise(ln_bwd, [s["hc"], dhs], [ln_g, ln_b], [row()], [vec()] * 2, "ln_bwd")
            add("cv_ln_g", j, dln_g)
            add("cv_ln_b", j, dln_b)
            add("cv_b_pw2", j, dm_cols)
            dhg, dw_dw, db_dw = _dwconv_bwd(s["hg"], dhc, w["cv_w_dw"][j], "dwconv_bwd")
            add("cv_w_dw", j, dw_dw)
            add("cv_b_dw", j, db_dw)
            b_pw1 = w["cv_b_pw1"][j:j + 1]

            def glu_bwd(t, dhg, b):
                dt, db = _glu_bwd(t, dhg, b)
                return (dt,), (db,)

            dt, db_pw1 = _rowwise(glu_bwd, [s["t"], dhg], [b_pw1], [row(BF16, 2 * d_dim)], [vec(2 * d_dim)], "cv_glu_bwd")
            add("cv_b_pw1", j, db_pw1)
            add("cv_w_pw1", j, _matmul(s["u"], dt, "tn", F32, "d_pw1"))
            du = _matmul(dt, w["cv_w_pw1"][j], "nt", F32, "d_u")

        def pre_bwd(h, dh, du, g, a, b):
            _, vjp = jax.vjp(_pre, h, g, a, b)
            dh0, dg, da, db = vjp(du)
            return (dh + dh0,), (dg, da, db)

        dh, dnorm[l][0], dmod[l][0], dmod[l][1] = _rowwise(pre_bwd, [s["h0"], dh, du], [ng(l, 0), sh_m, sc_m],
                                                            [row()], [vec()] * 3, "pre_bwd")
    grads["norm_g"] = jnp.stack([jnp.concatenate(r, axis=0) for r in dnorm])
    dmod_arr = jnp.concatenate([jnp.concatenate(r, axis=1) for r in dmod], axis=0)
    return loss_cols, dh, dmod_arr, grads


def _position():
    return lax.axis_index("x"), lax.axis_index("y"), lax.axis_index("c")


def _all_gather8(x, base):
    m, n = x.shape
    space = pltpu.VMEM

    def body(x_ref, out_ref, send_sems, recv_sems, local_sem):
        px, py, pc = _position()
        me, sibling = (px, py, pc), (px, py, 1 - pc)
        chips = [(1 - px, py), (px, 1 - py), (1 - px, 1 - py)]
        src_mine = x_ref

        def rows(bx, by, bc):
            return out_ref.at[pl.ds((4 * bx + 2 * by + bc) * m, m), :]

        def copy(k, block, to, src=None):
            return pltpu.make_async_remote_copy(
                src_ref=rows(*block) if src is None else src, dst_ref=rows(*block),
                send_sem=send_sems.at[k], recv_sem=recv_sems.at[k], device_id=to, device_id_type=MESH)

        mine = pltpu.make_async_copy(src_mine, rows(*me), local_sem)
        mine.start()
        first = [copy(0, me, sibling, src=src_mine)]
        first += [copy(1 + j, me, (*chip, pc), src=src_mine) for j, chip in enumerate(chips)]
        for cp in first:
            cp.start()
        passed = [copy(4 + j, (*chip, pc), sibling) for j, chip in enumerate(chips)]
        for j, chip in enumerate(chips):
            copy(1 + j, (*chip, pc), me).wait_recv()
            passed[j].start()
        copy(0, sibling, me).wait_recv()
        for j, chip in enumerate(chips):
            copy(4 + j, (*chip, 1 - pc), me).wait_recv()
        for cp in first + passed:
            cp.wait_send()
        mine.wait()

    return pl.pallas_call(
        body, name=_name(base), out_shape=SDS((8 * m, n), x.dtype),
        in_specs=[pl.BlockSpec(memory_space=space)], out_specs=pl.BlockSpec(memory_space=space),
        scratch_shapes=[pltpu.SemaphoreType.DMA((7,)), pltpu.SemaphoreType.DMA((7,)), pltpu.SemaphoreType.DMA],
    )(x)


_ANY = pl.BlockSpec(memory_space=pl.ANY)


def _half(ref, core, lead):
    k_half = ref.shape[lead] // 2
    return ref.at[(slice(None),) * lead + (pl.ds(core * k_half, k_half), slice(None))]


def _gather_weights(shards, base):
    nw = len(shards)

    def body(*refs):
        x_refs, out_refs = refs[:nw], refs[nw:2 * nw]
        send_sems, recv_sems, local_sems = refs[2 * nw:]
        px, py, pc = _position()
        me, sibling = (px, py, pc), (px, py, 1 - pc)
        chips = [(1 - px, py), (px, 1 - py), (1 - px, 1 - py)]

        def block(i, bx, by, bc):
            return _half(out_refs[i].at[2 * bx + by], bc, 1)

        def copy(i, k, blk, to, src=None):
            dst = block(i, *blk)
            return pltpu.make_async_remote_copy(
                src_ref=dst if src is None else src, dst_ref=dst, send_sem=send_sems.at[7 * i + k],
                recv_sem=recv_sems.at[7 * i + k], device_id=to, device_id_type=MESH)

        mine = [_half(x_refs[i], pc, 1) for i in range(nw)]
        local = [pltpu.make_async_copy(mine[i], block(i, *me), local_sems.at[i]) for i in range(nw)]
        for cp in local:
            cp.start()
        first = []
        for i in range(nw):
            first.append(copy(i, 0, me, sibling, src=mine[i]))
            first += [copy(i, 1 + j, me, (*chip, pc), src=mine[i]) for j, chip in enumerate(chips)]
        for cp in first:
            cp.start()
        passed = []
        for j, chip in enumerate(chips):
            for i in range(nw):
                copy(i, 1 + j, (*chip, pc), me).wait_recv()
                passed.append(copy(i, 4 + j, (*chip, pc), sibling))
                passed[-1].start()
        for i in range(nw):
            copy(i, 0, sibling, me).wait_recv()
        for j, chip in enumerate(chips):
            for i in range(nw):
                copy(i, 4 + j, (*chip, 1 - pc), me).wait_recv()
        for cp in first + passed:
            cp.wait_send()
        for cp in local:
            cp.wait()

    return pl.pallas_call(
        body, name=_name(base), out_shape=[SDS((4,) + s.shape, s.dtype) for s in shards],
        in_specs=[_ANY] * nw, out_specs=[_ANY] * nw,
        scratch_shapes=[pltpu.SemaphoreType.DMA((7 * nw,)), pltpu.SemaphoreType.DMA((7 * nw,)),
                        pltpu.SemaphoreType.DMA((nw,))],
    )(*shards)


def _reduce_to_sibling(gs, base):
    nw = len(gs)

    def body(*refs):
        g_refs, recv_refs, send_sems, recv_sems = refs[:nw], refs[nw:2 * nw], refs[2 * nw], refs[2 * nw + 1]
        px, py, pc = _position()
        copies = [pltpu.make_async_remote_copy(
            src_ref=_half(g_refs[i], 1 - pc, 2), dst_ref=recv_refs[i], send_sem=send_sems.at[i],
            recv_sem=recv_sems.at[i], device_id=(px, py, 1 - pc), device_id_type=MESH) for i in range(nw)]
        for cp in copies:
            cp.start()
        for cp in copies:
            cp.wait_recv()
        for cp in copies:
            cp.wait_send()

    outs = [SDS(g.shape[:2] + (g.shape[2] // 2, g.shape[3]), g.dtype) for g in gs]
    return pl.pallas_call(
        body, name=_name(base), out_shape=outs, in_specs=[_ANY] * nw, out_specs=[_ANY] * nw,
        scratch_shapes=[pltpu.SemaphoreType.DMA((nw,)), pltpu.SemaphoreType.DMA((nw,))],
    )(*gs)


def _exchange_chips(ts, base):
    nw = len(ts)

    def body(*refs):
        t_refs, recv_refs, send_sems, recv_sems = refs[:nw], refs[nw:2 * nw], refs[2 * nw], refs[2 * nw + 1]
        px, py, pc = _position()
        chips = [(1 - px, py), (px, 1 - py), (1 - px, 1 - py)]
        copies = [pltpu.make_async_remote_copy(
            src_ref=t_refs[i].at[2 * cx + cy], dst_ref=recv_refs[i].at[k], send_sem=send_sems.at[3 * i + k],
            recv_sem=recv_sems.at[3 * i + k], device_id=(cx, cy, pc), device_id_type=MESH)
            for k, (cx, cy) in enumerate(chips) for i in range(nw)]
        for cp in copies:
            cp.start()
        for cp in copies:
            cp.wait_recv()
        for cp in copies:
            cp.wait_send()

    outs = [SDS((3,) + t.shape[1:], t.dtype) for t in ts]
    return pl.pallas_call(
        body, name=_name(base), out_shape=outs, in_specs=[_ANY] * nw, out_specs=[_ANY] * nw,
        scratch_shapes=[pltpu.SemaphoreType.DMA((3 * nw,)), pltpu.SemaphoreType.DMA((3 * nw,))],
    )(*ts)


def _share_with_sibling(rs, base):
    nw = len(rs)

    def body(*refs):
        in_refs, out_refs, send_sems, recv_sems = refs[:nw], refs[nw:2 * nw], refs[2 * nw], refs[2 * nw + 1]
        px, py, pc = _position()
        send = [pltpu.make_async_remote_copy(
            src_ref=_half(in_refs[i], pc, 1), dst_ref=_half(out_refs[i], pc, 1), send_sem=send_sems.at[i],
            recv_sem=recv_sems.at[i], device_id=(px, py, 1 - pc), device_id_type=MESH) for i in range(nw)]
        for cp in send:
            cp.start()
        for i in range(nw):
            pltpu.make_async_remote_copy(
                src_ref=_half(in_refs[i], 1 - pc, 1), dst_ref=_half(out_refs[i], 1 - pc, 1), send_sem=send_sems.at[i],
                recv_sem=recv_sems.at[i], device_id=(px, py, 1 - pc), device_id_type=MESH).wait_recv()
        for cp in send:
            cp.wait_send()

    return pl.pallas_call(
        body, name=_name(base), out_shape=[SDS(r.shape, r.dtype) for r in rs], in_specs=[_ANY] * nw,
        out_specs=[_ANY] * nw, input_output_aliases={i: i for i in range(nw)},
        scratch_shapes=[pltpu.SemaphoreType.DMA((nw,)), pltpu.SemaphoreType.DMA((nw,))],
    )(*rs)


def _add_own_half(g, recv, core, base, tr=512):
    _, l_dim, k_half, n = recv.shape
    tr = _tile(k_half, tr, 2 * SUBLANES)
    per = k_half // tr

    def body(core_ref, g_ref, r_ref, o_ref):
        o_ref[...] = (g_ref[...] + r_ref[...]).astype(o_ref.dtype)

    blk = (1, 1, tr, n)
    spec = pltpu.PrefetchScalarGridSpec(
        num_scalar_prefetch=1, grid=(4, l_dim, per),
        in_specs=[pl.BlockSpec(blk, lambda j, l, i, core: (j, l, core[0] * per + i, 0)),
                  pl.BlockSpec(blk, lambda j, l, i, core: (j, l, i, 0))],
        out_specs=pl.BlockSpec(blk, lambda j, l, i, core: (j, l, i, 0)))
    return pl.pallas_call(body, name=_name(base), grid_spec=spec, out_shape=SDS(recv.shape, BF16),
                          compiler_params=_params(("parallel", "parallel", "parallel")))(core, g, recv)


def _add_chips(g, from_sibling, recv, chip_core, base, tr=512):
    _, l_dim, k_half, n = from_sibling.shape
    tr = _tile(k_half, tr, 2 * SUBLANES)
    per = k_half // tr

    def body(idx_ref, g_ref, s_ref, r_ref, o_ref):
        own = g_ref[0, 0] + s_ref[0, 0]
        o_ref[0] = ((own + _f(r_ref[0, 0])) + _f(r_ref[1, 0])) + _f(r_ref[2, 0])

    spec = pltpu.PrefetchScalarGridSpec(
        num_scalar_prefetch=1, grid=(l_dim, per),
        in_specs=[pl.BlockSpec((1, 1, tr, n), lambda l, i, idx: (idx[0], l, idx[1] * per + i, 0)),
                  pl.BlockSpec((1, 1, tr, n), lambda l, i, idx: (idx[0], l, i, 0)),
                  pl.BlockSpec((3, 1, tr, n), lambda l, i, idx: (0, l, i, 0))],
        out_specs=pl.BlockSpec((1, tr, n), lambda l, i, idx: (l, idx[1] * per + i, 0)))
    return pl.pallas_call(body, name=_name(base), grid_spec=spec, out_shape=SDS((l_dim, 2 * k_half, n), F32),
                          compiler_params=_params(("parallel", "parallel")))(chip_core, g, from_sibling, recv)


def _sum8(g, base):
    r, n = g.shape[0] // 8, g.shape[1]

    def body(g_ref, o_ref):
        acc = g_ref[pl.ds(0, r), :]
        for d in range(1, 8):
            acc = acc + g_ref[pl.ds(d * r, r), :]
        o_ref[...] = acc

    return pl.pallas_call(body, name=_name(base), out_shape=SDS((r, n), F32), compiler_params=_params())(g)


def _adamw_math(w, g, m, v):
    m = ADAM_B1 * m + (1.0 - ADAM_B1) * g
    v = ADAM_B2 * v + (1.0 - ADAM_B2) * jnp.square(g)
    m_hat = m / (1.0 - ADAM_B1 ** ADAM_STEP)
    v_hat = v / (1.0 - ADAM_B2 ** ADAM_STEP)
    delta = -ADAM_LR * (m_hat / (jnp.sqrt(v_hat) + ADAM_EPS) + ADAM_WD * w)
    return delta, m, v


def _adamw(w, g, m, v, base):
    out = SDS(w.shape, F32)
    return _rowwise(lambda w, g, m, v: (_adamw_math(w, g, m, v), ()), [w, g, m, v], [], [out, out, out], [], base)


def _mod_grad_adamw(sc_t, dmod, w, m, v, base, tr=256):
    l_dim, d_dim, n = w.shape
    tr = _tile(d_dim, tr, SUBLANES)

    def body(sc_ref, dm_ref, w_ref, m_ref, v_ref, g_ref, d_ref, nm_ref, nv_ref):
        g = lax.dot_general(sc_ref[...], dm_ref[0], _DIMS["nn"], preferred_element_type=F32,
                            precision=lax.Precision.HIGHEST)
        delta, nm, nv = _adamw_math(w_ref[0], g, m_ref[0], v_ref[0])
        g_ref[0], d_ref[0], nm_ref[0], nv_ref[0] = g, delta, nm, nv

    tile = pl.BlockSpec((1, tr, n), lambda l, i: (l, i, 0))
    out = SDS(w.shape, F32)
    return pl.pallas_call(
        body, name=_name(base), grid=(l_dim, d_dim // tr),
        in_specs=[pl.BlockSpec((tr, 8), lambda l, i: (i, 0)), pl.BlockSpec((1, 8, n), lambda l, i: (l, 0, 0)),
                  tile, tile, tile],
        out_specs=[tile] * 4, out_shape=[out] * 4, compiler_params=_params(("parallel", "parallel")),
    )(sc_t, dmod, w, m, v)


def _mod_forward(c_all, w_mod, b_mod, base):
    l_dim, d_dim, n = w_mod.shape

    def body(c_ref, w_ref, b_ref, o_ref):
        o_ref[0] = lax.dot_general(jax.nn.silu(c_ref[...]), w_ref[0], _DIMS["nn"], preferred_element_type=F32,
                                   precision=lax.Precision.HIGHEST) + b_ref[0]

    return pl.pallas_call(
        body, name=_name(base), grid=(l_dim,),
        in_specs=[pl.BlockSpec((8, d_dim), lambda l: (0, 0)), pl.BlockSpec((1, d_dim, n), lambda l: (l, 0, 0)),
                  pl.BlockSpec((1, 1, n), lambda l: (l, 0, 0))],
        out_specs=pl.BlockSpec((1, 8, n), lambda l: (l, 0, 0)), out_shape=SDS((l_dim, 8, n), F32),
        compiler_params=_params(("parallel",)),
    )(c_all, w_mod, b_mod)


def _pack(arrays, dtype, row_mult):
    flat = jnp.concatenate([a.reshape(-1).astype(dtype) for a in arrays])
    quantum = PACK_LANES * row_mult
    pad = (-flat.shape[0]) % quantum
    return jnp.pad(flat, (0, pad)).reshape(-1, PACK_LANES)


def _unpack(flat, shapes):
    out, off = [], 0
    for shp in shapes:
        n = math.prod(shp)
        out.append(flat[off:off + n].reshape(shp))
        off += n
    return out


_WEIGHTS = ["norm_g", "w_mod", "b_mod", "sb_w_qkv", "sb_w_o", "s5_lam_re", "s5_lam_im", "s5_log_dt", "s5_b_re",
            "s5_b_im", "s5_c_re", "s5_c_im", "s5_d", "s5_w_glu", "s5_b_glu", "cv_w_pw1", "cv_b_pw1", "cv_w_dw",
            "cv_b_dw", "cv_ln_g", "cv_ln_b", "cv_w_pw2", "cv_b_pw2", "ffn_w_gate", "ffn_w_up", "ffn_w_down"]
_BIG_COL = ["sb_w_qkv", "s5_w_glu", "cv_w_pw1", "ffn_w_gate", "ffn_w_up"]
_BIG_ROW = ["sb_w_o", "cv_w_pw2", "ffn_w_down"]
_BIG = _BIG_COL + _BIG_ROW
_SMALL_SHARDED = {"norm_g": 2, "cv_b_pw1": 1, "cv_w_dw": 2, "cv_b_dw": 1, "cv_ln_g": 1, "cv_ln_b": 1, "cv_b_pw2": 1}
_REPLICATED = ["b_mod", "s5_lam_re", "s5_lam_im", "s5_log_dt", "s5_b_re", "s5_b_im", "s5_c_re", "s5_c_im", "s5_d",
               "s5_b_glu"]


def _full_from_shards(stacked, name):
    if name in _BIG_COL or (name in _SMALL_SHARDED):
        axis = stacked.ndim - 1
    else:
        axis = stacked.ndim - 2
    moved = jnp.moveaxis(stacked, 0, axis - 1)
    shp = list(moved.shape)
    shp[axis - 1:axis + 1] = [shp[axis - 1] * shp[axis]]
    return moved.reshape(shp)


def _shards_from_full(full, name):
    axis = full.ndim - 1 if (name in _BIG_COL or name in _SMALL_SHARDED) else full.ndim - 2
    shp = list(full.shape)
    shp[axis:axis + 1] = [4, shp[axis] // 4]
    return jnp.moveaxis(full.reshape(shp), axis, 0)


def kernel(x, c, norm_g, w_mod, b_mod, sb_w_qkv, sb_w_o, s5_lam_re, s5_lam_im, s5_log_dt, s5_b_re, s5_b_im, s5_c_re, s5_c_im, s5_d, s5_w_glu, s5_b_glu, cv_w_pw1, cv_b_pw1, cv_w_dw, cv_b_dw, cv_ln_g, cv_ln_b, cv_w_pw2, cv_b_pw2, ffn_w_gate, ffn_w_up, ffn_w_down, loss_target, m_norm_g, m_w_mod, m_b_mod, m_sb_w_qkv, m_sb_w_o, m_s5_lam_re, m_s5_lam_im, m_s5_log_dt, m_s5_b_re, m_s5_b_im, m_s5_c_re, m_s5_c_im, m_s5_d, m_s5_w_glu, m_s5_b_glu, m_cv_w_pw1, m_cv_b_pw1, m_cv_w_dw, m_cv_b_dw, m_cv_ln_g, m_cv_ln_b, m_cv_w_pw2, m_cv_b_pw2, m_ffn_w_gate, m_ffn_w_up, m_ffn_w_down, v_norm_g, v_w_mod, v_b_mod, v_sb_w_qkv, v_sb_w_o, v_s5_lam_re, v_s5_lam_im, v_s5_log_dt, v_s5_b_re, v_s5_b_im, v_s5_c_re, v_s5_c_im, v_s5_d, v_s5_w_glu, v_s5_b_glu, v_cv_w_pw1, v_cv_b_pw1, v_cv_w_dw, v_cv_b_dw, v_cv_ln_g, v_cv_ln_b, v_cv_w_pw2, v_cv_b_pw2, v_ffn_w_gate, v_ffn_w_up, v_ffn_w_down):
    args = locals()
    wts = {n: args[n] for n in _WEIGHTS}
    mom = {n: args["m_" + n] for n in _WEIGHTS}
    var = {n: args["v_" + n] for n in _WEIGHTS}
    px, py, pc = lax.axis_index("x"), lax.axis_index("y"), lax.axis_index("c")
    chip = 2 * px + py
    dev = 2 * chip + pc
    s_dim, d_dim = x.shape[1], x.shape[2]
    depth = norm_g.shape[0]
    x2, tgt = x[0], loss_target[0]

    small_names = list(_SMALL_SHARDED)
    small_in = _pack([c] + [wts[n] for n in small_names], F32, SUBLANES)
    small_all = _all_gather8(small_in, "gather_small").reshape(8, -1)
    c_all = small_all[:, :d_dim]
    full = {}
    off = d_dim
    for n in small_names:
        size = math.prod(wts[n].shape)
        stacked = small_all[0::2, off:off + size].reshape((4,) + wts[n].shape)
        full[n] = _full_from_shards(stacked, n)
        off += size

    n_mod = w_mod.shape[2]
    b_mod_cols = lax.dynamic_slice_in_dim(b_mod, chip * n_mod, n_mod, axis=1)[:, None, :]
    mod_part = _mod_forward(c_all, w_mod, b_mod_cols, "mod_fwd")
    mod_all = _all_gather8(mod_part.reshape(depth * 8, n_mod), "gather_mod")
    mod_all = mod_all.reshape(4, 2, depth, 8, n_mod)[:, 0]
    mod_mine = lax.dynamic_index_in_dim(mod_all, dev, axis=2, keepdims=False)
    mod = mod_mine.transpose(1, 0, 2).reshape(depth, 4 * n_mod)

    stacked = _gather_weights([wts[n].astype(BF16) for n in _BIG], "gather_big")
    for n, st in zip(_BIG, stacked, strict=True):
        full[n] = _full_from_shards(st, n)
    full["ffn_w_gu"] = jnp.concatenate([full.pop("ffn_w_gate"), full.pop("ffn_w_up")], axis=2)
    for n in _REPLICATED:
        full[n] = wts[n]

    n_s5 = s5_lam_re.shape[0]
    gp = s5_lam_re.shape[1] * s5_lam_re.shape[2]
    col = lambda a: a.reshape(gp, 1)
    s5_raw_in = []
    full["s5"] = []
    for j in range(n_s5):
        ldt = jnp.broadcast_to(s5_log_dt[j][:, None], s5_lam_re[j].shape)
        raw = (col(s5_lam_re[j]), col(s5_lam_im[j]), col(ldt), s5_b_re[j].reshape(gp, S5_GROUP), s5_b_im[j].reshape(gp, S5_GROUP))
        ar, ai, bbr, bbi = _s5_prep(*raw, "s5_prep")
        s5_raw_in.append(raw)
        shape3 = s5_b_re[j].shape
        full["s5"].append(dict(
            ar=ar.reshape(1, gp), ai=ai.reshape(1, gp),
            wr=_block_diag(bbr.reshape(shape3), False).astype(BF16), wi=_block_diag(bbi.reshape(shape3), False).astype(BF16),
            cr=_block_diag(s5_c_re[j], True).astype(BF16), ci=_block_diag(s5_c_im[j], True).astype(BF16)))

    loss_cols, grad_x, dmod, grads = _local_step(x2, mod, full, tgt)
    loss = lax.psum(0.5 * jnp.sum(loss_cols) / d_dim, ("x", "y", "c"))

    small_grads = {}
    for name in ("s5_lam_re", "s5_lam_im", "s5_log_dt", "s5_b_re", "s5_b_im", "s5_c_re", "s5_c_im"):
        small_grads[name] = []
    for j in range(n_s5):
        dwr, dwi, dcr, dci, dar, dai = grads["s5_raw"][j]
        shape3 = s5_b_re[j].shape
        cots = (jnp.sum(dar, axis=0).reshape(gp, 1), jnp.sum(dai, axis=0).reshape(gp, 1),
                _block_diag_extract(dwr, False).reshape(gp, S5_GROUP), _block_diag_extract(dwi, False).reshape(gp, S5_GROUP))
        dlr, dli, dldt, dbr, dbi = _s5_prep_bwd(*s5_raw_in[j], cots, "s5_prep_bwd")
        small_grads["s5_lam_re"].append(dlr.reshape(s5_lam_re[j].shape))
        small_grads["s5_lam_im"].append(dli.reshape(s5_lam_re[j].shape))
        small_grads["s5_log_dt"].append(jnp.sum(dldt.reshape(s5_lam_re[j].shape), axis=1))
        small_grads["s5_b_re"].append(dbr.reshape(shape3))
        small_grads["s5_b_im"].append(dbi.reshape(shape3))
        small_grads["s5_c_re"].append(_block_diag_extract(dcr, True))
        small_grads["s5_c_im"].append(_block_diag_extract(dci, True))
    small_grads = {n: jnp.stack(v) for n, v in small_grads.items()}
    stack = lambda name, count: jnp.stack([grads[name][j] for j in range(count)])
    small_grads["s5_d"] = stack("s5_d", n_s5)[:, 0]
    small_grads["s5_b_glu"] = stack("s5_b_glu", n_s5)[:, 0]
    n_cv = cv_w_pw1.shape[0]
    for name in ("cv_b_pw1", "cv_b_dw", "cv_ln_g", "cv_ln_b", "cv_b_pw2"):
        small_grads[name] = stack(name, n_cv)[:, 0]
    small_grads["cv_w_dw"] = stack("cv_w_dw", n_cv)
    small_grads["norm_g"] = grads["norm_g"]

    small_order = ["norm_g", "s5_lam_re", "s5_lam_im", "s5_log_dt", "s5_b_re", "s5_b_im", "s5_c_re", "s5_c_im", "s5_d",
                   "s5_b_glu", "cv_b_pw1", "cv_w_dw", "cv_b_dw", "cv_ln_g", "cv_ln_b", "cv_b_pw2"]
    small_shapes = [small_grads[n].shape for n in small_order]
    sg_in = _pack([dmod] + [small_grads[n] for n in small_order], F32, SUBLANES)
    sg_all = _all_gather8(sg_in, "gather_small_grads")
    sg_sum = _sum8(sg_all, "sum_small_grads").reshape(-1)
    dmod_all = sg_all.reshape(8, -1)[:, :dmod.size].reshape(8, depth, 4, n_mod)
    g_full = dict(zip(small_order, _unpack(sg_sum[dmod.size:], small_shapes), strict=True))
    g_full["b_mod"] = sg_sum[:dmod.size].reshape(dmod.shape)

    gu = stack("ffn_w_gu", depth)
    f_dim = gu.shape[2] // 2
    big_grads = {"sb_w_qkv": stack("sb_w_qkv", sb_w_qkv.shape[0]), "sb_w_o": stack("sb_w_o", sb_w_o.shape[0]),
                 "s5_w_glu": stack("s5_w_glu", n_s5), "cv_w_pw1": stack("cv_w_pw1", n_cv), "cv_w_pw2": stack("cv_w_pw2", n_cv),
                 "ffn_w_gate": gu[:, :, :f_dim], "ffn_w_up": gu[:, :, f_dim:], "ffn_w_down": stack("ffn_w_down", depth)}
    chip_major = [_shards_from_full(big_grads[n], n) for n in _BIG]
    core_idx, chip_core = pc.reshape(1).astype(I32), jnp.stack([chip, pc]).astype(I32)
    from_sibling = _reduce_to_sibling(chip_major, "reduce_sibling")
    chip_sums = [_add_own_half(g, r, core_idx, "reduce_add_sibling") for g, r in zip(chip_major, from_sibling, strict=True)]
    from_chips = _exchange_chips(chip_sums, "reduce_chips")
    my_half = [_add_chips(g, s, r, chip_core, "reduce_add_chips")
               for g, s, r in zip(chip_major, from_sibling, from_chips, strict=True)]
    g_shard = dict(zip(_BIG, _share_with_sibling(my_half, "reduce_share"), strict=True))

    out_g, out_d, out_m, out_v = {}, {}, {}, {}
    for n in _BIG:
        shp = wts[n].shape
        two = lambda a: a.reshape(-1, shp[-1])
        out_g[n] = g_shard[n]
        d, nm, nv = _adamw(two(wts[n]), two(g_shard[n]), two(mom[n]), two(var[n]), "adamw")
        out_d[n], out_m[n], out_v[n] = d.reshape(shp), nm.reshape(shp), nv.reshape(shp)
    sc_t = jax.nn.silu(c_all).T
    dmod_cols = lax.dynamic_index_in_dim(dmod_all, chip, axis=2, keepdims=False).transpose(1, 0, 2)
    out_g["w_mod"], out_d["w_mod"], out_m["w_mod"], out_v["w_mod"] = _mod_grad_adamw(
        sc_t, dmod_cols, w_mod, m_w_mod, v_w_mod, "w_mod_adamw")
    small_all_names = [n for n in _WEIGHTS if n not in _BIG and n != "w_mod"]
    local_g = []
    for n in small_all_names:
        g = g_full[n]
        if n in _SMALL_SHARDED:
            width = wts[n].shape[-1]
            g = lax.dynamic_slice_in_dim(g, chip * width, width, axis=g.ndim - 1)
        local_g.append(g)
        out_g[n] = g
    pk = lambda arrs: _pack(arrs, F32, SUBLANES)
    d, nm, nv = _adamw(pk([wts[n] for n in small_all_names]), pk(local_g), pk([mom[n] for n in small_all_names]),
                       pk([var[n] for n in small_all_names]), "adamw_small")
    shapes = [wts[n].shape for n in small_all_names]
    for n, a, b, cc in zip(small_all_names, _unpack(d.reshape(-1), shapes), _unpack(nm.reshape(-1), shapes),
                           _unpack(nv.reshape(-1), shapes), strict=True):
        out_d[n], out_m[n], out_v[n] = a, b, cc
    return (loss, grad_x[None], *[out_g[n] for n in _WEIGHTS], *[out_d[n] for n in _WEIGHTS],
            *[out_m[n] for n in _WEIGHTS], *[out_v[n] for n in _WEIGHTS])
```

```python
import functools
import math

import jax
import jax.numpy as jnp
from jax import lax
from jax.experimental import pallas as pl
from jax.experimental.pallas import tpu as pltpu

F32 = jnp.float32
BF16 = jnp.bfloat16
I32 = jnp.int32
SDS = jax.ShapeDtypeStruct
MESH = pl.DeviceIdType.MESH

HEAD_DIM = 64
HEAD_PAIR = 2 * HEAD_DIM
S5_GROUP = 16
S5_STATE = 64
S5_BLOCK_GROUPS = 8
S5_BLOCK_CH = S5_GROUP * S5_BLOCK_GROUPS
S5_BLOCK_ST = S5_STATE * S5_BLOCK_GROUPS
CONV_WIDTH = 31
CONV_HALO = 32
EPS = 1e-6
ADAM_LR = 0.001
ADAM_B1 = 0.9
ADAM_B2 = 0.999
ADAM_EPS = 1e-08
ADAM_WD = 0.01
ADAM_STEP = 10

LANES = 128
SUBLANES = 8
VMEM_LIMIT = 56 * 2 ** 20
PACK_LANES = 512
ATT_BLOCK = 128
EXP_FLOOR = -104.0
ATT_PAIRS_FWD = 4
ATT_PAIRS_BWD = 2
ATT_KEEP = 4
_KEPT = ("log_keep", "log_beta", "weight", "d_exponent")

_COUNTER = [0]


def _name(base):
    _COUNTER[0] += 1
    return f"{base}_{_COUNTER[0]}"


def _tile(dim, target, mult=LANES):
    t = min(dim, target)
    t -= t % mult
    while t > mult and dim % t:
        t -= mult
    return t if t > 0 and dim % t == 0 else dim


def _params(sem=None):
    return pltpu.CompilerParams(dimension_semantics=sem, vmem_limit_bytes=VMEM_LIMIT)


_DIMS = {"nn": (((1,), (0,)), ((), ())), "nt": (((1,), (1,)), ((), ())), "tn": (((0,), (0,)), ((), ()))}


def _matmul(a, b, form, out_dtype, base, tm=1536, tn=1536, tk=1536):
    if form == "tn":
        k_dim, m_dim = a.shape
    else:
        m_dim, k_dim = a.shape
    n_dim = b.shape[0] if form == "nt" else b.shape[1]
    tm, tn, tk = _tile(m_dim, tm), _tile(n_dim, tn), _tile(k_dim, tk)
    nk = k_dim // tk
    dims = _DIMS[form]

    def body(a_ref, b_ref, o_ref, *acc):
        part = lax.dot_general(a_ref[...].astype(BF16), b_ref[...].astype(BF16), dims, preferred_element_type=F32)
        if nk == 1:
            o_ref[...] = part.astype(o_ref.dtype)
            return
        acc_ref, = acc
        k = pl.program_id(2)

        @pl.when(k == 0)
        def _():
            acc_ref[...] = part

        @pl.when(jnp.logical_and(k > 0, k < nk - 1))
        def _():
            acc_ref[...] += part

        @pl.when(k == nk - 1)
        def _():
            o_ref[...] = (acc_ref[...] + part).astype(o_ref.dtype)

    a_spec = (pl.BlockSpec((tk, tm), lambda i, j, k: (k, i)) if form == "tn"
              else pl.BlockSpec((tm, tk), lambda i, j, k: (i, k)))
    b_spec = (pl.BlockSpec((tn, tk), lambda i, j, k: (j, k)) if form == "nt"
              else pl.BlockSpec((tk, tn), lambda i, j, k: (k, j)))
    return pl.pallas_call(
        body, name=_name(base), grid=(m_dim // tm, n_dim // tn, nk),
        in_specs=[a_spec, b_spec], out_specs=pl.BlockSpec((tm, tn), lambda i, j, k: (i, j)),
        out_shape=SDS((m_dim, n_dim), out_dtype), scratch_shapes=[pltpu.VMEM((tm, tn), F32)] if nk > 1 else [],
        compiler_params=_params(("parallel", "parallel", "arbitrary")),
    )(a, b)


def _ffn_up(a, w_gu, base, tm=512, tn=1536):
    s_dim, d_dim = a.shape
    f_dim = w_gu.shape[1] // 2
    tm, tn = _tile(s_dim, tm), _tile(f_dim, tn)
    up_off = f_dim // tn

    def body(a_ref, g_w, u_w, gate_ref, up_ref, act_ref):
        gate = _dot(a_ref[...], g_w[...]).astype(BF16)
        up = _dot(a_ref[...], u_w[...]).astype(BF16)
        gate_ref[...] = gate
        up_ref[...] = up
        act_ref[...] = _swish2(_f(gate), _f(up)).astype(BF16)

    out = SDS((s_dim, f_dim), BF16)
    tile = pl.BlockSpec((tm, tn), lambda i, j: (i, j))
    return pl.pallas_call(
        body, name=_name(base), grid=(s_dim // tm, f_dim // tn),
        in_specs=[pl.BlockSpec((tm, d_dim), lambda i, j: (i, 0)), pl.BlockSpec((d_dim, tn), lambda i, j: (0, j)),
                  pl.BlockSpec((d_dim, tn), lambda i, j: (0, up_off + j))],
        out_specs=[tile, tile, tile], out_shape=[out, out, out], compiler_params=_params(("parallel", "parallel")),
    )(a, w_gu, w_gu)


def _rowwise(fn, rows, vecs, out_rows, out_vecs, base, tr=256):
    s_dim = rows[0].shape[0]
    tr = _tile(s_dim, tr, SUBLANES)
    nr, nv, nor = len(rows), len(vecs), len(out_rows)

    def body(*refs):
        r_in, v_in = refs[:nr], refs[nr:nr + nv]
        r_out, v_out = refs[nr + nv:nr + nv + nor], refs[nr + nv + nor:]
        ro, vo = fn(*[r[...] for r in r_in], *[v[...] for v in v_in])
        for ref, val in zip(r_out, ro, strict=True):
            ref[...] = val.astype(ref.dtype)
        if v_out:
            i = pl.program_id(0)

            @pl.when(i == 0)
            def _():
                for ref, val in zip(v_out, vo, strict=True):
                    ref[...] = val.astype(ref.dtype)

            @pl.when(i > 0)
            def _():
                for ref, val in zip(v_out, vo, strict=True):
                    ref[...] += val.astype(ref.dtype)

    in_specs = [pl.BlockSpec((tr, r.shape[1]), lambda i: (i, 0)) for r in rows]
    in_specs += [pl.BlockSpec(v.shape, lambda i: (0, 0)) for v in vecs]
    out_specs = [pl.BlockSpec((tr, o.shape[1]), lambda i: (i, 0)) for o in out_rows]
    out_specs += [pl.BlockSpec(o.shape, lambda i: (0, 0)) for o in out_vecs]
    outs = pl.pallas_call(
        body, name=_name(base), grid=(s_dim // tr,), in_specs=in_specs, out_specs=out_specs,
        out_shape=list(out_rows) + list(out_vecs),
        compiler_params=_params(("arbitrary",)),
    )(*rows, *vecs)
    return outs


def _f(x):
    return x.astype(F32)


def _rms(x, g):
    return x * lax.rsqrt(jnp.mean(x * x, axis=-1, keepdims=True) + EPS) * g


def _pre(h, g, sh, sc):
    return _rms(h, g) * (1 + sc) + sh


def _post(h, m, gate, g):
    return h + gate * _rms(m, g)


def _halves(t):
    d = t.shape[1] // 2
    return t[:, :d], t[:, d:]


def _glu2(ta, tb, ba, bb):
    return (ta + ba) * jax.nn.sigmoid(tb + bb)


def _glu(t, b):
    return _glu2(*_halves(t), *_halves(b))


def _glu_bwd(t, dout, b):
    _, vjp = jax.vjp(_glu2, *_halves(t), *_halves(b))
    dta, dtb, dba, dbb = vjp(dout)
    return jnp.concatenate([dta, dtb], axis=1), jnp.concatenate([dba, dbb], axis=1)


def _swish2(g, u):
    return jax.nn.silu(g) * u


def _swish_gate(gu):
    return _swish2(*_halves(gu))


def _ln_silu(h, g, b):
    mu = jnp.mean(h, axis=-1, keepdims=True)
    var = jnp.mean(jnp.square(h - mu), axis=-1, keepdims=True)
    return jax.nn.silu((h - mu) * lax.rsqrt(var + EPS) * g + b)


def _dot(a, b, form="nn"):
    return lax.dot_general(a, b, _DIMS[form], preferred_element_type=F32)


def _split_dot(x, tri):
    hi = x.astype(BF16)
    lo = (x - hi.astype(F32)).astype(BF16)
    return _dot(hi, tri) + _dot(lo, tri)


def _sb_scores(qb, kb, scale, mask):
    z = _dot(qb, kb, "nt") * scale
    sp = jnp.maximum(z, 0.0) + jnp.log(1.0 + jnp.exp(-jnp.abs(z)))
    lk = -sp
    if mask is not None:
        lk = jnp.where(mask, lk, 0.0)
    return lk, z - sp


def _walk_left(i, state, step):
    def cond(ns):
        live = functools.reduce(jnp.maximum, [jnp.max(c) for c in ns[1][0]])
        return jnp.logical_and(ns[0] < i, live > EXP_FLOOR)

    def body(ns):
        return ns[0] + 1, step(i - 1 - ns[0], ns[1])

    return lax.while_loop(cond, body, (jnp.int32(0), state))[1]


def _pair_split(x):
    first = lax.broadcasted_iota(I32, x.shape, 1) < HEAD_DIM
    zero = jnp.zeros_like(x)
    return (jnp.where(first, x, zero), jnp.where(first, zero, x)), first


def _group_specs(s_dim, d_dim, bq, pairs):
    while d_dim % (pairs * HEAD_PAIR):
        pairs //= 2
    width = pairs * HEAD_PAIR
    ngroup = d_dim // width
    blk = pl.BlockSpec((bq, width), lambda p, i: (i, p))
    k_all = pl.BlockSpec((s_dim, width), lambda p, i: (0, ngroup + p))
    v_all = pl.BlockSpec((s_dim, width), lambda p, i: (0, 2 * ngroup + p))
    return pairs, ngroup, blk, k_all, v_all


def _pair_lanes(g):
    return slice(g * HEAD_PAIR, (g + 1) * HEAD_PAIR)


def _sb_fwd(qkv, base):
    s_dim, d_dim = qkv.shape[0], qkv.shape[1] // 3
    bq = ATT_BLOCK
    scale = HEAD_DIM ** -0.5
    pairs, ngroup, blk, k_all, v_all = _group_specs(s_dim, d_dim, bq, ATT_PAIRS_FWD)

    def body(q_ref, k_ref, v_ref, o_ref):
        i = pl.program_id(1)
        split = [_pair_split(q_ref[:, _pair_lanes(g)]) for g in range(pairs)]
        qs = [q for pair, _ in split for q in pair]
        first = split[0][1]
        row = lax.broadcasted_iota(I32, (bq, bq), 0)
        col = lax.broadcasted_iota(I32, (bq, bq), 1)
        later = (row > col).astype(BF16)
        causal = col < row

        def tile(j, cs, accs, mask):
            ks = pl.ds(pl.multiple_of(j * bq, bq), bq)
            new_c, new_acc = [], []
            for h, (qh, c, acc) in enumerate(zip(qs, cs, accs, strict=True)):
                lanes = _pair_lanes(h // 2)
                lk, lb = _sb_scores(qh, k_ref[ks, lanes], scale, mask)
                w = jnp.exp(lb + _split_dot(lk, later) + c)
                if mask is not None:
                    w = jnp.where(mask, w, 0.0)
                new_acc.append(acc + _dot(w.astype(BF16), v_ref[ks, lanes]))
                new_c.append(c + jnp.sum(lk, axis=1, keepdims=True))
            return tuple(new_c), tuple(new_acc)

        zero_c, zero_acc = jnp.zeros((bq, 1), F32), jnp.zeros((bq, HEAD_PAIR), F32)
        cs, accs = tile(i, (zero_c,) * len(qs), (zero_acc,) * len(qs), causal)
        cs, accs = _walk_left(i, (cs, accs), lambda j, s: tile(j, s[0], s[1], None))
        for g in range(pairs):
            o_ref[:, _pair_lanes(g)] = jnp.where(first, accs[2 * g], accs[2 * g + 1]).astype(o_ref.dtype)

    return pl.pallas_call(
        body, name=_name(base), grid=(ngroup, s_dim // bq), in_specs=[blk, k_all, v_all], out_specs=blk,
        out_shape=SDS((s_dim, d_dim), BF16), compiler_params=_params(("parallel", "arbitrary")),
    )(qkv, qkv, qkv)


def _sb_bwd(qkv, do, base):
    s_dim, d_dim = do.shape
    bq = ATT_BLOCK
    scale = HEAD_DIM ** -0.5
    pairs, ngroup, blk, k_all, v_all = _group_specs(s_dim, d_dim, bq, ATT_PAIRS_BWD)
    width = pairs * HEAD_PAIR
    nq = s_dim // bq

    def body(q_ref, k_ref, v_ref, do_ref, dq_ref, dk_hbm, dv_hbm, dk_ref, dv_ref, kept, out_sems):
        p, i = pl.program_id(0), pl.program_id(1)
        q_split = [_pair_split(q_ref[:, _pair_lanes(g)]) for g in range(pairs)]
        qs = [q for pair, _ in q_split for q in pair]
        dos = [d for g in range(pairs) for d in _pair_split(do_ref[:, _pair_lanes(g)])[0]]
        first = q_split[0][1]
        heads = tuple(range(2 * pairs))
        row = lax.broadcasted_iota(I32, (bq, bq), 0)
        col = lax.broadcasted_iota(I32, (bq, bq), 1)
        later = (row > col).astype(BF16)
        from_here = (row >= col).astype(BF16)
        causal = col < row

        @pl.when(i == 0)
        def _():
            dk_ref[...] = jnp.zeros_like(dk_ref)
            dv_ref[...] = jnp.zeros_like(dv_ref)

        def weights(h, kb, vb, c, mask):
            lk, lb = _sb_scores(qs[h], kb, scale, mask)
            w = jnp.exp(lb + _split_dot(lk, later) + c)
            if mask is not None:
                w = jnp.where(mask, w, 0.0)
            return lk, lb, w, _dot(dos[h], vb, "nt") * w

        def slot(h, n):
            return (h * ATT_KEEP + n) * len(_KEPT)

        def total(j, cs, tots, mask):
            ks = pl.ds(pl.multiple_of(j * bq, bq), bq)
            new_c, new_tot = [], []
            for h in heads:
                lanes = _pair_lanes(h // 2)
                tile = weights(h, k_ref[ks, lanes], v_ref[ks, lanes], cs[h], mask)

                @pl.when(i - j < ATT_KEEP)
                def _(tile=tile, h=h):
                    for a, val in enumerate(tile):
                        kept[slot(h, i - j) + a] = val

                new_c.append(cs[h] + jnp.sum(tile[0], axis=1, keepdims=True))
                new_tot.append(tots[h] + jnp.sum(tile[3], axis=1, keepdims=True))
            return tuple(new_c), tuple(new_tot)

        zero = jnp.zeros((bq, 1), F32)
        zeros = (zero,) * len(heads)
        cs, tots = total(i, zeros, zeros, causal)
        cs, tots = _walk_left(i, (cs, tots), lambda j, s: total(j, s[0], s[1], None))

        def grads(j, cs, rs, dqs, mask):
            ks = pl.ds(pl.multiple_of(j * bq, bq), bq)
            new_c, new_r, new_dq = [], [], []
            dk_t, dv_t = [None] * pairs, [None] * pairs
            for h in heads:
                g, lanes = h // 2, _pair_lanes(h // 2)
                kb = k_ref[ks, lanes]
                lk, lb, w, de = lax.cond(
                    i - j < ATT_KEEP,
                    lambda h=h: tuple(kept[slot(h, i - j) + a] for a in range(len(_KEPT))),
                    lambda h=h, kb=kb, lanes=lanes: weights(h, kb, v_ref[ks, lanes], cs[h], mask))
                before = tots[h] - (rs[h] + _split_dot(de, from_here))
                dz = de * jnp.exp(lk) - jnp.exp(lb) * before
                if mask is not None:
                    dz = jnp.where(mask, dz, 0.0)
                dzb = (dz * scale).astype(BF16)
                dk_h, dv_h = _dot(dzb, qs[h], "tn"), _dot(w.astype(BF16), dos[h], "tn")
                dk_t[g], dv_t[g] = (dk_h, dv_h) if dk_t[g] is None else (dk_t[g] + dk_h, dv_t[g] + dv_h)
                new_dq.append(dqs[h] + _dot(dzb, kb))
                new_c.append(cs[h] + jnp.sum(lk, axis=1, keepdims=True))
                new_r.append(rs[h] + jnp.sum(de, axis=1, keepdims=True))
            for g in range(pairs):
                dk_ref[ks, _pair_lanes(g)] += dk_t[g]
                dv_ref[ks, _pair_lanes(g)] += dv_t[g]
            return tuple(new_c), tuple(new_r), tuple(new_dq)

        zero_dq = jnp.zeros((bq, HEAD_PAIR), F32)
        state = grads(i, zeros, zeros, (zero_dq,) * len(heads), causal)
        _, _, dqs = _walk_left(i, state, lambda j, s: grads(j, s[0], s[1], s[2], None))
        for g in range(pairs):
            dq_ref[:, _pair_lanes(g)] = jnp.where(first, dqs[2 * g], dqs[2 * g + 1])

        @pl.when(i == nq - 1)
        def _():
            cols = pl.ds(pl.multiple_of(p * width, width), width)
            out = [pltpu.make_async_copy(dk_ref, dk_hbm.at[:, cols], out_sems.at[0]),
                   pltpu.make_async_copy(dv_ref, dv_hbm.at[:, cols], out_sems.at[1])]
            for cp in out:
                cp.start()
            for cp in out:
                cp.wait()

    out = SDS((s_dim, d_dim), F32)
    acc = pltpu.VMEM((s_dim, width), F32)
    return pl.pallas_call(
        body, name=_name(base), grid=(ngroup, nq), in_specs=[blk, k_all, v_all, blk],
        out_specs=[blk, _ANY, _ANY], out_shape=[out, out, out],
        scratch_shapes=[acc, acc, pltpu.VMEM((2 * pairs * ATT_KEEP * len(_KEPT), bq, bq), F32),
                        pltpu.SemaphoreType.DMA((2,))],
        compiler_params=_params(("parallel", "arbitrary")),
    )(qkv, qkv, qkv, do)


def _cmul(ar, ai, br, bi):
    return ar * br - ai * bi, ar * bi + ai * br


def _s5_prep_fn(lr, li, ldt, br, bi):
    dt = jnp.exp(ldt)
    mag = jnp.exp(lr * dt)
    ar, ai = mag * jnp.cos(li * dt), mag * jnp.sin(li * dt)
    den = lr * lr + li * li
    er = ((ar - 1) * lr + ai * li) / den
    ei = (ai * lr - (ar - 1) * li) / den
    return ar, ai, er * br - ei * bi, er * bi + ei * br


def _s5_prep(lr, li, ldt, br, bi, base):
    col, mat = SDS(lr.shape, F32), SDS(br.shape, F32)
    return _rowwise(lambda *a: (_s5_prep_fn(*a), ()), [lr, li, ldt, br, bi], [], [col, col, mat, mat], [], base, tr=512)


def _s5_prep_bwd(lr, li, ldt, br, bi, cots, base):
    col, mat = SDS(lr.shape, F32), SDS(br.shape, F32)

    def bwd(lr, li, ldt, br, bi, c0, c1, c2, c3):
        _, vjp = jax.vjp(_s5_prep_fn, lr, li, ldt, br, bi)
        return vjp((c0, c1, c2, c3)), ()

    return _rowwise(bwd, [lr, li, ldt, br, bi, *cots], [], [col, col, col, mat, mat], [], base, tr=512)


def _scan_tables(ar_ref, ai_ref, sl, conj):
    shape = (SUBLANES, S5_BLOCK_ST)
    a_r = jnp.broadcast_to(ar_ref[:, sl], shape)
    a_i = jnp.broadcast_to(ai_ref[:, sl], shape)
    if conj:
        a_i = -a_i
    a2 = _cmul(a_r, a_i, a_r, a_i)
    a4 = _cmul(*a2, *a2)
    row = lax.broadcasted_iota(I32, shape, 0)
    p_r, p_i, t_r, t_i = a_r, a_i, a_r, a_i
    for k in range(1, SUBLANES):
        p_r, p_i = _cmul(p_r, p_i, a_r, a_i)
        sel = (row == (SUBLANES - 1 - k)) if conj else (row == k)
        t_r, t_i = jnp.where(sel, p_r, t_r), jnp.where(sel, p_i, t_i)
    if conj:
        t_r, t_i = jnp.where(row == SUBLANES - 1, a_r, t_r), jnp.where(row == SUBLANES - 1, a_i, t_i)
    return row, (a_r, a_i), a2, a4, (t_r, t_i)


def _scan_group(xr, xi, row, a1, a2, a4, table, cr, ci, reverse):
    for s, (mr, mi) in ((1, a1), (2, a2), (4, a4)):
        if reverse:
            keep = row < SUBLANES - s
            sr, si = pltpu.roll(xr, SUBLANES - s, 0), pltpu.roll(xi, SUBLANES - s, 0)
        else:
            keep = row >= s
            sr, si = pltpu.roll(xr, s, 0), pltpu.roll(xi, s, 0)
        sr, si = jnp.where(keep, sr, 0.0), jnp.where(keep, si, 0.0)
        xr, xi = xr + mr * sr - mi * si, xi + mr * si + mi * sr
    tr, ti = table
    return xr + tr * cr - ti * ci, xi + tr * ci + ti * cr


def _s5_fwd(u, ar, ai, wr, wi, cr_w, ci_w, dsk, base, chunk=256):
    s_dim, d_dim = u.shape
    nb = d_dim // S5_BLOCK_CH
    n_st = nb * S5_BLOCK_ST
    chunk = _tile(s_dim, chunk, SUBLANES)
    nch = s_dim // chunk

    def body(u_ref, ar_ref, ai_ref, wr_ref, wi_ref, cr_ref, ci_ref, d_ref, y_ref, gy_ref, sr_ref, si_ref,
             xr_s, xi_s, car_r, car_i):
        @pl.when(pl.program_id(0) == 0)
        def _():
            car_r[...] = jnp.zeros_like(car_r)
            car_i[...] = jnp.zeros_like(car_i)

        sr_ref[...] = car_r[...]
        si_ref[...] = car_i[...]
        for b in range(nb):
            ub = u_ref[:, b * S5_BLOCK_CH:(b + 1) * S5_BLOCK_CH].astype(BF16)
            sl = slice(b * S5_BLOCK_ST, (b + 1) * S5_BLOCK_ST)
            xr_s[:, sl] = _dot(ub, wr_ref[b])
            xi_s[:, sl] = _dot(ub, wi_ref[b])
        for b in range(nb):
            sl = slice(b * S5_BLOCK_ST, (b + 1) * S5_BLOCK_ST)
            row, a1, a2, a4, table = _scan_tables(ar_ref, ai_ref, sl, False)

            def group(g, carry, sl=sl, row=row, a1=a1, a2=a2, a4=a4, table=table):
                rows = pl.ds(pl.multiple_of(g * SUBLANES, SUBLANES), SUBLANES)
                xr, xi = _scan_group(xr_s[rows, sl], xi_s[rows, sl], row, a1, a2, a4, table, carry[0], carry[1], False)
                xr_s[rows, sl] = xr
                xi_s[rows, sl] = xi
                last = (SUBLANES, S5_BLOCK_ST)
                return (jnp.broadcast_to(xr[SUBLANES - 1:, :], last), jnp.broadcast_to(xi[SUBLANES - 1:, :], last))

            c_r, c_i = lax.fori_loop(0, chunk // SUBLANES, group, (car_r[:, sl], car_i[:, sl]))
            car_r[:, sl] = c_r
            car_i[:, sl] = c_i
        for b in range(nb):
            sl = slice(b * S5_BLOCK_ST, (b + 1) * S5_BLOCK_ST)
            ch = slice(b * S5_BLOCK_CH, (b + 1) * S5_BLOCK_CH)
            y = _dot(xr_s[:, sl].astype(BF16), cr_ref[b]) - _dot(xi_s[:, sl].astype(BF16), ci_ref[b])
            y = y + d_ref[:, ch] * u_ref[:, ch]
            y_ref[:, ch] = y
            gy_ref[:, ch] = jax.nn.gelu(y).astype(BF16)

    rows = pl.BlockSpec((chunk, d_dim), lambda i: (i, 0))
    vec = pl.BlockSpec((1, n_st), lambda i: (0, 0))
    w_in = pl.BlockSpec((nb, S5_BLOCK_CH, S5_BLOCK_ST), lambda i: (0, 0, 0))
    w_out = pl.BlockSpec((nb, S5_BLOCK_ST, S5_BLOCK_CH), lambda i: (0, 0, 0))
    st = pl.BlockSpec((SUBLANES, n_st), lambda i: (i, 0))
    return pl.pallas_call(
        body, name=_name(base), grid=(nch,),
        in_specs=[rows, vec, vec, w_in, w_in, w_out, w_out, pl.BlockSpec((1, d_dim), lambda i: (0, 0))],
        out_specs=[rows, rows, st, st],
        out_shape=[SDS((s_dim, d_dim), F32), SDS((s_dim, d_dim), BF16),
                   SDS((nch * SUBLANES, n_st), F32), SDS((nch * SUBLANES, n_st), F32)],
        scratch_shapes=[pltpu.VMEM((chunk, n_st), F32), pltpu.VMEM((chunk, n_st), F32),
                        pltpu.VMEM((SUBLANES, n_st), F32), pltpu.VMEM((SUBLANES, n_st), F32)],
        compiler_params=_params(("arbitrary",)),
    )(u, ar, ai, wr, wi, cr_w, ci_w, dsk)


def _s5_bwd(u, dy, st_r, st_i, ar, ai, wr, wi, cr_w, ci_w, dsk, base, chunk):
    s_dim, d_dim = u.shape
    nb = d_dim // S5_BLOCK_CH
    n_st = nb * S5_BLOCK_ST
    nch = s_dim // chunk
    ng = chunk // SUBLANES

    def body(u_ref, dy_ref, sr_ref, si_ref, ar_ref, ai_ref, wr_ref, wi_ref, cr_ref, ci_ref, d_ref,
             du_ref, dwr_ref, dwi_ref, dcr_ref, dci_ref, dar_ref, dai_ref, dd_ref,
             xr_s, xi_s, lr_s, li_s, car_r, car_i):
        first = pl.program_id(0) == 0

        @pl.when(first)
        def _():
            car_r[...] = jnp.zeros_like(car_r)
            car_i[...] = jnp.zeros_like(car_i)
            for ref in (dwr_ref, dwi_ref, dcr_ref, dci_ref, dar_ref, dai_ref, dd_ref):
                ref[...] = jnp.zeros_like(ref)

        xr_s[pl.ds(0, SUBLANES), :] = sr_ref[...]
        xi_s[pl.ds(0, SUBLANES), :] = si_ref[...]
        body_rows = pl.ds(SUBLANES, chunk)
        dyb = dy_ref[...]
        dd_ref[...] += jnp.sum(dyb * u_ref[...], axis=0, keepdims=True)
        for b in range(nb):
            ch = slice(b * S5_BLOCK_CH, (b + 1) * S5_BLOCK_CH)
            sl = slice(b * S5_BLOCK_ST, (b + 1) * S5_BLOCK_ST)
            ub = u_ref[:, ch].astype(BF16)
            dyc = dyb[:, ch].astype(BF16)
            xr_s[body_rows, sl] = _dot(ub, wr_ref[b])
            xi_s[body_rows, sl] = _dot(ub, wi_ref[b])
            lr_s[:, sl] = _dot(dyc, cr_ref[b], "nt")
            li_s[:, sl] = -_dot(dyc, ci_ref[b], "nt")
        for b in range(nb):
            sl = slice(b * S5_BLOCK_ST, (b + 1) * S5_BLOCK_ST)
            row, a1, a2, a4, table = _scan_tables(ar_ref, ai_ref, sl, False)

            def fwd_group(g, carry, sl=sl, row=row, a1=a1, a2=a2, a4=a4, table=table):
                rows = pl.ds(pl.multiple_of((g + 1) * SUBLANES, SUBLANES), SUBLANES)
                xr, xi = _scan_group(xr_s[rows, sl], xi_s[rows, sl], row, a1, a2, a4, table, carry[0], carry[1], False)
                xr_s[rows, sl] = xr
                xi_s[rows, sl] = xi
                last = (SUBLANES, S5_BLOCK_ST)
                return (jnp.broadcast_to(xr[SUBLANES - 1:, :], last), jnp.broadcast_to(xi[SUBLANES - 1:, :], last))

            lax.fori_loop(0, ng, fwd_group, (sr_ref[:, sl], si_ref[:, sl]))
            row, a1, a2, a4, table = _scan_tables(ar_ref, ai_ref, sl, True)

            def bwd_group(n, carry, sl=sl, row=row, a1=a1, a2=a2, a4=a4, table=table):
                g = ng - 1 - n
                c_r, c_i, acc_r, acc_i = carry
                rows = pl.ds(pl.multiple_of(g * SUBLANES, SUBLANES), SUBLANES)
                lr, li = _scan_group(lr_s[rows, sl], li_s[rows, sl], row, a1, a2, a4, table, c_r, c_i, True)
                lr_s[rows, sl] = lr
                li_s[rows, sl] = li
                prev = pl.ds(pl.multiple_of(g * SUBLANES, SUBLANES), SUBLANES)
                here = pl.ds(pl.multiple_of((g + 1) * SUBLANES, SUBLANES), SUBLANES)
                px_r = jnp.where(row >= 1, pltpu.roll(xr_s[here, sl], 1, 0), pltpu.roll(xr_s[prev, sl], 1, 0))
                px_i = jnp.where(row >= 1, pltpu.roll(xi_s[here, sl], 1, 0), pltpu.roll(xi_s[prev, sl], 1, 0))
                acc_r = acc_r + lr * px_r + li * px_i
                acc_i = acc_i + li * px_r - lr * px_i
                first_row = (SUBLANES, S5_BLOCK_ST)
                return (jnp.broadcast_to(lr[:1, :], first_row), jnp.broadcast_to(li[:1, :], first_row), acc_r, acc_i)

            zero = jnp.zeros((SUBLANES, S5_BLOCK_ST), F32)
            c_r, c_i, acc_r, acc_i = lax.fori_loop(0, ng, bwd_group, (car_r[:, sl], car_i[:, sl], zero, zero))
            car_r[:, sl] = c_r
            car_i[:, sl] = c_i
            dar_ref[:, sl] += acc_r
            dai_ref[:, sl] += acc_i
        for b in range(nb):
            ch = slice(b * S5_BLOCK_CH, (b + 1) * S5_BLOCK_CH)
            sl = slice(b * S5_BLOCK_ST, (b + 1) * S5_BLOCK_ST)
            ub = u_ref[:, ch].astype(BF16)
            dyc = dyb[:, ch].astype(BF16)
            lrb, lib = lr_s[:, sl].astype(BF16), li_s[:, sl].astype(BF16)
            du_ref[:, ch] = (_dot(lrb, wr_ref[b], "nt") + _dot(lib, wi_ref[b], "nt") + d_ref[:, ch] * dyb[:, ch])
            dwr_ref[b] += _dot(ub, lrb, "tn")
            dwi_ref[b] += _dot(ub, lib, "tn")
            dcr_ref[b] += _dot(xr_s[body_rows, sl].astype(BF16), dyc, "tn")
            dci_ref[b] -= _dot(xi_s[body_rows, sl].astype(BF16), dyc, "tn")

    rev = lambda i: (nch - 1 - i, 0)
    rows = pl.BlockSpec((chunk, d_dim), rev)
    st = pl.BlockSpec((SUBLANES, n_st), rev)
    vec = pl.BlockSpec((1, n_st), lambda i: (0, 0))
    w_in = pl.BlockSpec((nb, S5_BLOCK_CH, S5_BLOCK_ST), lambda i: (0, 0, 0))
    w_out = pl.BlockSpec((nb, S5_BLOCK_ST, S5_BLOCK_CH), lambda i: (0, 0, 0))
    acc8 = pl.BlockSpec((SUBLANES, n_st), lambda i: (0, 0))
    dvec = pl.BlockSpec((1, d_dim), lambda i: (0, 0))
    return pl.pallas_call(
        body, name=_name(base), grid=(nch,),
        in_specs=[rows, rows, st, st, vec, vec, w_in, w_in, w_out, w_out, dvec],
        out_specs=[rows, w_in, w_in, w_out, w_out, acc8, acc8, dvec],
        out_shape=[SDS((s_dim, d_dim), F32),
                   SDS((nb, S5_BLOCK_CH, S5_BLOCK_ST), F32), SDS((nb, S5_BLOCK_CH, S5_BLOCK_ST), F32),
                   SDS((nb, S5_BLOCK_ST, S5_BLOCK_CH), F32), SDS((nb, S5_BLOCK_ST, S5_BLOCK_CH), F32),
                   SDS((SUBLANES, n_st), F32), SDS((SUBLANES, n_st), F32), SDS((1, d_dim), F32)],
        scratch_shapes=[pltpu.VMEM((chunk + SUBLANES, n_st), F32), pltpu.VMEM((chunk + SUBLANES, n_st), F32),
                        pltpu.VMEM((chunk, n_st), F32), pltpu.VMEM((chunk, n_st), F32),
                        pltpu.VMEM((SUBLANES, n_st), F32), pltpu.VMEM((SUBLANES, n_st), F32)],
        compiler_params=_params(("arbitrary",)),
    )(u, dy, st_r, st_i, ar, ai, wr, wi, cr_w, ci_w, dsk)


def _block_diag(w, transpose):
    g = w.shape[0]
    nb = g // S5_BLOCK_GROUPS
    eye = jnp.eye(S5_BLOCK_GROUPS, dtype=w.dtype)
    if transpose:
        w = w.reshape(nb, S5_BLOCK_GROUPS, S5_GROUP, S5_STATE)
        return jnp.einsum("bgcp,gh->bgphc", w, eye).reshape(nb, S5_BLOCK_ST, S5_BLOCK_CH)
    w = w.reshape(nb, S5_BLOCK_GROUPS, S5_STATE, S5_GROUP)
    return jnp.einsum("bgpc,gh->bgchp", w, eye).reshape(nb, S5_BLOCK_CH, S5_BLOCK_ST)


def _block_diag_extract(w, transpose):
    nb = w.shape[0]
    eye = jnp.eye(S5_BLOCK_GROUPS, dtype=w.dtype)
    if transpose:
        w = w.reshape(nb, S5_BLOCK_GROUPS, S5_STATE, S5_BLOCK_GROUPS, S5_GROUP)
        return jnp.einsum("bgphc,gh->bgcp", w, eye).reshape(nb * S5_BLOCK_GROUPS, S5_GROUP, S5_STATE)
    w = w.reshape(nb, S5_BLOCK_GROUPS, S5_GROUP, S5_BLOCK_GROUPS, S5_STATE)
    return jnp.einsum("bgchp,gh->bgpc", w, eye).reshape(nb * S5_BLOCK_GROUPS, S5_STATE, S5_GROUP)


def _dwconv_fwd(h, w, b, base, tr=512):
    s_dim, d_dim = h.shape
    tr = _tile(s_dim, tr, CONV_HALO)
    per = tr // CONV_HALO

    def body(h_ref, halo_ref, w_ref, b_ref, o_ref, buf):
        i = pl.program_id(0)
        buf[pl.ds(0, CONV_HALO), :] = jnp.where(i > 0, halo_ref[...], 0.0)
        buf[pl.ds(CONV_HALO, tr), :] = h_ref[...]
        acc = jnp.broadcast_to(b_ref[...], (tr, d_dim))
        for k in range(CONV_WIDTH):
            acc = acc + w_ref[pl.ds(k, 1), :] * buf[pl.ds(CONV_HALO - (CONV_WIDTH - 1) + k, tr), :]
        o_ref[...] = acc

    return pl.pallas_call(
        body, name=_name(base), grid=(s_dim // tr,),
        in_specs=[pl.BlockSpec((tr, d_dim), lambda i: (i, 0)),
                  pl.BlockSpec((CONV_HALO, d_dim), lambda i: (jnp.maximum(i * per - 1, 0), 0)),
                  pl.BlockSpec((CONV_WIDTH, d_dim), lambda i: (0, 0)), pl.BlockSpec((1, d_dim), lambda i: (0, 0))],
        out_specs=pl.BlockSpec((tr, d_dim), lambda i: (i, 0)), out_shape=SDS((s_dim, d_dim), F32),
        scratch_shapes=[pltpu.VMEM((tr + CONV_HALO, d_dim), F32)], compiler_params=_params(("arbitrary",)),
    )(h, h, w, b)


def _dwconv_bwd(h, dout, w, base, tr=512):
    s_dim, d_dim = h.shape
    tr = _tile(s_dim, tr, CONV_HALO)
    per = tr // CONV_HALO
    n = s_dim // tr
    last_halo = s_dim // CONV_HALO - 1

    def body(h_ref, hhalo_ref, d_ref, dhalo_ref, w_ref, dh_ref, dw_ref, db_ref, hbuf, dbuf):
        i = pl.program_id(0)
        hbuf[pl.ds(0, CONV_HALO), :] = jnp.where(i > 0, hhalo_ref[...], 0.0)
        hbuf[pl.ds(CONV_HALO, tr), :] = h_ref[...]
        dbuf[pl.ds(0, tr), :] = d_ref[...]
        dbuf[pl.ds(tr, CONV_HALO), :] = jnp.where(i < n - 1, dhalo_ref[...], 0.0)
        dout_t = d_ref[...]
        acc = jnp.zeros((tr, d_dim), F32)
        dws = []
        for k in range(CONV_WIDTH):
            acc = acc + w_ref[pl.ds(k, 1), :] * dbuf[pl.ds(CONV_WIDTH - 1 - k, tr), :]
            shifted = hbuf[pl.ds(CONV_HALO - (CONV_WIDTH - 1) + k, tr), :]
            dws.append(jnp.sum(dout_t * shifted, axis=0, keepdims=True))
        dh_ref[...] = acc
        dw_t = jnp.concatenate(dws, axis=0)
        db_t = jnp.sum(dout_t, axis=0, keepdims=True)

        @pl.when(i == 0)
        def _():
            dw_ref[...] = dw_t
            db_ref[...] = db_t

        @pl.when(i > 0)
        def _():
            dw_ref[...] += dw_t
            db_ref[...] += db_t

    tile = pl.BlockSpec((tr, d_dim), lambda i: (i, 0))
    before = pl.BlockSpec((CONV_HALO, d_dim), lambda i: (jnp.maximum(i * per - 1, 0), 0))
    after = pl.BlockSpec((CONV_HALO, d_dim), lambda i: (jnp.minimum((i + 1) * per, last_halo), 0))
    return pl.pallas_call(
        body, name=_name(base), grid=(n,),
        in_specs=[tile, before, tile, after, pl.BlockSpec((CONV_WIDTH, d_dim), lambda i: (0, 0))],
        out_specs=[tile, pl.BlockSpec((CONV_WIDTH, d_dim), lambda i: (0, 0)), pl.BlockSpec((1, d_dim), lambda i: (0, 0))],
        out_shape=[SDS((s_dim, d_dim), F32), SDS((CONV_WIDTH, d_dim), F32), SDS((1, d_dim), F32)],
        scratch_shapes=[pltpu.VMEM((tr + CONV_HALO, d_dim), F32), pltpu.VMEM((tr + CONV_HALO, d_dim), F32)],
        compiler_params=_params(("arbitrary",)),
    )(h, h, dout, dout, w)


def _local_step(x, mod, w, target):
    s_dim, d_dim = x.shape
    depth = mod.shape[0]
    row = lambda dt=F32, n=d_dim: SDS((s_dim, n), dt)
    vec = lambda n=d_dim: SDS((1, n), F32)
    mods = [[mod[l:l + 1, k * d_dim:(k + 1) * d_dim] for k in range(6)] for l in range(depth)]
    ng = lambda l, k: w["norm_g"][l, k:k + 1, :]
    saved = []
    h = x
    for l in range(depth):
        sh_m, sc_m, g_m, sh_f, sc_f, g_f = mods[l]
        kind, j = l % 3, l // 3
        s = {"h0": h}
        if kind == 1:
            (u,) = _rowwise(lambda h, g, a, b: ((_pre(h, g, a, b),), ()), [h], [ng(l, 0), sh_m, sc_m], [row()], [], "pre")
        else:
            (u,) = _rowwise(lambda h, g, a, b: ((_pre(h, g, a, b),), ()), [h], [ng(l, 0), sh_m, sc_m], [row(BF16)], [], "pre")
        s["u"] = u
        bias = None
        if kind == 0:
            qkv = _matmul(u, w["sb_w_qkv"][j], "nn", BF16, "qkv")
            o = _sb_fwd(qkv, "sb_fwd")
            m = _matmul(o, w["sb_w_o"][j], "nn", F32, "wo")
            s.update(qkv=qkv, o=o)
        elif kind == 1:
            p = w["s5"][j]
            y_pre, gy, st_r, st_i = _s5_fwd(u, p["ar"], p["ai"], p["wr"], p["wi"], p["cr"], p["ci"], w["s5_d"][j:j + 1],
                                            "s5_fwd")
            t = _matmul(gy, w["s5_w_glu"][j], "nn", F32, "s5_glu")
            b_glu = w["s5_b_glu"][j:j + 1]
            (m,) = _rowwise(lambda t, b: ((_glu(t, b),), ()), [t], [b_glu], [row()], [], "s5_gate")
            s.update(y_pre=y_pre, gy=gy, st_r=st_r, st_i=st_i, t=t)
        else:
            t = _matmul(u, w["cv_w_pw1"][j], "nn", F32, "pw1")
            (hg,) = _rowwise(lambda t, b: ((_glu(t, b),), ()), [t], [w["cv_b_pw1"][j:j + 1]], [row()], [], "cv_glu")
            hc = _dwconv_fwd(hg, w["cv_w_dw"][j], w["cv_b_dw"][j:j + 1], "dwconv")
            (hs,) = _rowwise(lambda h, g, b: ((_ln_silu(h, g, b),), ()), [hc],
                             [w["cv_ln_g"][j:j + 1], w["cv_ln_b"][j:j + 1]], [row(BF16)], [], "cv_ln")
            m = _matmul(hs, w["cv_w_pw2"][j], "nn", F32, "pw2")
            bias = w["cv_b_pw2"][j:j + 1]
            s.update(t=t, hg=hg, hc=hc, hs=hs)
        s["m"] = m
        if bias is None:
            h1, u2 = _rowwise(
                lambda h, m, gate, g1, g2, a, b: ((lambda h1: (h1, _pre(h1, g2, a, b)))(_post(h, m, gate, g1)), ()),
                [h, m], [g_m, ng(l, 1), ng(l, 2), sh_f, sc_f], [row(), row(BF16)], [], "post_pre")
        else:
            h1, u2 = _rowwise(
                lambda h, m, gate, g1, g2, a, b, bb: ((lambda h1: (h1, _pre(h1, g2, a, b)))(_post(h, m + bb, gate, g1)), ()),
                [h, m], [g_m, ng(l, 1), ng(l, 2), sh_f, sc_f, bias], [row(), row(BF16)], [], "post_pre")
        f_dim = w["ffn_w_gu"].shape[2] // 2
        gate, up, act = _ffn_up(u2, w["ffn_w_gu"][l], "ffn_up")
        f = _matmul(act, w["ffn_w_down"][l], "nn", F32, "ffn_down")
        (h,) = _rowwise(lambda h, m, gate, g: ((_post(h, m, gate, g),), ()), [h1, f], [g_f, ng(l, 3)], [row()], [], "post")
        s.update(h1=h1, u2=u2, gate=gate, up=up, act=act, f=f, bias=bias)
        saved.append(s)
    dh, loss_cols = _rowwise(
        lambda y, t: ((((y - t) * (1.0 / d_dim)),), (jnp.sum(jnp.square(y - t), axis=0, keepdims=True),)),
        [h, target], [], [row()], [vec()], "loss")
    grads = {}
    dmod = [[None] * 6 for _ in range(depth)]
    dnorm = [[None] * 4 for _ in range(depth)]

    def add(name, idx, val):
        grads.setdefault(name, {})[idx] = val

    for l in reversed(range(depth)):
        sh_m, sc_m, g_m, sh_f, sc_f, g_f = mods[l]
        kind, j = l % 3, l // 3
        s = saved[l]

        def post_bwd(h, m, dh, gate, g):
            _, vjp = jax.vjp(_post, h, m, gate, g)
            _, dm, dgate, dg = vjp(dh)
            return (dm,), (dgate, dg)

        df, dmod[l][5], dnorm[l][3] = _rowwise(post_bwd, [s["h1"], s["f"], dh], [g_f, ng(l, 3)],
                                               [row(BF16)], [vec(), vec()], "post_bwd")
        f_dim = s["act"].shape[1]
        add("ffn_w_down", l, _matmul(s["act"], df, "tn", F32, "d_down"))
        dact = _matmul(df, w["ffn_w_down"][l], "nt", BF16, "d_act")

        def act_bwd(gate, up, da):
            _, vjp = jax.vjp(_swish2, _f(gate), _f(up))
            return (jnp.concatenate(vjp(_f(da)), axis=1),), ()

        (dgu,) = _rowwise(act_bwd, [s["gate"], s["up"], dact], [], [row(BF16, 2 * f_dim)], [], "act_bwd")
        add("ffn_w_gu", l, _matmul(s["u2"], dgu, "tn", F32, "d_gu"))
        du2 = _matmul(dgu, w["ffn_w_gu"][l], "nt", F32, "d_u2")
        bias = s["bias"]

        def post_pre_bwd(h, m, dh1, du2, gate, g1, g2, a, b, *bb):
            def fwd(h, m, gate, g1, g2, a, b):
                mm = m + bb[0] if bb else m
                h1 = _post(h, mm, gate, g1)
                return h1, _pre(h1, g2, a, b)
            _, vjp = jax.vjp(fwd, h, m, gate, g1, g2, a, b)
            dh0, dm, dgate, dg1, dg2, da, db = vjp((dh1, du2))
            return (dh0, dm), (dgate, dg1, dg2, da, db, jnp.sum(dm, axis=0, keepdims=True))

        vecs = [g_m, ng(l, 1), ng(l, 2), sh_f, sc_f] + ([bias] if bias is not None else [])
        dh, dm, dmod[l][2], dnorm[l][1], dnorm[l][2], dmod[l][3], dmod[l][4], dm_cols = _rowwise(
            post_pre_bwd, [s["h0"], s["m"], dh, du2], vecs, [row(), row(BF16)], [vec()] * 6, "post_pre_bwd")
        if kind == 0:
            add("sb_w_o", j, _matmul(s["o"], dm, "tn", F32, "d_wo"))
            do = _matmul(dm, w["sb_w_o"][j], "nt", BF16, "d_o")
            dqkv = jnp.concatenate(_sb_bwd(s["qkv"], do, "sb_bwd"), axis=1).astype(BF16)
            add("sb_w_qkv", j, _matmul(s["u"], dqkv, "tn", F32, "d_wqkv"))
            du = _matmul(dqkv, w["sb_w_qkv"][j], "nt", F32, "d_u")
        elif kind == 1:
            p = w["s5"][j]
            b_glu = w["s5_b_glu"][j:j + 1]

            def gate_bwd(t, dm, b):
                dt, db = _glu_bwd(t, _f(dm), b)
                return (dt,), (db,)

            dt, db_glu = _rowwise(gate_bwd, [s["t"], dm], [b_glu], [row(BF16, 2 * d_dim)], [vec(2 * d_dim)], "s5_gate_bwd")
            add("s5_b_glu", j, db_glu)
            add("s5_w_glu", j, _matmul(s["gy"], dt, "tn", F32, "d_wglu"))
            dgy = _matmul(dt, w["s5_w_glu"][j], "nt", F32, "d_gy")

            def gelu_bwd(y, dg):
                _, vjp = jax.vjp(jax.nn.gelu, y)
                return (vjp(dg)[0],), ()

            (dy,) = _rowwise(gelu_bwd, [s["y_pre"], dgy], [], [row()], [], "gelu_bwd")
            chunk = s_dim // (s["st_r"].shape[0] // SUBLANES)
            du, dwr, dwi, dcr, dci, dar, dai, dd = _s5_bwd(
                s["u"], dy, s["st_r"], s["st_i"], p["ar"], p["ai"], p["wr"], p["wi"], p["cr"], p["ci"],
                w["s5_d"][j:j + 1], "s5_bwd", chunk)
            add("s5_d", j, dd)
            add("s5_raw", j, (dwr, dwi, dcr, dci, dar, dai))
        else:
            add("cv_w_pw2", j, _matmul(s["hs"], dm, "tn", F32, "d_pw2"))
            dhs = _matmul(dm, w["cv_w_pw2"][j], "nt", F32, "d_hs")
            ln_g, ln_b = w["cv_ln_g"][j:j + 1], w["cv_ln_b"][j:j + 1]

            def ln_bwd(hc, dhs, g, b):
                _, vjp = jax.vjp(_ln_silu, hc, g, b)
                dhc, dg, db = vjp(dhs)
                return (dhc,), (dg, db)

            dhc, dln_g, dln_b = _rowwise(ln_bwd, [s["hc"], dhs], [ln_g, ln_b], [row()], [vec()] * 2, "ln_bwd")
            add("cv_ln_g", j, dln_g)
            add("cv_ln_b", j, dln_b)
            add("cv_b_pw2", j, dm_cols)
            dhg, dw_dw, db_dw = _dwconv_bwd(s["hg"], dhc, w["cv_w_dw"][j], "dwconv_bwd")
            add("cv_w_dw", j, dw_dw)
            add("cv_b_dw", j, db_dw)
            b_pw1 = w["cv_b_pw1"][j:j + 1]

            def glu_bwd(t, dhg, b):
                dt, db = _glu_bwd(t, dhg, b)
                return (dt,), (db,)

            dt, db_pw1 = _rowwise(glu_bwd, [s["t"], dhg], [b_pw1], [row(BF16, 2 * d_dim)], [vec(2 * d_dim)], "cv_glu_bwd")
            add("cv_b_pw1", j, db_pw1)
            add("cv_w_pw1", j, _matmul(s["u"], dt, "tn", F32, "d_pw1"))
            du = _matmul(dt, w["cv_w_pw1"][j], "nt", F32, "d_u")

        def pre_bwd(h, dh, du, g, a, b):
            _, vjp = jax.vjp(_pre, h, g, a, b)
            dh0, dg, da, db = vjp(du)
            return (dh + dh0,), (dg, da, db)

        dh, dnorm[l][0], dmod[l][0], dmod[l][1] = _rowwise(pre_bwd, [s["h0"], dh, du], [ng(l, 0), sh_m, sc_m],
                                                            [row()], [vec()] * 3, "pre_bwd")
    grads["norm_g"] = jnp.stack([jnp.concatenate(r, axis=0) for r in dnorm])
    dmod_arr = jnp.concatenate([jnp.concatenate(r, axis=1) for r in dmod], axis=0)
    return loss_cols, dh, dmod_arr, grads


def _position():
    return lax.axis_index("x"), lax.axis_index("y"), lax.axis_index("c")


def _all_gather8(x, base):
    m, n = x.shape
    space = pltpu.VMEM

    def body(x_ref, out_ref, send_sems, recv_sems, local_sem):
        px, py, pc = _position()
        me, sibling = (px, py, pc), (px, py, 1 - pc)
        chips = [(1 - px, py), (px, 1 - py), (1 - px, 1 - py)]
        src_mine = x_ref

        def rows(bx, by, bc):
            return out_ref.at[pl.ds((4 * bx + 2 * by + bc) * m, m), :]

        def copy(k, block, to, src=None):
            return pltpu.make_async_remote_copy(
                src_ref=rows(*block) if src is None else src, dst_ref=rows(*block),
                send_sem=send_sems.at[k], recv_sem=recv_sems.at[k], device_id=to, device_id_type=MESH)

        mine = pltpu.make_async_copy(src_mine, rows(*me), local_sem)
        mine.start()
        first = [copy(0, me, sibling, src=src_mine)]
        first += [copy(1 + j, me, (*chip, pc), src=src_mine) for j, chip in enumerate(chips)]
        for cp in first:
            cp.start()
        passed = [copy(4 + j, (*chip, pc), sibling) for j, chip in enumerate(chips)]
        for j, chip in enumerate(chips):
            copy(1 + j, (*chip, pc), me).wait_recv()
            passed[j].start()
        copy(0, sibling, me).wait_recv()
        for j, chip in enumerate(chips):
            copy(4 + j, (*chip, 1 - pc), me).wait_recv()
        for cp in first + passed:
            cp.wait_send()
        mine.wait()

    return pl.pallas_call(
        body, name=_name(base), out_shape=SDS((8 * m, n), x.dtype),
        in_specs=[pl.BlockSpec(memory_space=space)], out_specs=pl.BlockSpec(memory_space=space),
        scratch_shapes=[pltpu.SemaphoreType.DMA((7,)), pltpu.SemaphoreType.DMA((7,)), pltpu.SemaphoreType.DMA],
    )(x)


_ANY = pl.BlockSpec(memory_space=pl.ANY)


def _half(ref, core, lead):
    k_half = ref.shape[lead] // 2
    return ref.at[(slice(None),) * lead + (pl.ds(core * k_half, k_half), slice(None))]


def _gather_weights(shards, base):
    nw = len(shards)

    def body(*refs):
        x_refs, out_refs = refs[:nw], refs[nw:2 * nw]
        send_sems, recv_sems, local_sems = refs[2 * nw:]
        px, py, pc = _position()
        me, sibling = (px, py, pc), (px, py, 1 - pc)
        chips = [(1 - px, py), (px, 1 - py), (1 - px, 1 - py)]

        def block(i, bx, by, bc):
            return _half(out_refs[i].at[2 * bx + by], bc, 1)

        def copy(i, k, blk, to, src=None):
            dst = block(i, *blk)
            return pltpu.make_async_remote_copy(
                src_ref=dst if src is None else src, dst_ref=dst, send_sem=send_sems.at[7 * i + k],
                recv_sem=recv_sems.at[7 * i + k], device_id=to, device_id_type=MESH)

        mine = [_half(x_refs[i], pc, 1) for i in range(nw)]
        local = [pltpu.make_async_copy(mine[i], block(i, *me), local_sems.at[i]) for i in range(nw)]
        for cp in local:
            cp.start()
        first = []
        for i in range(nw):
            first.append(copy(i, 0, me, sibling, src=mine[i]))
            first += [copy(i, 1 + j, me, (*chip, pc), src=mine[i]) for j, chip in enumerate(chips)]
        for cp in first:
            cp.start()
        passed = []
        for j, chip in enumerate(chips):
            for i in range(nw):
                copy(i, 1 + j, (*chip, pc), me).wait_recv()
                passed.append(copy(i, 4 + j, (*chip, pc), sibling))
                passed[-1].start()
        for i in range(nw):
            copy(i, 0, sibling, me).wait_recv()
        for j, chip in enumerate(chips):
            for i in range(nw):
                copy(i, 4 + j, (*chip, 1 - pc), me).wait_recv()
        for cp in first + passed:
            cp.wait_send()
        for cp in local:
            cp.wait()

    return pl.pallas_call(
        body, name=_name(base), out_shape=[SDS((4,) + s.shape, s.dtype) for s in shards],
        in_specs=[_ANY] * nw, out_specs=[_ANY] * nw,
        scratch_shapes=[pltpu.SemaphoreType.DMA((7 * nw,)), pltpu.SemaphoreType.DMA((7 * nw,)),
                        pltpu.SemaphoreType.DMA((nw,))],
    )(*shards)


def _reduce_to_sibling(gs, base):
    nw = len(gs)

    def body(*refs):
        g_refs, recv_refs, send_sems, recv_sems = refs[:nw], refs[nw:2 * nw], refs[2 * nw], refs[2 * nw + 1]
        px, py, pc = _position()
        copies = [pltpu.make_async_remote_copy(
            src_ref=_half(g_refs[i], 1 - pc, 2), dst_ref=recv_refs[i], send_sem=send_sems.at[i],
            recv_sem=recv_sems.at[i], device_id=(px, py, 1 - pc), device_id_type=MESH) for i in range(nw)]
        for cp in copies:
            cp.start()
        for cp in copies:
            cp.wait_recv()
        for cp in copies:
            cp.wait_send()

    outs = [SDS(g.shape[:2] + (g.shape[2] // 2, g.shape[3]), g.dtype) for g in gs]
    return pl.pallas_call(
        body, name=_name(base), out_shape=outs, in_specs=[_ANY] * nw, out_specs=[_ANY] * nw,
        scratch_shapes=[pltpu.SemaphoreType.DMA((nw,)), pltpu.SemaphoreType.DMA((nw,))],
    )(*gs)


def _exchange_chips(ts, base):
    nw = len(ts)

    def body(*refs):
        t_refs, recv_refs, send_sems, recv_sems = refs[:nw], refs[nw:2 * nw], refs[2 * nw], refs[2 * nw + 1]
        px, py, pc = _position()
        chips = [(1 - px, py), (px, 1 - py), (1 - px, 1 - py)]
        copies = [pltpu.make_async_remote_copy(
            src_ref=t_refs[i].at[2 * cx + cy], dst_ref=recv_refs[i].at[k], send_sem=send_sems.at[3 * i + k],
            recv_sem=recv_sems.at[3 * i + k], device_id=(cx, cy, pc), device_id_type=MESH)
            for k, (cx, cy) in enumerate(chips) for i in range(nw)]
        for cp in copies:
            cp.start()
        for cp in copies:
            cp.wait_recv()
        for cp in copies:
            cp.wait_send()

    outs = [SDS((3,) + t.shape[1:], t.dtype) for t in ts]
    return pl.pallas_call(
        body, name=_name(base), out_shape=outs, in_specs=[_ANY] * nw, out_specs=[_ANY] * nw,
        scratch_shapes=[pltpu.SemaphoreType.DMA((3 * nw,)), pltpu.SemaphoreType.DMA((3 * nw,))],
    )(*ts)


def _share_with_sibling(rs, base):
    nw = len(rs)

    def body(*refs):
        in_refs, out_refs, send_sems, recv_sems = refs[:nw], refs[nw:2 * nw], refs[2 * nw], refs[2 * nw + 1]
        px, py, pc = _position()
        send = [pltpu.make_async_remote_copy(
            src_ref=_half(in_refs[i], pc, 1), dst_ref=_half(out_refs[i], pc, 1), send_sem=send_sems.at[i],
            recv_sem=recv_sems.at[i], device_id=(px, py, 1 - pc), device_id_type=MESH) for i in range(nw)]
        for cp in send:
            cp.start()
        for i in range(nw):
            pltpu.make_async_remote_copy(
                src_ref=_half(in_refs[i], 1 - pc, 1), dst_ref=_half(out_refs[i], 1 - pc, 1), send_sem=send_sems.at[i],
                recv_sem=recv_sems.at[i], device_id=(px, py, 1 - pc), device_id_type=MESH).wait_recv()
        for cp in send:
            cp.wait_send()

    return pl.pallas_call(
        body, name=_name(base), out_shape=[SDS(r.shape, r.dtype) for r in rs], in_specs=[_ANY] * nw,
        out_specs=[_ANY] * nw, input_output_aliases={i: i for i in range(nw)},
        scratch_shapes=[pltpu.SemaphoreType.DMA((nw,)), pltpu.SemaphoreType.DMA((nw,))],
    )(*rs)


def _add_own_half(g, recv, core, base, tr=512):
    _, l_dim, k_half, n = recv.shape
    tr = _tile(k_half, tr, 2 * SUBLANES)
    per = k_half // tr

    def body(core_ref, g_ref, r_ref, o_ref):
        o_ref[...] = (g_ref[...] + r_ref[...]).astype(o_ref.dtype)

    blk = (1, 1, tr, n)
    spec = pltpu.PrefetchScalarGridSpec(
        num_scalar_prefetch=1, grid=(4, l_dim, per),
        in_specs=[pl.BlockSpec(blk, lambda j, l, i, core: (j, l, core[0] * per + i, 0)),
                  pl.BlockSpec(blk, lambda j, l, i, core: (j, l, i, 0))],
        out_specs=pl.BlockSpec(blk, lambda j, l, i, core: (j, l, i, 0)))
    return pl.pallas_call(body, name=_name(base), grid_spec=spec, out_shape=SDS(recv.shape, BF16),
                          compiler_params=_params(("parallel", "parallel", "parallel")))(core, g, recv)


def _add_chips(g, from_sibling, recv, chip_core, base, tr=512):
    _, l_dim, k_half, n = from_sibling.shape
    tr = _tile(k_half, tr, 2 * SUBLANES)
    per = k_half // tr

    def body(idx_ref, g_ref, s_ref, r_ref, o_ref):
        own = g_ref[0, 0] + s_ref[0, 0]
        o_ref[0] = ((own + _f(r_ref[0, 0])) + _f(r_ref[1, 0])) + _f(r_ref[2, 0])

    spec = pltpu.PrefetchScalarGridSpec(
        num_scalar_prefetch=1, grid=(l_dim, per),
        in_specs=[pl.BlockSpec((1, 1, tr, n), lambda l, i, idx: (idx[0], l, idx[1] * per + i, 0)),
                  pl.BlockSpec((1, 1, tr, n), lambda l, i, idx: (idx[0], l, i, 0)),
                  pl.BlockSpec((3, 1, tr, n), lambda l, i, idx: (0, l, i, 0))],
        out_specs=pl.BlockSpec((1, tr, n), lambda l, i, idx: (l, idx[1] * per + i, 0)))
    return pl.pallas_call(body, name=_name(base), grid_spec=spec, out_shape=SDS((l_dim, 2 * k_half, n), F32),
                          compiler_params=_params(("parallel", "parallel")))(chip_core, g, from_sibling, recv)


def _sum8(g, base):
    r, n = g.shape[0] // 8, g.shape[1]

    def body(g_ref, o_ref):
        acc = g_ref[pl.ds(0, r), :]
        for d in range(1, 8):
            acc = acc + g_ref[pl.ds(d * r, r), :]
        o_ref[...] = acc

    return pl.pallas_call(body, name=_name(base), out_shape=SDS((r, n), F32), compiler_params=_params())(g)


def _adamw_math(w, g, m, v):
    m = ADAM_B1 * m + (1.0 - ADAM_B1) * g
    v = ADAM_B2 * v + (1.0 - ADAM_B2) * jnp.square(g)
    m_hat = m / (1.0 - ADAM_B1 ** ADAM_STEP)
    v_hat = v / (1.0 - ADAM_B2 ** ADAM_STEP)
    delta = -ADAM_LR * (m_hat / (jnp.sqrt(v_hat) + ADAM_EPS) + ADAM_WD * w)
    return delta, m, v


def _adamw(w, g, m, v, base):
    out = SDS(w.shape, F32)
    return _rowwise(lambda w, g, m, v: (_adamw_math(w, g, m, v), ()), [w, g, m, v], [], [out, out, out], [], base)


def _mod_grad_adamw(sc_t, dmod, w, m, v, base, tr=256):
    l_dim, d_dim, n = w.shape
    tr = _tile(d_dim, tr, SUBLANES)

    def body(sc_ref, dm_ref, w_ref, m_ref, v_ref, g_ref, d_ref, nm_ref, nv_ref):
        g = lax.dot_general(sc_ref[...], dm_ref[0], _DIMS["nn"], preferred_element_type=F32,
                            precision=lax.Precision.HIGHEST)
        delta, nm, nv = _adamw_math(w_ref[0], g, m_ref[0], v_ref[0])
        g_ref[0], d_ref[0], nm_ref[0], nv_ref[0] = g, delta, nm, nv

    tile = pl.BlockSpec((1, tr, n), lambda l, i: (l, i, 0))
    out = SDS(w.shape, F32)
    return pl.pallas_call(
        body, name=_name(base), grid=(l_dim, d_dim // tr),
        in_specs=[pl.BlockSpec((tr, 8), lambda l, i: (i, 0)), pl.BlockSpec((1, 8, n), lambda l, i: (l, 0, 0)),
                  tile, tile, tile],
        out_specs=[tile] * 4, out_shape=[out] * 4, compiler_params=_params(("parallel", "parallel")),
    )(sc_t, dmod, w, m, v)


def _mod_forward(c_all, w_mod, b_mod, base):
    l_dim, d_dim, n = w_mod.shape

    def body(c_ref, w_ref, b_ref, o_ref):
        o_ref[0] = lax.dot_general(jax.nn.silu(c_ref[...]), w_ref[0], _DIMS["nn"], preferred_element_type=F32,
                                   precision=lax.Precision.HIGHEST) + b_ref[0]

    return pl.pallas_call(
        body, name=_name(base), grid=(l_dim,),
        in_specs=[pl.BlockSpec((8, d_dim), lambda l: (0, 0)), pl.BlockSpec((1, d_dim, n), lambda l: (l, 0, 0)),
                  pl.BlockSpec((1, 1, n), lambda l: (l, 0, 0))],
        out_specs=pl.BlockSpec((1, 8, n), lambda l: (l, 0, 0)), out_shape=SDS((l_dim, 8, n), F32),
        compiler_params=_params(("parallel",)),
    )(c_all, w_mod, b_mod)


def _pack(arrays, dtype, row_mult):
    flat = jnp.concatenate([a.reshape(-1).astype(dtype) for a in arrays])
    quantum = PACK_LANES * row_mult
    pad = (-flat.shape[0]) % quantum
    return jnp.pad(flat, (0, pad)).reshape(-1, PACK_LANES)


def _unpack(flat, shapes):
    out, off = [], 0
    for shp in shapes:
        n = math.prod(shp)
        out.append(flat[off:off + n].reshape(shp))
        off += n
    return out


_WEIGHTS = ["norm_g", "w_mod", "b_mod", "sb_w_qkv", "sb_w_o", "s5_lam_re", "s5_lam_im", "s5_log_dt", "s5_b_re",
            "s5_b_im", "s5_c_re", "s5_c_im", "s5_d", "s5_w_glu", "s5_b_glu", "cv_w_pw1", "cv_b_pw1", "cv_w_dw",
            "cv_b_dw", "cv_ln_g", "cv_ln_b", "cv_w_pw2", "cv_b_pw2", "ffn_w_gate", "ffn_w_up", "ffn_w_down"]
_BIG_COL = ["sb_w_qkv", "s5_w_glu", "cv_w_pw1", "ffn_w_gate", "ffn_w_up"]
_BIG_ROW = ["sb_w_o", "cv_w_pw2", "ffn_w_down"]
_BIG = _BIG_COL + _BIG_ROW
_SMALL_SHARDED = {"norm_g": 2, "cv_b_pw1": 1, "cv_w_dw": 2, "cv_b_dw": 1, "cv_ln_g": 1, "cv_ln_b": 1, "cv_b_pw2": 1}
_REPLICATED = ["b_mod", "s5_lam_re", "s5_lam_im", "s5_log_dt", "s5_b_re", "s5_b_im", "s5_c_re", "s5_c_im", "s5_d",
               "s5_b_glu"]


def _full_from_shards(stacked, name):
    if name in _BIG_COL or (name in _SMALL_SHARDED):
        axis = stacked.ndim - 1
    else:
        axis = stacked.ndim - 2
    moved = jnp.moveaxis(stacked, 0, axis - 1)
    shp = list(moved.shape)
    shp[axis - 1:axis + 1] = [shp[axis - 1] * shp[axis]]
    return moved.reshape(shp)


def _shards_from_full(full, name):
    axis = full.ndim - 1 if (name in _BIG_COL or name in _SMALL_SHARDED) else full.ndim - 2
    shp = list(full.shape)
    shp[axis:axis + 1] = [4, shp[axis] // 4]
    return jnp.moveaxis(full.reshape(shp), axis, 0)


def kernel(x, c, norm_g, w_mod, b_mod, sb_w_qkv, sb_w_o, s5_lam_re, s5_lam_im, s5_log_dt, s5_b_re, s5_b_im, s5_c_re, s5_c_im, s5_d, s5_w_glu, s5_b_glu, cv_w_pw1, cv_b_pw1, cv_w_dw, cv_b_dw, cv_ln_g, cv_ln_b, cv_w_pw2, cv_b_pw2, ffn_w_gate, ffn_w_up, ffn_w_down, loss_target, m_norm_g, m_w_mod, m_b_mod, m_sb_w_qkv, m_sb_w_o, m_s5_lam_re, m_s5_lam_im, m_s5_log_dt, m_s5_b_re, m_s5_b_im, m_s5_c_re, m_s5_c_im, m_s5_d, m_s5_w_glu, m_s5_b_glu, m_cv_w_pw1, m_cv_b_pw1, m_cv_w_dw, m_cv_b_dw, m_cv_ln_g, m_cv_ln_b, m_cv_w_pw2, m_cv_b_pw2, m_ffn_w_gate, m_ffn_w_up, m_ffn_w_down, v_norm_g, v_w_mod, v_b_mod, v_sb_w_qkv, v_sb_w_o, v_s5_lam_re, v_s5_lam_im, v_s5_log_dt, v_s5_b_re, v_s5_b_im, v_s5_c_re, v_s5_c_im, v_s5_d, v_s5_w_glu, v_s5_b_glu, v_cv_w_pw1, v_cv_b_pw1, v_cv_w_dw, v_cv_b_dw, v_cv_ln_g, v_cv_ln_b, v_cv_w_pw2, v_cv_b_pw2, v_ffn_w_gate, v_ffn_w_up, v_ffn_w_down):
    args = locals()
    wts = {n: args[n] for n in _WEIGHTS}
    mom = {n: args["m_" + n] for n in _WEIGHTS}
    var = {n: args["v_" + n] for n in _WEIGHTS}
    px, py, pc = lax.axis_index("x"), lax.axis_index("y"), lax.axis_index("c")
    chip = 2 * px + py
    dev = 2 * chip + pc
    s_dim, d_dim = x.shape[1], x.shape[2]
    depth = norm_g.shape[0]
    x2, tgt = x[0], loss_target[0]

    small_names = list(_SMALL_SHARDED)
    small_in = _pack([c] + [wts[n] for n in small_names], F32, SUBLANES)
    small_all = _all_gather8(small_in, "gather_small").reshape(8, -1)
    c_all = small_all[:, :d_dim]
    full = {}
    off = d_dim
    for n in small_names:
        size = math.prod(wts[n].shape)
        stacked = small_all[0::2, off:off + size].reshape((4,) + wts[n].shape)
        full[n] = _full_from_shards(stacked, n)
        off += size

    n_mod = w_mod.shape[2]
    b_mod_cols = lax.dynamic_slice_in_dim(b_mod, chip * n_mod, n_mod, axis=1)[:, None, :]
    mod_part = _mod_forward(c_all, w_mod, b_mod_cols, "mod_fwd")
    mod_all = _all_gather8(mod_part.reshape(depth * 8, n_mod), "gather_mod")
    mod_all = mod_all.reshape(4, 2, depth, 8, n_mod)[:, 0]
    mod_mine = lax.dynamic_index_in_dim(mod_all, dev, axis=2, keepdims=False)
    mod = mod_mine.transpose(1, 0, 2).reshape(depth, 4 * n_mod)

    stacked = _gather_weights([wts[n].astype(BF16) for n in _BIG], "gather_big")
    for n, st in zip(_BIG, stacked, strict=True):
        full[n] = _full_from_shards(st, n)
    full["ffn_w_gu"] = jnp.concatenate([full.pop("ffn_w_gate"), full.pop("ffn_w_up")], axis=2)
    for n in _REPLICATED:
        full[n] = wts[n]

    n_s5 = s5_lam_re.shape[0]
    gp = s5_lam_re.shape[1] * s5_lam_re.shape[2]
    col = lambda a: a.reshape(gp, 1)
    s5_raw_in = []
    full["s5"] = []
    for j in range(n_s5):
        ldt = jnp.broadcast_to(s5_log_dt[j][:, None], s5_lam_re[j].shape)
        raw = (col(s5_lam_re[j]), col(s5_lam_im[j]), col(ldt), s5_b_re[j].reshape(gp, S5_GROUP), s5_b_im[j].reshape(gp, S5_GROUP))
        ar, ai, bbr, bbi = _s5_prep(*raw, "s5_prep")
        s5_raw_in.append(raw)
        shape3 = s5_b_re[j].shape
        full["s5"].append(dict(
            ar=ar.reshape(1, gp), ai=ai.reshape(1, gp),
            wr=_block_diag(bbr.reshape(shape3), False).astype(BF16), wi=_block_diag(bbi.reshape(shape3), False).astype(BF16),
            cr=_block_diag(s5_c_re[j], True).astype(BF16), ci=_block_diag(s5_c_im[j], True).astype(BF16)))

    loss_cols, grad_x, dmod, grads = _local_step(x2, mod, full, tgt)
    loss = lax.psum(0.5 * jnp.sum(loss_cols) / d_dim, ("x", "y", "c"))

    small_grads = {}
    for name in ("s5_lam_re", "s5_lam_im", "s5_log_dt", "s5_b_re", "s5_b_im", "s5_c_re", "s5_c_im"):
        small_grads[name] = []
    for j in range(n_s5):
        dwr, dwi, dcr, dci, dar, dai = grads["s5_raw"][j]
        shape3 = s5_b_re[j].shape
        cots = (jnp.sum(dar, axis=0).reshape(gp, 1), jnp.sum(dai, axis=0).reshape(gp, 1),
                _block_diag_extract(dwr, False).reshape(gp, S5_GROUP), _block_diag_extract(dwi, False).reshape(gp, S5_GROUP))
        dlr, dli, dldt, dbr, dbi = _s5_prep_bwd(*s5_raw_in[j], cots, "s5_prep_bwd")
        small_grads["s5_lam_re"].append(dlr.reshape(s5_lam_re[j].shape))
        small_grads["s5_lam_im"].append(dli.reshape(s5_lam_re[j].shape))
        small_grads["s5_log_dt"].append(jnp.sum(dldt.reshape(s5_lam_re[j].shape), axis=1))
        small_grads["s5_b_re"].append(dbr.reshape(shape3))
        small_grads["s5_b_im"].append(dbi.reshape(shape3))
        small_grads["s5_c_re"].append(_block_diag_extract(dcr, True))
        small_grads["s5_c_im"].append(_block_diag_extract(dci, True))
    small_grads = {n: jnp.stack(v) for n, v in small_grads.items()}
    stack = lambda name, count: jnp.stack([grads[name][j] for j in range(count)])
    small_grads["s5_d"] = stack("s5_d", n_s5)[:, 0]
    small_grads["s5_b_glu"] = stack("s5_b_glu", n_s5)[:, 0]
    n_cv = cv_w_pw1.shape[0]
    for name in ("cv_b_pw1", "cv_b_dw", "cv_ln_g", "cv_ln_b", "cv_b_pw2"):
        small_grads[name] = stack(name, n_cv)[:, 0]
    small_grads["cv_w_dw"] = stack("cv_w_dw", n_cv)
    small_grads["norm_g"] = grads["norm_g"]

    small_order = ["norm_g", "s5_lam_re", "s5_lam_im", "s5_log_dt", "s5_b_re", "s5_b_im", "s5_c_re", "s5_c_im", "s5_d",
                   "s5_b_glu", "cv_b_pw1", "cv_w_dw", "cv_b_dw", "cv_ln_g", "cv_ln_b", "cv_b_pw2"]
    small_shapes = [small_grads[n].shape for n in small_order]
    sg_in = _pack([dmod] + [small_grads[n] for n in small_order], F32, SUBLANES)
    sg_all = _all_gather8(sg_in, "gather_small_grads")
    sg_sum = _sum8(sg_all, "sum_small_grads").reshape(-1)
    dmod_all = sg_all.reshape(8, -1)[:, :dmod.size].reshape(8, depth, 4, n_mod)
    g_full = dict(zip(small_order, _unpack(sg_sum[dmod.size:], small_shapes), strict=True))
    g_full["b_mod"] = sg_sum[:dmod.size].reshape(dmod.shape)

    gu = stack("ffn_w_gu", depth)
    f_dim = gu.shape[2] // 2
    big_grads = {"sb_w_qkv": stack("sb_w_qkv", sb_w_qkv.shape[0]), "sb_w_o": stack("sb_w_o", sb_w_o.shape[0]),
                 "s5_w_glu": stack("s5_w_glu", n_s5), "cv_w_pw1": stack("cv_w_pw1", n_cv), "cv_w_pw2": stack("cv_w_pw2", n_cv),
                 "ffn_w_gate": gu[:, :, :f_dim], "ffn_w_up": gu[:, :, f_dim:], "ffn_w_down": stack("ffn_w_down", depth)}
    chip_major = [_shards_from_full(big_grads[n], n) for n in _BIG]
    core_idx, chip_core = pc.reshape(1).astype(I32), jnp.stack([chip, pc]).astype(I32)
    from_sibling = _reduce_to_sibling(chip_major, "reduce_sibling")
    chip_sums = [_add_own_half(g, r, core_idx, "reduce_add_sibling") for g, r in zip(chip_major, from_sibling, strict=True)]
    from_chips = _exchange_chips(chip_sums, "reduce_chips")
    my_half = [_add_chips(g, s, r, chip_core, "reduce_add_chips")
               for g, s, r in zip(chip_major, from_sibling, from_chips, strict=True)]
    g_shard = dict(zip(_BIG, _share_with_sibling(my_half, "reduce_share"), strict=True))

    out_g, out_d, out_m, out_v = {}, {}, {}, {}
    for n in _BIG:
        shp = wts[n].shape
        two = lambda a: a.reshape(-1, shp[-1])
        out_g[n] = g_shard[n]
        d, nm, nv = _adamw(two(wts[n]), two(g_shard[n]), two(mom[n]), two(var[n]), "adamw")
        out_d[n], out_m[n], out_v[n] = d.reshape(shp), nm.reshape(shp), nv.reshape(shp)
    sc_t = jax.nn.silu(c_all).T
    dmod_cols = lax.dynamic_index_in_dim(dmod_all, chip, axis=2, keepdims=False).transpose(1, 0, 2)
    out_g["w_mod"], out_d["w_mod"], out_m["w_mod"], out_v["w_mod"] = _mod_grad_adamw(
        sc_t, dmod_cols, w_mod, m_w_mod, v_w_mod, "w_mod_adamw")
    small_all_names = [n for n in _WEIGHTS if n not in _BIG and n != "w_mod"]
    local_g = []
    for n in small_all_names:
        g = g_full[n]
        if n in _SMALL_SHARDED:
            width = wts[n].shape[-1]
            g = lax.dynamic_slice_in_dim(g, chip * width, width, axis=g.ndim - 1)
        local_g.append(g)
        out_g[n] = g
    pk = lambda arrs: _pack(arrs, F32, SUBLANES)
    d, nm, nv = _adamw(pk([wts[n] for n in small_all_names]), pk(local_g), pk([mom[n] for n in small_all_names]),
                       pk([var[n] for n in small_all_names]), "adamw_small")
    shapes = [wts[n].shape for n in small_all_names]
    for n, a, b, cc in zip(small_all_names, _unpack(d.reshape(-1), shapes), _unpack(nm.reshape(-1), shapes),
                           _unpack(nv.reshape(-1), shapes), strict=True):
        out_d[n], out_m[n], out_v[n] = a, b, cc
    return (loss, grad_x[None], *[out_g[n] for n in _WEIGHTS], *[out_d[n] for n in _WEIGHTS],
            *[out_m[n] for n in _WEIGHTS], *[out_v[n] for n in _WEIGHTS])
```
